```python
import math
import jax
import jax.numpy as jnp
from jax import lax
import numpy as np

D_MODEL = 2048
BATCH = 4
SEQ = 2048
DEPTH = 4

CHUNK = 64
MEM_LEN = 256
Q_BLOCK = 128
N_EVEN = (DEPTH + 1) // 2
N_ODD = DEPTH // 2

GMLP_CHUNK = 128
A_WIDTH = D_MODEL // 2
A_HEADS = 8
A_HEAD_DIM = A_WIDTH // A_HEADS
B_WIDTH = D_MODEL // 2
POOL_WINDOWS = (2, 4, 8, 16)
N_POOL = len(POOL_WINDOWS)
B_GROUP = B_WIDTH // N_POOL
MIX_IN = 2 * A_WIDTH + B_WIDTH
C_HEADS = 16
C_HEAD_DIM = D_MODEL // (2 * C_HEADS)
C_V_DIM = 2 * C_HEAD_DIM
REL_BUCKETS = 32
REL_MAX_DIST = 128
X_HEADS = 4
X_HEAD_DIM = D_MODEL // X_HEADS
D_FF = ((8 * D_MODEL) // 3 + 255) // 256 * 256
N_EXPERTS = 8
TOP_K = 2
D_FF_EXPERT = D_FF // 2
EXPERT_BLOCK = 256
DN_ALPHA = (2 * DEPTH) ** 0.25
DN_BETA = (8 * DEPTH) ** -0.25
LN_EPS = 1e-5
NEG = -1e30

kernel_name = 'hybrid_gmlp_pool_diffattn_moe_trunk'


def layer_norm(x, g, b):
    xf = x.astype(jnp.float32)
    mu = jnp.mean(xf, -1, keepdims=True)
    var = jnp.mean(jnp.square(xf - mu), -1, keepdims=True)
    y = (xf - mu) * lax.rsqrt(var + LN_EPS)
    return (y * g.astype(jnp.float32) + b.astype(jnp.float32)).astype(x.dtype)


def rms_norm(x, g):
    xf = x.astype(jnp.float32)
    y = xf * lax.rsqrt(jnp.mean(xf * xf, -1, keepdims=True) + LN_EPS)
    return (y * g.astype(jnp.float32)).astype(x.dtype)


def t5_bucket(rel):
    half = REL_BUCKETS // 2
    max_exact = half // 2
    ret = jnp.where(rel > 0, half, 0)
    n = jnp.abs(rel)
    nf = jnp.maximum(n, 1).astype(jnp.float32)
    large = max_exact + (jnp.log(nf / max_exact) / math.log(REL_MAX_DIST / max_exact) * (half - max_exact)).astype(jnp.int32)
    large = jnp.minimum(large, half - 1)
    return ret + jnp.where(n < max_exact, n, large)


def gmlp_pool_mixer(x, w_in, ln_v_g, ln_v_b, w_s, b_s, w_pool, pool_scale, w_out):
    bsz, S, _ = x.shape
    h = x @ w_in
    z = jax.nn.gelu(h[..., :2 * A_WIDTH])
    u, v = z[..., :A_WIDTH], z[..., A_WIDTH:]
    v = layer_norm(v, ln_v_g, ln_v_b)
    nc = S // GMLP_CHUNK
    vc = v.reshape(bsz, nc, GMLP_CHUNK, A_HEADS, A_HEAD_DIM)
    pos = jnp.arange(GMLP_CHUNK)
    allowed = (pos[None, :] // CHUNK) <= (pos[:, None] // CHUNK)
    ws = jnp.where(allowed[None], w_s, jnp.zeros((), w_s.dtype))
    sv = jnp.einsum('hpq,bnqhc->bnphc', ws, vc) + b_s.T[:, :, None]
    a_out = u * sv.reshape(bsz, S, A_WIDTH)
    pin = h[..., 2 * A_WIDTH:].reshape(bsz, S, N_POOL, B_GROUP)
    cs = jnp.cumsum(pin.astype(jnp.float32), axis=1)
    t1 = jnp.arange(1, S + 1)
    pooled = []
    for gi, w in enumerate(POOL_WINDOWS):
        c = cs[:, :, gi]
        lag = jnp.pad(c[:, :S - w], ((0, 0), (w, 0), (0, 0)))
        cnt = jnp.minimum(t1, w).astype(jnp.float32)[None, :, None]
        pooled.append((c - lag) / cnt)
    pooled = jnp.stack(pooled, axis=2).astype(x.dtype) - pin
    b_out = jnp.einsum('bsgc,gcd->bsgd', pooled, w_pool).reshape(bsz, S, B_WIDTH) * pool_scale
    return jnp.concatenate([a_out, b_out], axis=-1) @ w_out


def diff_attention(x, w_qkv, lam_q1, lam_k1, lam_q2, lam_k2, subln_g, w_o, rel_table, lam_init):
    bsz, S, _ = x.shape
    qkv = x @ w_qkv
    q = qkv[..., :D_MODEL].reshape(bsz, S, C_HEADS, 2, C_HEAD_DIM)
    k = qkv[..., D_MODEL:2 * D_MODEL].reshape(bsz, S, C_HEADS, 2, C_HEAD_DIM)
    v = qkv[..., 2 * D_MODEL:].reshape(bsz, S, C_HEADS, C_V_DIM)
    lam = (jnp.exp(jnp.sum(lam_q1 * lam_k1).astype(jnp.float32))
           - jnp.exp(jnp.sum(lam_q2 * lam_k2).astype(jnp.float32)) + lam_init)
    scale = C_HEAD_DIM ** -0.5
    outs = []
    for qb in range(S // Q_BLOCK):
        q0 = qb * Q_BLOCK
        kend = q0 + Q_BLOCK
        qi = jnp.arange(q0, kend)
        kj = jnp.arange(kend)
        bias = jnp.transpose(rel_table[t5_bucket(kj[None, :] - qi[:, None])], (2, 0, 1)).astype(jnp.float32)
        allowed = (kj[None, :] // CHUNK) <= (qi[:, None] // CHUNK)
        s = jnp.einsum('bqhmd,bkhmd->bmhqk', q[:, q0:kend], k[:, :kend]).astype(jnp.float32) * scale + bias
        s = jnp.where(allowed, s, NEG)
        p = jax.nn.softmax(s, axis=-1)
        p = p[:, 0] - lam * p[:, 1]
        outs.append(jnp.einsum('bhqk,bkhe->bqhe', p.astype(v.dtype), v[:, :kend]))
    o = jnp.concatenate(outs, axis=1)
    o = rms_norm(o, subln_g) * (1.0 - lam_init)
    return o.reshape(bsz, S, D_MODEL) @ w_o


def memory_cross_attention(x, mem, w_q, w_kv, w_o):
    bsz, S, _ = x.shape
    q = (x @ w_q).reshape(bsz, S, X_HEADS, X_HEAD_DIM)
    kv = mem @ w_kv
    k = kv[..., :D_MODEL].reshape(bsz, -1, X_HEADS, X_HEAD_DIM)
    v = kv[..., D_MODEL:].reshape(bsz, -1, X_HEADS, X_HEAD_DIM)
    s = jnp.einsum('bqhd,bkhd->bhqk', q, k).astype(jnp.float32) * (X_HEAD_DIM ** -0.5)
    p = jax.nn.softmax(s, axis=-1)
    o = jnp.einsum('bhqk,bkhd->bqhd', p.astype(v.dtype), v)
    return o.reshape(bsz, S, D_MODEL) @ w_o


def swiglu(x, w_gate, w_up, w_down):
    return (jax.nn.silu(x @ w_gate) * (x @ w_up)) @ w_down


def moe_swiglu(x, w_router, w_gate, w_up, w_down):
    bsz, S, D = x.shape
    xf = x.reshape(-1, D)
    N = xf.shape[0]
    logits = (xf @ w_router).astype(jnp.float32)
    top_logits, top_idx = lax.top_k(logits, TOP_K)
    top_w = jax.nn.softmax(top_logits, axis=-1)
    M = N * TOP_K
    flat_e = top_idx.reshape(-1)
    flat_tok = jnp.repeat(jnp.arange(N, dtype=jnp.int32), TOP_K)
    flat_w = top_w.reshape(-1)
    order = jnp.argsort(flat_e)
    sorted_e = flat_e[order]
    counts = jnp.bincount(flat_e, length=N_EXPERTS)
    starts = jnp.cumsum(counts) - counts
    padded = (counts + EXPERT_BLOCK - 1) // EXPERT_BLOCK * EXPERT_BLOCK
    pad_ends = jnp.cumsum(padded)
    pad_starts = pad_ends - padded
    dest = pad_starts[sorted_e] + jnp.arange(M, dtype=jnp.int32) - starts[sorted_e]
    n_blocks = (M + N_EXPERTS * (EXPERT_BLOCK - 1) + EXPERT_BLOCK - 1) // EXPERT_BLOCK
    P = n_blocks * EXPERT_BLOCK
    buf_tok = jnp.full((P,), N, jnp.int32).at[dest].set(flat_tok[order])
    buf_w = jnp.zeros((P,), jnp.float32).at[dest].set(flat_w[order])
    block_start = jnp.arange(n_blocks, dtype=jnp.int32) * EXPERT_BLOCK
    block_e = jnp.minimum(jnp.searchsorted(pad_ends, block_start, side='right'), N_EXPERTS - 1)
    x_pad = jnp.concatenate([xf, jnp.zeros((1, D), xf.dtype)], axis=0)
    xb = x_pad[buf_tok].reshape(n_blocks, EXPERT_BLOCK, D)

    def expert_block(args):
        xblk, e = args
        return swiglu(xblk, w_gate[e], w_up[e], w_down[e])

    yb = lax.map(expert_block, (xb, block_e)).reshape(P, D)
    out = jnp.zeros((N + 1, D), jnp.float32).at[buf_tok].add(yb.astype(jnp.float32) * buf_w[:, None])[:N]
    return out.astype(x.dtype).reshape(bsz, S, D)


def setup_inputs(seed: int = 0) -> dict:
    key = jax.random.key(seed)
    ks = iter(jax.random.split(key, 32))

    def nrm(shape, scale):
        return jax.random.normal(next(ks), shape, jnp.float32) * scale

    D = D_MODEL
    ne, no = N_EVEN, N_ODD
    return {
        'x': nrm((BATCH, SEQ, D), 1.0),
        'mem': nrm((BATCH, MEM_LEN, D), 1.0),
        'rel_table': nrm((REL_BUCKETS, C_HEADS), 0.5),
        'mix_w_in': nrm((ne, D, MIX_IN), D ** -0.5),
        'gmlp_ln_g': 1.0 + nrm((ne, A_WIDTH), 0.02),
        'gmlp_ln_b': nrm((ne, A_WIDTH), 0.02),
        'gmlp_w_s': nrm((ne, A_HEADS, GMLP_CHUNK, GMLP_CHUNK), GMLP_CHUNK ** -0.5),
        'gmlp_b_s': 1.0 + nrm((ne, A_HEADS, GMLP_CHUNK), 0.02),
        'pool_w': nrm((ne, N_POOL, B_GROUP, B_GROUP), B_GROUP ** -0.5),
        'pool_scale': 1.0 + nrm((ne, B_WIDTH), 0.02),
        'mix_w_out': nrm((ne, D, D), D ** -0.5 * DN_BETA),
        'diff_w_qkv': nrm((no, D, 3 * D), D ** -0.5),
        'diff_lam_q1': nrm((no, C_HEAD_DIM), 0.1),
        'diff_lam_k1': nrm((no, C_HEAD_DIM), 0.1),
        'diff_lam_q2': nrm((no, C_HEAD_DIM), 0.1),
        'diff_lam_k2': nrm((no, C_HEAD_DIM), 0.1),
        'diff_subln_g': 1.0 + nrm((no, C_V_DIM), 0.02),
        'diff_w_o': nrm((no, D, D), D ** -0.5 * DN_BETA),
        'xa_w_q': nrm((DEPTH, D, D), D ** -0.5),
        'xa_w_kv': nrm((DEPTH, D, 2 * D), D ** -0.5),
        'xa_w_o': nrm((DEPTH, D, D), D ** -0.5 * DN_BETA),
        'ffn_w_gate': nrm((ne, D, D_FF), D ** -0.5),
        'ffn_w_up': nrm((ne, D, D_FF), D ** -0.5),
        'ffn_w_down': nrm((ne, D_FF, D), D_FF ** -0.5 * DN_BETA),
        'moe_w_router': nrm((no, D, N_EXPERTS), D ** -0.5),
        'moe_w_gate': nrm((no, N_EXPERTS, D, D_FF_EXPERT), D ** -0.5),
        'moe_w_up': nrm((no, N_EXPERTS, D, D_FF_EXPERT), D ** -0.5),
        'moe_w_down': nrm((no, N_EXPERTS, D_FF_EXPERT, D), D_FF_EXPERT ** -0.5 * DN_BETA),
        'ln_g': 1.0 + nrm((DEPTH, 3, D), 0.02),
        'ln_b': nrm((DEPTH, 3, D), 0.02),
    }


def reference(x, mem, rel_table, mix_w_in, gmlp_ln_g, gmlp_ln_b, gmlp_w_s, gmlp_b_s, pool_w, pool_scale,
              mix_w_out, diff_w_qkv, diff_lam_q1, diff_lam_k1, diff_lam_q2, diff_lam_k2, diff_subln_g, diff_w_o,
              xa_w_q, xa_w_kv, xa_w_o, ffn_w_gate, ffn_w_up, ffn_w_down, moe_w_router, moe_w_gate, moe_w_up,
              moe_w_down, ln_g, ln_b):
    h = x
    for layer in range(DEPTH):
        i = layer // 2
        if layer % 2 == 0:
            mix = gmlp_pool_mixer(h, mix_w_in[i], gmlp_ln_g[i], gmlp_ln_b[i], gmlp_w_s[i], gmlp_b_s[i],
                                  pool_w[i], pool_scale[i], mix_w_out[i])
        else:
            lam_init = 0.8 - 0.6 * math.exp(-0.3 * layer)
            mix = diff_attention(h, diff_w_qkv[i], diff_lam_q1[i], diff_lam_k1[i], diff_lam_q2[i], diff_lam_k2[i],
                                 diff_subln_g[i], diff_w_o[i], rel_table, lam_init)
        h = layer_norm(DN_ALPHA * h + mix, ln_g[layer, 0], ln_b[layer, 0])
        xa = memory_cross_attention(h, mem, xa_w_q[layer], xa_w_kv[layer], xa_w_o[layer])
        h = layer_norm(DN_ALPHA * h + xa, ln_g[layer, 1], ln_b[layer, 1])
        if layer % 2 == 0:
            ff = swiglu(h, ffn_w_gate[i], ffn_w_up[i], ffn_w_down[i])
        else:
            ff = moe_swiglu(h, moe_w_router[i], moe_w_gate[i], moe_w_up[i], moe_w_down[i])
        h = layer_norm(DN_ALPHA * h + ff, ln_g[layer, 2], ln_b[layer, 2])
    return h
```

```python
import functools
import math

import numpy as np
import jax
import jax.numpy as jnp
from jax import lax
from jax.experimental import pallas as pl
from jax.experimental.pallas import tpu as pltpu

F32 = jnp.float32
BF16 = jnp.bfloat16

D = 2048
DEPTH = 4
CHUNK = 64
GMLP_CHUNK = 128
A_WIDTH = D // 2
A_HEADS = 8
A_HEAD_DIM = A_WIDTH // A_HEADS
B_WIDTH = D // 2
POOL_WINDOWS = (2, 4, 8, 16)
B_GROUP = B_WIDTH // len(POOL_WINDOWS)
MIX_IN = 2 * A_WIDTH + B_WIDTH
C_HEADS = 16
C_HEAD_DIM = D // (2 * C_HEADS)
C_V_DIM = 2 * C_HEAD_DIM
REL_BUCKETS = 32
REL_MAX_DIST = 128
X_HEADS = 4
X_HEAD_DIM = D // X_HEADS
N_EXPERTS = 8
TOP_K = 2
EXPERT_BLOCK = 256
DN_ALPHA = (2 * DEPTH) ** 0.25
LN_EPS = 1e-5
NEG = -1e30

VMEM_CAP_BYTES = 56 * 1024 * 1024
CAST_ROWS = 256
ATT_TQ = 256
POOL_HALO = 16


def _params(n_axes, vmem_bytes):
    return pltpu.CompilerParams(
        dimension_semantics=("arbitrary",) * n_axes,
        vmem_limit_bytes=int(min(max(vmem_bytes, 16 * 1024 * 1024), VMEM_CAP_BYTES)),
    )


def _cast_rows(w_ref, wb_ref):
    k = w_ref.shape[0]
    rows = CAST_ROWS if k % CAST_ROWS == 0 else k

    def body(c, carry):
        r = pl.multiple_of(c * rows, rows)
        wb_ref[pl.ds(r, rows), :] = w_ref[pl.ds(r, rows), :].astype(BF16)
        return carry

    lax.fori_loop(0, k // rows, body, 0)


def _layer_norm(x, g, b):
    mu = jnp.mean(x, axis=-1, keepdims=True)
    xc = x - mu
    var = jnp.mean(xc * xc, axis=-1, keepdims=True)
    return xc * lax.rsqrt(var + LN_EPS) * g + b


def _mm_kernel(a_ref, w_ref, o_ref, wb_ref):
    @pl.when(pl.program_id(1) == 0)
    def _():
        _cast_rows(w_ref, wb_ref)

    a = a_ref[...].astype(BF16)
    o_ref[...] = jnp.dot(a, wb_ref[...], preferred_element_type=F32).astype(o_ref.dtype)


def matmul(a, w, layer, *, tm, tn, out_dtype):
    m, k = a.shape
    n = w.shape[-1]
    assert m % tm == 0 and n % tn == 0 and w.shape[-2] == k
    vmem = 2 * tm * k * a.dtype.itemsize + 2 * k * tn * 4 + k * tn * 2 + 2 * tm * tn * 4 + tm * tn * 4
    return pl.pallas_call(
        _mm_kernel,
        out_shape=jax.ShapeDtypeStruct((m, n), out_dtype),
        grid=(n // tn, m // tm),
        in_specs=[
            pl.BlockSpec((tm, k), lambda j, i: (i, 0)),
            pl.BlockSpec((None, k, tn), lambda j, i: (layer, 0, j)),
        ],
        out_specs=pl.BlockSpec((tm, tn), lambda j, i: (i, j)),
        scratch_shapes=[pltpu.VMEM((k, tn), BF16)],
        compiler_params=_params(2, vmem + (4 << 20)),
        name="matmul",
    )(a, w)


def _swiglu_kernel(a_ref, wg_ref, wu_ref, o_ref, wgb_ref, wub_ref):
    @pl.when(pl.program_id(1) == 0)
    def _():
        _cast_rows(wg_ref, wgb_ref)
        _cast_rows(wu_ref, wub_ref)

    a = a_ref[...]
    g = jnp.dot(a, wgb_ref[...], preferred_element_type=F32)
    u = jnp.dot(a, wub_ref[...], preferred_element_type=F32)
    o_ref[...] = (jax.nn.silu(g) * u).astype(o_ref.dtype)


def swiglu_up(a, wg, wu, layer, *, tm, tn):
    m, k = a.shape
    n = wg.shape[-1]
    assert m % tm == 0 and n % tn == 0
    vmem = 2 * tm * k * 2 + 4 * k * tn * 4 + 2 * k * tn * 2 + 2 * tm * tn * 2 + 3 * tm * tn * 4
    wspec = pl.BlockSpec((None, k, tn), lambda j, i: (layer, 0, j))
    return pl.pallas_call(
        _swiglu_kernel,
        out_shape=jax.ShapeDtypeStruct((m, n), BF16),
        grid=(n // tn, m // tm),
        in_specs=[pl.BlockSpec((tm, k), lambda j, i: (i, 0)), wspec, wspec],
        out_specs=pl.BlockSpec((tm, tn), lambda j, i: (i, j)),
        scratch_shapes=[pltpu.VMEM((k, tn), BF16), pltpu.VMEM((k, tn), BF16)],
        compiler_params=_params(2, vmem + (4 << 20)),
        name="swiglu_up",
    )(a, wg, wu)


def _ln_kernel(h_ref, y_ref, g_ref, b_ref, of_ref, ob_ref):
    x = DN_ALPHA * h_ref[...] + y_ref[...]
    y = _layer_norm(x, g_ref[...], b_ref[...])
    of_ref[...] = y
    ob_ref[...] = y.astype(BF16)


def residual_ln(h, y, ln_g, ln_b, layer, which, *, tm=256):
    m, d = h.shape
    row = pl.BlockSpec((tm, d), lambda i: (i, 0))
    par = pl.BlockSpec((None, None, 1, d), lambda i: (layer, which, 0, 0))
    return pl.pallas_call(
        _ln_kernel,
        out_shape=(jax.ShapeDtypeStruct((m, d), F32), jax.ShapeDtypeStruct((m, d), BF16)),
        grid=(m // tm,),
        in_specs=[row, row, par, par],
        out_specs=(row, row),
        compiler_params=_params(1, 16 * tm * d * 4),
        name="residual_ln",
    )(h, y, ln_g, ln_b)


MIX_ROWS = 2 * GMLP_CHUNK


def _mixer_kernel(hm_ref, prev_ref, lng_ref, lnb_ref, ws_ref, bst_ref, pw_ref, ps_ref, o_ref,
                  wbd_ref, ext_ref, *, blocks_per_seq):
    i = pl.program_id(0)

    @pl.when(i == 0)
    def _():
        p = lax.broadcasted_iota(jnp.int32, (GMLP_CHUNK, GMLP_CHUNK), 0)
        q = lax.broadcasted_iota(jnp.int32, (GMLP_CHUNK, GMLP_CHUNK), 1)
        allowed = (q // CHUNK) <= (p // CHUNK)
        wbd_ref[...] = jnp.zeros(wbd_ref.shape, BF16)
        for h in range(A_HEADS):
            w = jnp.where(allowed, ws_ref[h], 0.0).astype(BF16)
            wbd_ref[h, 0:GMLP_CHUNK, 0:GMLP_CHUNK] = w
            wbd_ref[h, GMLP_CHUNK:MIX_ROWS, GMLP_CHUNK:MIX_ROWS] = w

    z = jax.nn.gelu(hm_ref[:, 0:2 * A_WIDTH])
    u = z[:, 0:A_WIDTH]
    v = _layer_norm(z[:, A_WIDTH:2 * A_WIDTH], lng_ref[...], lnb_ref[...]).astype(BF16)
    for h in range(A_HEADS):
        c0 = h * A_HEAD_DIM
        sv = jnp.dot(wbd_ref[h], v[:, c0:c0 + A_HEAD_DIM], preferred_element_type=F32)
        sv = sv + bst_ref[:, h:h + 1]
        o_ref[:, c0:c0 + A_HEAD_DIM] = (u[:, c0:c0 + A_HEAD_DIM] * sv).astype(BF16)

    blk = i % blocks_per_seq
    prev = jnp.where(blk == 0, 0.0, prev_ref[...])
    ext_ref[0:POOL_HALO, :] = prev
    ext_ref[POOL_HALO:POOL_HALO + MIX_ROWS, :] = hm_ref[:, 2 * A_WIDTH:MIX_IN]
    t1 = blk * MIX_ROWS + lax.broadcasted_iota(jnp.int32, (MIX_ROWS, 1), 0) + 1
    for g, w in enumerate(POOL_WINDOWS):
        c0 = g * B_GROUP
        pin = ext_ref[POOL_HALO:POOL_HALO + MIX_ROWS, c0:c0 + B_GROUP]
        s = pin
        for k in range(1, w):
            s = s + ext_ref[POOL_HALO - k:POOL_HALO - k + MIX_ROWS, c0:c0 + B_GROUP]
        cnt = jnp.minimum(t1, w).astype(F32)
        pooled = s / cnt - pin
        b = jnp.dot(pooled.astype(BF16), pw_ref[g].astype(BF16), preferred_element_type=F32)
        b = b * ps_ref[:, c0:c0 + B_GROUP]
        o_ref[:, A_WIDTH + c0:A_WIDTH + c0 + B_GROUP] = b.astype(BF16)


def gmlp_pool_mixer(hm, seq_len, ln_g, ln_b, w_s, b_s, pool_w, pool_scale, layer):
    n = hm.shape[0]
    assert seq_len % MIX_ROWS == 0 and max(POOL_WINDOWS) <= POOL_HALO
    blocks_per_seq = seq_len // MIX_ROWS
    halo_per_block = MIX_ROWS // POOL_HALO
    bst = jnp.tile(b_s[layer].T, (MIX_ROWS // GMLP_CHUNK, 1))
    kernel = functools.partial(_mixer_kernel, blocks_per_seq=blocks_per_seq)
    return pl.pallas_call(
        kernel,
        out_shape=jax.ShapeDtypeStruct((n, D), BF16),
        grid=(n // MIX_ROWS,),
        in_specs=[
            pl.BlockSpec((MIX_ROWS, MIX_IN), lambda i: (i, 0)),
            pl.BlockSpec((POOL_HALO, B_WIDTH),
                         lambda i: (jnp.maximum(i * halo_per_block - 1, 0), 2 * A_WIDTH // B_WIDTH)),
            pl.BlockSpec((None, 1, A_WIDTH), lambda i: (layer, 0, 0)),
            pl.BlockSpec((None, 1, A_WIDTH), lambda i: (layer, 0, 0)),
            pl.BlockSpec((None, A_HEADS, GMLP_CHUNK, GMLP_CHUNK), lambda i: (layer, 0, 0, 0)),
            pl.BlockSpec((MIX_ROWS, A_HEADS), lambda i: (0, 0)),
            pl.BlockSpec((None, len(POOL_WINDOWS), B_GROUP, B_GROUP), lambda i: (layer, 0, 0, 0)),
            pl.BlockSpec((None, 1, B_WIDTH), lambda i: (layer, 0, 0)),
        ],
        out_specs=pl.BlockSpec((MIX_ROWS, D), lambda i: (i, 0)),
        scratch_shapes=[
            pltpu.VMEM((A_HEADS, MIX_ROWS, MIX_ROWS), BF16),
            pltpu.VMEM((POOL_HALO + MIX_ROWS, B_WIDTH), F32),
        ],
        compiler_params=_params(1, 40 << 20),
        name="gmlp_pool_mixer",
    )(hm, hm, ln_g.reshape(-1, 1, A_WIDTH), ln_b.reshape(-1, 1, A_WIDTH), w_s, bst, pool_w,
      pool_scale.reshape(-1, 1, B_WIDTH))


def _t5_bucket_np(rel):
    half = REL_BUCKETS // 2
    max_exact = half // 2
    rel = np.asarray(rel, dtype=np.int64)
    n = np.abs(rel)
    steps = half - max_exact
    ratio = REL_MAX_DIST // max_exact
    large = np.zeros_like(n)
    nn = n.astype(object) ** steps
    for j in range(1, steps + 1):
        large = large + (nn >= (max_exact ** steps) * (ratio ** j)).astype(np.int64)
    large = np.minimum(max_exact + large, half - 1)
    return np.where(rel > 0, half, 0) + np.where(n < max_exact, n, large)


def _bias_bucket_tiles():
    qi = np.arange(ATT_TQ)[:, None]
    kj = np.arange(ATT_TQ)[None, :]
    diag = _t5_bucket_np(kj - qi)
    diag = np.where((kj // CHUNK) <= (qi // CHUNK), diag, -1)
    prev = _t5_bucket_np(kj - ATT_TQ - qi)
    tiles = np.stack([np.concatenate([diag, diag], 0), np.concatenate([prev, prev], 0)], 0)
    return tiles.astype(np.int32)


def _bias_kernel(rel_ref, bk_ref, o_ref):
    h = pl.program_id(0)
    bk = bk_ref[...]
    acc = jnp.zeros(bk.shape, F32)
    for b in range(REL_BUCKETS):
        acc = jnp.where(bk == b, rel_ref[b, h], acc)
    o_ref[...] = jnp.where(bk < 0, NEG, acc)


def attention_bias_tiles(rel_table):
    bk = jnp.asarray(_bias_bucket_tiles())
    return pl.pallas_call(
        _bias_kernel,
        out_shape=jax.ShapeDtypeStruct((C_HEADS, 2, 2 * ATT_TQ, ATT_TQ), F32),
        grid=(C_HEADS,),
        in_specs=[
            pl.BlockSpec(memory_space=pltpu.SMEM),
            pl.BlockSpec((2, 2 * ATT_TQ, ATT_TQ), lambda h: (0, 0, 0)),
        ],
        out_specs=pl.BlockSpec((None, 2, 2 * ATT_TQ, ATT_TQ), lambda h: (h, 0, 0, 0)),
        compiler_params=_params(1, 16 << 20),
        name="attention_bias_tiles",
    )(rel_table, bk)


def _diff_attn_kernel(rel_ref, q_ref, k_ref, v_ref, bias_ref, lam_ref, g_ref, o_ref,
                      qq_ref, m_ref, l_ref, acc_ref, *, lam_init):
    h = pl.program_id(1)
    qi = pl.program_id(2)
    tq = ATT_TQ
    q = q_ref[...]
    lane = lax.broadcasted_iota(jnp.int32, q.shape, 1)
    zero = jnp.zeros_like(q)
    qq_ref[0:tq, :] = jnp.where(lane < C_HEAD_DIM, q, zero)
    qq_ref[tq:2 * tq, :] = jnp.where(lane >= C_HEAD_DIM, q, zero)
    m_ref[...] = jnp.full(m_ref.shape, -jnp.inf, F32)
    l_ref[...] = jnp.zeros(l_ref.shape, F32)
    acc_ref[...] = jnp.zeros(acc_ref.shape, F32)
    scale = C_HEAD_DIM ** -0.5

    def tile(j, bias):
        r = pl.multiple_of(j * tq, tq)
        k = k_ref[pl.ds(r, tq), :]
        v = v_ref[pl.ds(r, tq), :]
        s = lax.dot_general(qq_ref[...], k, (((1,), (1,)), ((), ())), preferred_element_type=F32)
        s = s * scale + bias
        m_prev = m_ref[...]
        m_new = jnp.maximum(m_prev, jnp.max(s, axis=1, keepdims=True))
        alpha = jnp.exp(m_prev - m_new)
        p = jnp.exp(s - m_new)
        l_ref[...] = alpha * l_ref[...] + jnp.sum(p, axis=1, keepdims=True)
        acc_ref[...] = alpha * acc_ref[...] + jnp.dot(p.astype(BF16), v, preferred_element_type=F32)
        m_ref[...] = m_new

    far_bias = rel_ref[REL_BUCKETS // 2 - 1, h]

    def far_body(j, carry):
        tile(j, far_bias)
        return carry

    lax.fori_loop(0, qi - 1, far_body, 0)

    @pl.when(qi >= 1)
    def _():
        tile(qi - 1, bias_ref[1])

    tile(qi, bias_ref[0])

    l = l_ref[...]
    acc = acc_ref[...]
    o1 = acc[0:tq] / l[0:tq]
    o2 = acc[tq:2 * tq] / l[tq:2 * tq]
    lp = lam_ref[...]
    lam = (jnp.exp(jnp.sum(lp[0:1] * lp[1:2], axis=1, keepdims=True))
           - jnp.exp(jnp.sum(lp[2:3] * lp[3:4], axis=1, keepdims=True)) + lam_init)
    o = o1 - lam * o2
    y = o * lax.rsqrt(jnp.mean(o * o, axis=-1, keepdims=True) + LN_EPS) * g_ref[...]
    o_ref[...] = (y * (1.0 - lam_init)).astype(BF16)


def diff_attention(qkv, bias_tiles, rel_table, lam_params, subln_g, layer, lam_init, batch, seq_len):
    n = qkv.shape[0]
    tq = ATT_TQ
    nq = seq_len // tq
    kernel = functools.partial(_diff_attn_kernel, lam_init=lam_init)
    return pl.pallas_call(
        kernel,
        out_shape=jax.ShapeDtypeStruct((n, D), BF16),
        grid=(batch, C_HEADS, nq),
        in_specs=[
            pl.BlockSpec(memory_space=pltpu.SMEM),
            pl.BlockSpec((tq, C_V_DIM), lambda b, h, i: (b * nq + i, h)),
            pl.BlockSpec((seq_len, C_V_DIM), lambda b, h, i: (b, C_HEADS + h)),
            pl.BlockSpec((seq_len, C_V_DIM), lambda b, h, i: (b, 2 * C_HEADS + h)),
            pl.BlockSpec((None, 2, 2 * tq, tq), lambda b, h, i: (h, 0, 0, 0)),
            pl.BlockSpec((None, 4, C_HEAD_DIM), lambda b, h, i: (layer, 0, 0)),
            pl.BlockSpec((None, 1, C_V_DIM), lambda b, h, i: (layer, 0, 0)),
        ],
        out_specs=pl.BlockSpec((tq, C_V_DIM), lambda b, h, i: (b * nq + i, h)),
        scratch_shapes=[
            pltpu.VMEM((2 * tq, C_V_DIM), BF16),
            pltpu.VMEM((2 * tq, 1), F32),
            pltpu.VMEM((2 * tq, 1), F32),
            pltpu.VMEM((2 * tq, C_V_DIM), F32),
        ],
        compiler_params=_params(3, 24 << 20),
        name="diff_attention",
    )(rel_table, qkv, qkv, qkv, bias_tiles, lam_params, subln_g.reshape(-1, 1, C_V_DIM))


def _xattn_kernel(q_ref, kv_ref, o_ref):
    scale = X_HEAD_DIM ** -0.5
    for h in range(X_HEADS):
        c0 = h * X_HEAD_DIM
        q = q_ref[:, c0:c0 + X_HEAD_DIM]
        k = kv_ref[:, c0:c0 + X_HEAD_DIM]
        v = kv_ref[:, D + c0:D + c0 + X_HEAD_DIM]
        s = lax.dot_general(q, k, (((1,), (1,)), ((), ())), preferred_element_type=F32) * scale
        m = jnp.max(s, axis=1, keepdims=True)
        p = jnp.exp(s - m)
        l = jnp.sum(p, axis=1, keepdims=True)
        o = jnp.dot(p.astype(BF16), v, preferred_element_type=F32) / l
        o_ref[:, c0:c0 + X_HEAD_DIM] = o.astype(BF16)


def cross_attention(q, kv, batch, seq_len, mem_len, *, tm=512):
    n = q.shape[0]
    blocks_per_seq = seq_len // tm
    return pl.pallas_call(
        _xattn_kernel,
        out_shape=jax.ShapeDtypeStruct((n, D), BF16),
        grid=(n // tm,),
        in_specs=[
            pl.BlockSpec((tm, D), lambda i: (i, 0)),
            pl.BlockSpec((mem_len, 2 * D), lambda i: (i // blocks_per_seq, 0)),
        ],
        out_specs=pl.BlockSpec((tm, D), lambda i: (i, 0)),
        compiler_params=_params(1, 32 << 20),
        name="cross_attention",
    )(q, kv)


ROUTE_T = 256


def _router_kernel(h_ref, wrt_ref, e_ref, w_ref, rank_ref, cnt_ref, base_ref):
    i = pl.program_id(0)

    @pl.when(i == 0)
    def _():
        base_ref[...] = jnp.zeros(base_ref.shape, F32)

    t = ROUTE_T
    logits = lax.dot_general(wrt_ref[...], h_ref[...], (((1,), (1,)), ((), ())),
                             precision=lax.Precision.HIGHEST, preferred_element_type=F32)
    eidx = lax.broadcasted_iota(jnp.int32, (N_EXPERTS, t), 0)
    m1 = jnp.max(logits, axis=0, keepdims=True)
    e1 = jnp.min(jnp.where(logits == m1, eidx, N_EXPERTS), axis=0, keepdims=True)
    oh1 = eidx == e1
    rest = jnp.where(oh1, -jnp.inf, logits)
    m2 = jnp.max(rest, axis=0, keepdims=True)
    e2 = jnp.min(jnp.where(rest == m2, eidx, N_EXPERTS), axis=0, keepdims=True)
    oh2 = eidx == e2
    ex = jnp.exp(m2 - m1)
    den = 1.0 + ex
    e_ref[0:1, :] = e1
    e_ref[1:2, :] = e2
    w_ref[0:1, :] = 1.0 / den
    w_ref[1:2, :] = ex / den

    oh = (oh1.astype(F32) + oh2.astype(F32)).astype(BF16)
    r = lax.broadcasted_iota(jnp.int32, (t, t), 0)
    c = lax.broadcasted_iota(jnp.int32, (t, t), 1)
    before = (r < c).astype(BF16)
    ones = jnp.ones((t, t), BF16)
    pos = base_ref[...] + jnp.dot(oh, before, preferred_element_type=F32)
    rank_ref[0:1, :] = jnp.sum(jnp.where(oh1, pos, 0.0), axis=0, keepdims=True).astype(jnp.int32)
    rank_ref[1:2, :] = jnp.sum(jnp.where(oh2, pos, 0.0), axis=0, keepdims=True).astype(jnp.int32)
    total = base_ref[...] + jnp.dot(oh, ones, preferred_element_type=F32)
    base_ref[...] = total
    cnt_ref[...] = total[:, 0:128].astype(jnp.int32)


def moe_router(h, w_router, layer):
    n = h.shape[0]
    t = ROUTE_T
    wrt = jnp.swapaxes(w_router, 1, 2)
    pair = pl.BlockSpec((TOP_K, t), lambda i: (0, i))
    return pl.pallas_call(
        _router_kernel,
        out_shape=(
            jax.ShapeDtypeStruct((TOP_K, n), jnp.int32),
            jax.ShapeDtypeStruct((TOP_K, n), F32),
            jax.ShapeDtypeStruct((TOP_K, n), jnp.int32),
            jax.ShapeDtypeStruct((N_EXPERTS, 128), jnp.int32),
        ),
        grid=(n // t,),
        in_specs=[
            pl.BlockSpec((t, D), lambda i: (i, 0)),
            pl.BlockSpec((None, N_EXPERTS, D), lambda i: (layer, 0, 0)),
        ],
        out_specs=(pair, pair, pair, pl.BlockSpec((N_EXPERTS, 128), lambda i: (0, 0))),
        scratch_shapes=[pltpu.VMEM((N_EXPERTS, t), F32)],
        compiler_params=_params(1, 16 << 20),
        name="moe_router",
    )(h, wrt)


def _row_copy(src_hbm, row, dst_ref, r, sem):
    return pltpu.make_async_copy(src_hbm.at[pl.ds(row, 1)], dst_ref.at[pl.ds(r, 1)], sem)


def _gather_kernel(tok_ref, nused_ref, h_hbm, o_ref, buf_ref, sem):
    i = pl.program_id(0)
    rows = EXPERT_BLOCK

    @pl.when(i < nused_ref[0])
    def _():
        def start(r, carry):
            _row_copy(h_hbm, tok_ref[i * rows + r], buf_ref, r, sem).start()
            return carry

        lax.fori_loop(0, rows, start, 0)

        def wait(r, carry):
            _row_copy(h_hbm, 0, buf_ref, r, sem).wait()
            return carry

        lax.fori_loop(0, rows, wait, 0)
        o_ref[...] = buf_ref[...].astype(BF16)

    @pl.when(i >= nused_ref[0])
    def _():
        o_ref[...] = jnp.zeros(o_ref.shape, BF16)


def moe_gather(h, buf_tok, nused, n_blocks):
    rows = EXPERT_BLOCK
    return pl.pallas_call(
        _gather_kernel,
        out_shape=jax.ShapeDtypeStruct((n_blocks * rows, D), BF16),
        grid_spec=pltpu.PrefetchScalarGridSpec(
            num_scalar_prefetch=2,
            grid=(n_blocks,),
            in_specs=[pl.BlockSpec(memory_space=pl.ANY)],
            out_specs=pl.BlockSpec((rows, D), lambda i, tok, nu: (i, 0)),
            scratch_shapes=[pltpu.VMEM((rows, D), F32), pltpu.SemaphoreType.DMA(())],
        ),
        compiler_params=_params(1, 16 << 20),
        name="moe_gather",
    )(buf_tok, nused, h)


def _expert_changed(be_ref, i):
    return jnp.logical_or(i == 0, be_ref[i] != be_ref[jnp.maximum(i - 1, 0)])


def _moe_up_kernel(be_ref, nused_ref, x_ref, wg_ref, wu_ref, o_ref, wgb_ref, wub_ref):
    i = pl.program_id(1)

    @pl.when(_expert_changed(be_ref, i))
    def _():
        _cast_rows(wg_ref, wgb_ref)
        _cast_rows(wu_ref, wub_ref)

    @pl.when(i < nused_ref[0])
    def _():
        x = x_ref[...]
        g = jnp.dot(x, wgb_ref[...], preferred_element_type=F32)
        u = jnp.dot(x, wub_ref[...], preferred_element_type=F32)
        o_ref[...] = (jax.nn.silu(g) * u).astype(o_ref.dtype)

    @pl.when(i >= nused_ref[0])
    def _():
        o_ref[...] = jnp.zeros(o_ref.shape, o_ref.dtype)


def moe_up(xb, w_gate, w_up, layer, block_e, nused, *, tn=256):
    p, k = xb.shape
    rows = EXPERT_BLOCK
    f = w_gate.shape[-1]
    assert f % tn == 0
    wspec = pl.BlockSpec((None, None, k, tn), lambda j, i, be, nu: (layer, be[i], 0, j))
    return pl.pallas_call(
        _moe_up_kernel,
        out_shape=jax.ShapeDtypeStruct((p, f), BF16),
        grid_spec=pltpu.PrefetchScalarGridSpec(
            num_scalar_prefetch=2,
            grid=(f // tn, p // rows),
            in_specs=[pl.BlockSpec((rows, k), lambda j, i, be, nu: (i, 0)), wspec, wspec],
            out_specs=pl.BlockSpec((rows, tn), lambda j, i, be, nu: (i, j)),
            scratch_shapes=[pltpu.VMEM((k, tn), BF16), pltpu.VMEM((k, tn), BF16)],
        ),
        compiler_params=_params(2, 24 << 20),
        name="moe_up",
    )(block_e, nused, xb, w_gate, w_up)


def _moe_down_kernel(be_ref, nused_ref, x_ref, w_ref, o_ref, wb_ref):
    i = pl.program_id(1)

    @pl.when(_expert_changed(be_ref, i))
    def _():
        _cast_rows(w_ref, wb_ref)

    @pl.when(i < nused_ref[0])
    def _():
        o_ref[...] = jnp.dot(x_ref[...], wb_ref[...], preferred_element_type=F32)

    @pl.when(i >= nused_ref[0])
    def _():
        o_ref[...] = jnp.zeros(o_ref.shape, o_ref.dtype)


def moe_down(hb, w_down, layer, block_e, nused, *, tn=512):
    p, k = hb.shape
    rows = EXPERT_BLOCK
    n = w_down.shape[-1]
    return pl.pallas_call(
        _moe_down_kernel,
        out_shape=jax.ShapeDtypeStruct((p, n), F32),
        grid_spec=pltpu.PrefetchScalarGridSpec(
            num_scalar_prefetch=2,
            grid=(n // tn, p // rows),
            in_specs=[
                pl.BlockSpec((rows, k), lambda j, i, be, nu: (i, 0)),
                pl.BlockSpec((None, None, k, tn), lambda j, i, be, nu: (layer, be[i], 0, j)),
            ],
            out_specs=pl.BlockSpec((rows, tn), lambda j, i, be, nu: (i, j)),
            scratch_shapes=[pltpu.VMEM((k, tn), BF16)],
        ),
        compiler_params=_params(2, 32 << 20),
        name="moe_down",
    )(block_e, nused, hb, w_down)


COMBINE_T = 256


def _combine_kernel(dest_ref, y_hbm, wt_ref, h_ref, g_ref, b_ref, of_ref, ob_ref,
                    buf0_ref, buf1_ref, sem, *, n_tokens):
    i = pl.program_id(0)
    t = COMBINE_T

    def start(r, carry):
        tok = i * t + r
        _row_copy(y_hbm, dest_ref[tok], buf0_ref, r, sem).start()
        _row_copy(y_hbm, dest_ref[n_tokens + tok], buf1_ref, r, sem).start()
        return carry

    lax.fori_loop(0, t, start, 0)

    def wait(r, carry):
        _row_copy(y_hbm, 0, buf0_ref, r, sem).wait()
        _row_copy(y_hbm, 0, buf1_ref, r, sem).wait()
        return carry

    lax.fori_loop(0, t, wait, 0)
    ff = buf0_ref[...] * wt_ref[:, 0:1] + buf1_ref[...] * wt_ref[:, 1:2]
    y = _layer_norm(DN_ALPHA * h_ref[...] + ff, g_ref[...], b_ref[...])
    of_ref[...] = y
    ob_ref[...] = y.astype(BF16)


def moe_combine_ln(yb, dest, wts_t, h, ln_g, ln_b, layer, which):
    n, d = h.shape
    t = COMBINE_T
    row = pl.BlockSpec((t, d), lambda i, ds: (i, 0))
    par = pl.BlockSpec((None, None, 1, d), lambda i, ds: (layer, which, 0, 0))
    kernel = functools.partial(_combine_kernel, n_tokens=n)
    return pl.pallas_call(
        kernel,
        out_shape=(jax.ShapeDtypeStruct((n, d), F32), jax.ShapeDtypeStruct((n, d), BF16)),
        grid_spec=pltpu.PrefetchScalarGridSpec(
            num_scalar_prefetch=1,
            grid=(n // t,),
            in_specs=[
                pl.BlockSpec(memory_space=pl.ANY),
                pl.BlockSpec((t, TOP_K), lambda i, ds: (i, 0)),
                row, par, par,
            ],
            out_specs=(row, row),
            scratch_shapes=[pltpu.VMEM((t, d), F32), pltpu.VMEM((t, d), F32),
                            pltpu.SemaphoreType.DMA(())],
        ),
        compiler_params=_params(1, 24 << 20),
        name="moe_combine_ln",
    )(dest, yb, wts_t, h, ln_g, ln_b)


def moe_layer(h, hb, w_router, w_gate, w_up, w_down, ln_g, ln_b, layer, moe_idx):
    n = h.shape[0]
    m = n * TOP_K
    n_blocks = (m + N_EXPERTS * (EXPERT_BLOCK - 1) + EXPERT_BLOCK - 1) // EXPERT_BLOCK
    e, wts, rank, cnt = moe_router(h, w_router, moe_idx)
    counts = cnt[:, 0]
    padded = (counts + EXPERT_BLOCK - 1) // EXPERT_BLOCK * EXPERT_BLOCK
    pad_ends = jnp.cumsum(padded)
    pad_starts = pad_ends - padded
    dest = pad_starts[e] + rank
    tok = jnp.broadcast_to(jnp.arange(n, dtype=jnp.int32)[None, :], (TOP_K, n))
    buf_tok = jnp.zeros((n_blocks * EXPERT_BLOCK,), jnp.int32).at[dest.reshape(-1)].set(tok.reshape(-1))
    block_start = jnp.arange(n_blocks, dtype=jnp.int32) * EXPERT_BLOCK
    block_e = jnp.minimum(jnp.searchsorted(pad_ends, block_start, side="right"), N_EXPERTS - 1).astype(jnp.int32)
    nused = (pad_ends[-1:] // EXPERT_BLOCK).astype(jnp.int32)

    xb = moe_gather(h, buf_tok, nused, n_blocks)
    hid = moe_up(xb, w_gate, w_up, moe_idx, block_e, nused)
    yb = moe_down(hid, w_down, moe_idx, block_e, nused)
    return moe_combine_ln(yb, dest.reshape(-1), wts.T, h, ln_g, ln_b, layer, 2)


def kernel(x, mem, rel_table, mix_w_in, gmlp_ln_g, gmlp_ln_b, gmlp_w_s, gmlp_b_s, pool_w, pool_scale, mix_w_out, diff_w_qkv, diff_lam_q1, diff_lam_k1, diff_lam_q2, diff_lam_k2, diff_subln_g, diff_w_o, xa_w_q, xa_w_kv, xa_w_o, ffn_w_gate, ffn_w_up, ffn_w_down, moe_w_router, moe_w_gate, moe_w_up, moe_w_down, ln_g, ln_b):
    batch, seq_len, d = x.shape
    mem_len = mem.shape[1]
    n = batch * seq_len
    h = x.reshape(n, d)
    hb = h
    memf = mem.reshape(batch * mem_len, d)
    ln_g4 = ln_g.reshape(DEPTH, 3, 1, d)
    ln_b4 = ln_b.reshape(DEPTH, 3, 1, d)
    lam_params = jnp.stack([diff_lam_q1, diff_lam_k1, diff_lam_q2, diff_lam_k2], axis=1)
    bias_tiles = attention_bias_tiles(rel_table)

    for layer in range(DEPTH):
        i = layer // 2
        if layer % 2 == 0:
            hm = matmul(hb, mix_w_in, i, tm=512, tn=1024, out_dtype=F32)
            mixed = gmlp_pool_mixer(hm, seq_len, gmlp_ln_g, gmlp_ln_b, gmlp_w_s, gmlp_b_s, pool_w,
                                    pool_scale, i)
            mix = matmul(mixed, mix_w_out, i, tm=512, tn=1024, out_dtype=F32)
        else:
            lam_init = 0.8 - 0.6 * math.exp(-0.3 * layer)
            qkv = matmul(hb, diff_w_qkv, i, tm=512, tn=1024, out_dtype=BF16)
            att = diff_attention(qkv, bias_tiles, rel_table, lam_params, diff_subln_g, i, lam_init,
                                 batch, seq_len)
            mix = matmul(att, diff_w_o, i, tm=512, tn=1024, out_dtype=F32)
        h, hb = residual_ln(h, mix, ln_g4, ln_b4, layer, 0)

        q = matmul(hb, xa_w_q, layer, tm=512, tn=1024, out_dtype=BF16)
        kv = matmul(memf, xa_w_kv, layer, tm=512, tn=1024, out_dtype=BF16)
        xo = cross_attention(q, kv, batch, seq_len, mem_len)
        xa = matmul(xo, xa_w_o, layer, tm=512, tn=1024, out_dtype=F32)
        h, hb = residual_ln(h, xa, ln_g4, ln_b4, layer, 1)

        if layer % 2 == 0:
            hid = swiglu_up(hb, ffn_w_gate, ffn_w_up, i, tm=512, tn=512)
            ff = matmul(hid, ffn_w_down, i, tm=512, tn=512, out_dtype=F32)
            h, hb = residual_ln(h, ff, ln_g4, ln_b4, layer, 2)
        else:
            h, hb = moe_layer(h, hb, moe_w_router, moe_w_gate, moe_w_up, moe_w_down, ln_g4, ln_b4,
                              layer, i)
    return h.reshape(batch, seq_len, d)
```

```python
import functools
import math

import numpy as np
import jax
import jax.numpy as jnp
from jax import lax
from jax.experimental import pallas as pl
from jax.experimental.pallas import tpu as pltpu

F32 = jnp.float32
BF16 = jnp.bfloat16

D = 2048
DEPTH = 4
CHUNK = 64
GMLP_CHUNK = 128
A_WIDTH = D // 2
A_HEADS = 8
A_HEAD_DIM = A_WIDTH // A_HEADS
B_WIDTH = D // 2
POOL_WINDOWS = (2, 4, 8, 16)
B_GROUP = B_WIDTH // len(POOL_WINDOWS)
MIX_IN = 2 * A_WIDTH + B_WIDTH
C_HEADS = 16
C_HEAD_DIM = D // (2 * C_HEADS)
C_V_DIM = 2 * C_HEAD_DIM
REL_BUCKETS = 32
REL_MAX_DIST = 128
X_HEADS = 4
X_HEAD_DIM = D // X_HEADS
N_EXPERTS = 8
TOP_K = 2
EXPERT_BLOCK = 256
DN_ALPHA = (2 * DEPTH) ** 0.25
LN_EPS = 1e-5
NEG = -1e30

VMEM_CAP_BYTES = 56 * 1024 * 1024
CAST_ROWS = 256
ATT_TQ = 256
POOL_HALO = 16


def _params(n_axes, vmem_bytes):
    return pltpu.CompilerParams(
        dimension_semantics=("arbitrary",) * n_axes,
        vmem_limit_bytes=int(min(max(vmem_bytes, 16 * 1024 * 1024), VMEM_CAP_BYTES)),
    )


def _cast_rows(w_ref, wb_ref):
    k = w_ref.shape[0]
    rows = CAST_ROWS if k % CAST_ROWS == 0 else k

    def body(c, carry):
        r = pl.multiple_of(c * rows, rows)
        wb_ref[pl.ds(r, rows), :] = w_ref[pl.ds(r, rows), :].astype(BF16)
        return carry

    lax.fori_loop(0, k // rows, body, 0)


def _layer_norm(x, g, b):
    mu = jnp.mean(x, axis=-1, keepdims=True)
    xc = x - mu
    var = jnp.mean(xc * xc, axis=-1, keepdims=True)
    return xc * lax.rsqrt(var + LN_EPS) * g + b


def _mm_kernel(a_ref, w_ref, o_ref, wb_ref):
    @pl.when(pl.program_id(1) == 0)
    def _():
        _cast_rows(w_ref, wb_ref)

    a = a_ref[...].astype(BF16)
    o_ref[...] = jnp.dot(a, wb_ref[...], preferred_element_type=F32).astype(o_ref.dtype)


def matmul(a, w, layer, *, tm, tn, out_dtype):
    m, k = a.shape
    n = w.shape[-1]
    assert m % tm == 0 and n % tn == 0 and w.shape[-2] == k
    vmem = 2 * tm * k * a.dtype.itemsize + 2 * k * tn * 4 + k * tn * 2 + 2 * tm * tn * 4 + tm * tn * 4
    return pl.pallas_call(
        _mm_kernel,
        out_shape=jax.ShapeDtypeStruct((m, n), out_dtype),
        grid=(n // tn, m // tm),
        in_specs=[
            pl.BlockSpec((tm, k), lambda j, i: (i, 0)),
            pl.BlockSpec((None, k, tn), lambda j, i: (layer, 0, j)),
        ],
        out_specs=pl.BlockSpec((tm, tn), lambda j, i: (i, j)),
        scratch_shapes=[pltpu.VMEM((k, tn), BF16)],
        compiler_params=_params(2, vmem + (4 << 20)),
        name="matmul",
    )(a, w)


def _swiglu_kernel(a_ref, wg_ref, wu_ref, o_ref, wgb_ref, wub_ref):
    @pl.when(pl.program_id(1) == 0)
    def _():
        _cast_rows(wg_ref, wgb_ref)
        _cast_rows(wu_ref, wub_ref)

    a = a_ref[...]
    g = jnp.dot(a, wgb_ref[...], preferred_element_type=F32)
    u = jnp.dot(a, wub_ref[...], preferred_element_type=F32)
    o_ref[...] = (jax.nn.silu(g) * u).astype(o_ref.dtype)


def swiglu_up(a, wg, wu, layer, *, tm, tn):
    m, k = a.shape
    n = wg.shape[-1]
    assert m % tm == 0 and n % tn == 0
    vmem = 2 * tm * k * 2 + 4 * k * tn * 4 + 2 * k * tn * 2 + 2 * tm * tn * 2 + 3 * tm * tn * 4
    wspec = pl.BlockSpec((None, k, tn), lambda j, i: (layer, 0, j))
    return pl.pallas_call(
        _swiglu_kernel,
        out_shape=jax.ShapeDtypeStruct((m, n), BF16),
        grid=(n // tn, m // tm),
        in_specs=[pl.BlockSpec((tm, k), lambda j, i: (i, 0)), wspec, wspec],
        out_specs=pl.BlockSpec((tm, tn), lambda j, i: (i, j)),
        scratch_shapes=[pltpu.VMEM((k, tn), BF16), pltpu.VMEM((k, tn), BF16)],
        compiler_params=_params(2, vmem + (4 << 20)),
        name="swiglu_up",
    )(a, wg, wu)


def _ln_kernel(h_ref, y_ref, g_ref, b_ref, of_ref, ob_ref):
    x = DN_ALPHA * h_ref[...] + y_ref[...]
    y = _layer_norm(x, g_ref[...], b_ref[...])
    of_ref[...] = y
    ob_ref[...] = y.astype(BF16)


def residual_ln(h, y, ln_g, ln_b, layer, which, *, tm=256):
    m, d = h.shape
    row = pl.BlockSpec((tm, d), lambda i: (i, 0))
    par = pl.BlockSpec((None, None, 1, d), lambda i: (layer, which, 0, 0))
    return pl.pallas_call(
        _ln_kernel,
        out_shape=(jax.ShapeDtypeStruct((m, d), F32), jax.ShapeDtypeStruct((m, d), BF16)),
        grid=(m // tm,),
        in_specs=[row, row, par, par],
        out_specs=(row, row),
        compiler_params=_params(1, 16 * tm * d * 4),
        name="residual_ln",
    )(h, y, ln_g, ln_b)


MIX_ROWS = 2 * GMLP_CHUNK


def _mixer_kernel(hm_ref, prev_ref, lng_ref, lnb_ref, ws_ref, bst_ref, pw_ref, ps_ref, o_ref,
                  wbd_ref, ext_ref, *, blocks_per_seq):
    i = pl.program_id(0)

    @pl.when(i == 0)
    def _():
        p = lax.broadcasted_iota(jnp.int32, (GMLP_CHUNK, GMLP_CHUNK), 0)
        q = lax.broadcasted_iota(jnp.int32, (GMLP_CHUNK, GMLP_CHUNK), 1)
        allowed = (q // CHUNK) <= (p // CHUNK)
        wbd_ref[...] = jnp.zeros(wbd_ref.shape, BF16)
        for h in range(A_HEADS):
            w = jnp.where(allowed, ws_ref[h], 0.0).astype(BF16)
            wbd_ref[h, 0:GMLP_CHUNK, 0:GMLP_CHUNK] = w
            wbd_ref[h, GMLP_CHUNK:MIX_ROWS, GMLP_CHUNK:MIX_ROWS] = w

    z = jax.nn.gelu(hm_ref[:, 0:2 * A_WIDTH])
    u = z[:, 0:A_WIDTH]
    v = _layer_norm(z[:, A_WIDTH:2 * A_WIDTH], lng_ref[...], lnb_ref[...]).astype(BF16)
    for h in range(A_HEADS):
        c0 = h * A_HEAD_DIM
        sv = jnp.dot(wbd_ref[h], v[:, c0:c0 + A_HEAD_DIM], preferred_element_type=F32)
        sv = sv + bst_ref[:, h:h + 1]
        o_ref[:, c0:c0 + A_HEAD_DIM] = (u[:, c0:c0 + A_HEAD_DIM] * sv).astype(BF16)

    blk = i % blocks_per_seq
    prev = jnp.where(blk == 0, 0.0, prev_ref[...])
    ext_ref[0:POOL_HALO, :] = prev
    ext_ref[POOL_HALO:POOL_HALO + MIX_ROWS, :] = hm_ref[:, 2 * A_WIDTH:MIX_IN]
    t1 = blk * MIX_ROWS + lax.broadcasted_iota(jnp.int32, (MIX_ROWS, 1), 0) + 1
    for g, w in enumerate(POOL_WINDOWS):
        c0 = g * B_GROUP
        pin = ext_ref[POOL_HALO:POOL_HALO + MIX_ROWS, c0:c0 + B_GROUP]
        s = pin
        for k in range(1, w):
            s = s + ext_ref[POOL_HALO - k:POOL_HALO - k + MIX_ROWS, c0:c0 + B_GROUP]
        cnt = jnp.minimum(t1, w).astype(F32)
        pooled = s / cnt - pin
        b = jnp.dot(pooled.astype(BF16), pw_ref[g].astype(BF16), preferred_element_type=F32)
        b = b * ps_ref[:, c0:c0 + B_GROUP]
        o_ref[:, A_WIDTH + c0:A_WIDTH + c0 + B_GROUP] = b.astype(BF16)


def gmlp_pool_mixer(hm, seq_len, ln_g, ln_b, w_s, b_s, pool_w, pool_scale, layer):
    n = hm.shape[0]
    assert seq_len % MIX_ROWS == 0 and max(POOL_WINDOWS) <= POOL_HALO
    blocks_per_seq = seq_len // MIX_ROWS
    halo_per_block = MIX_ROWS // POOL_HALO
    bst = jnp.tile(b_s[layer].T, (MIX_ROWS // GMLP_CHUNK, 1))
    kernel = functools.partial(_mixer_kernel, blocks_per_seq=blocks_per_seq)
    return pl.pallas_call(
        kernel,
        out_shape=jax.ShapeDtypeStruct((n, D), BF16),
        grid=(n // MIX_ROWS,),
        in_specs=[
            pl.BlockSpec((MIX_ROWS, MIX_IN), lambda i: (i, 0)),
            pl.BlockSpec((POOL_HALO, B_WIDTH),
                         lambda i: (jnp.maximum(i * halo_per_block - 1, 0), 2 * A_WIDTH // B_WIDTH)),
            pl.BlockSpec((None, 1, A_WIDTH), lambda i: (layer, 0, 0)),
            pl.BlockSpec((None, 1, A_WIDTH), lambda i: (layer, 0, 0)),
            pl.BlockSpec((None, A_HEADS, GMLP_CHUNK, GMLP_CHUNK), lambda i: (layer, 0, 0, 0)),
            pl.BlockSpec((MIX_ROWS, A_HEADS), lambda i: (0, 0)),
            pl.BlockSpec((None, len(POOL_WINDOWS), B_GROUP, B_GROUP), lambda i: (layer, 0, 0, 0)),
            pl.BlockSpec((None, 1, B_WIDTH), lambda i: (layer, 0, 0)),
        ],
        out_specs=pl.BlockSpec((MIX_ROWS, D), lambda i: (i, 0)),
        scratch_shapes=[
            pltpu.VMEM((A_HEADS, MIX_ROWS, MIX_ROWS), BF16),
            pltpu.VMEM((POOL_HALO + MIX_ROWS, B_WIDTH), F32),
        ],
        compiler_params=_params(1, 40 << 20),
        name="gmlp_pool_mixer",
    )(hm, hm, ln_g.reshape(-1, 1, A_WIDTH), ln_b.reshape(-1, 1, A_WIDTH), w_s, bst, pool_w,
      pool_scale.reshape(-1, 1, B_WIDTH))


def _t5_bucket_np(rel):
    half = REL_BUCKETS // 2
    max_exact = half // 2
    rel = np.asarray(rel, dtype=np.int64)
    n = np.abs(rel)
    steps = half - max_exact
    ratio = REL_MAX_DIST // max_exact
    large = np.zeros_like(n)
    nn = n.astype(object) ** steps
    for j in range(1, steps + 1):
        large = large + (nn >= (max_exact ** steps) * (ratio ** j)).astype(np.int64)
    large = np.minimum(max_exact + large, half - 1)
    return np.where(rel > 0, half, 0) + np.where(n < max_exact, n, large)


def _bias_bucket_tiles():
    qi = np.arange(ATT_TQ)[:, None]
    kj = np.arange(ATT_TQ)[None, :]
    diag = _t5_bucket_np(kj - qi)
    diag = np.where((kj // CHUNK) <= (qi // CHUNK), diag, -1)
    prev = _t5_bucket_np(kj - ATT_TQ - qi)
    return np.stack([diag, prev], 0).astype(np.int32)


FAR_BUCKET = REL_BUCKETS // 2 - 1


def _bias_kernel(rel_ref, bk_ref, o_ref):
    h = pl.program_id(0)
    bk = bk_ref[...]
    acc = jnp.zeros(bk.shape, F32)
    for b in range(REL_BUCKETS):
        acc = jnp.where(bk == b, rel_ref[b, h], acc)
    o_ref[...] = jnp.where(bk < 0, NEG, acc - rel_ref[FAR_BUCKET, h])


def attention_bias_tiles(rel_table):
    assert ATT_TQ >= REL_MAX_DIST and ATT_TQ % CHUNK == 0
    bk = jnp.asarray(_bias_bucket_tiles())
    return pl.pallas_call(
        _bias_kernel,
        out_shape=jax.ShapeDtypeStruct((C_HEADS, 2, ATT_TQ, ATT_TQ), F32),
        grid=(C_HEADS,),
        in_specs=[
            pl.BlockSpec(memory_space=pltpu.SMEM),
            pl.BlockSpec((2, ATT_TQ, ATT_TQ), lambda h: (0, 0, 0)),
        ],
        out_specs=pl.BlockSpec((None, 2, ATT_TQ, ATT_TQ), lambda h: (h, 0, 0, 0)),
        compiler_params=_params(1, 16 << 20),
        name="attention_bias_tiles",
    )(rel_table, bk)


def _diff_attn_kernel(q_ref, k_ref, v_ref, bias_ref, lam_ref, g_ref, o_ref, *, lam_init, seq_len):
    tq = ATT_TQ
    nt = (((1,), (1,)), ((), ()))
    lp = lam_ref[...]
    lam = (jnp.exp(jnp.sum(lp[0:1] * lp[1:2], axis=1, keepdims=True))
           - jnp.exp(jnp.sum(lp[2:3] * lp[3:4], axis=1, keepdims=True)) + lam_init)
    lane = lax.broadcasted_iota(jnp.int32, (tq, C_V_DIM), 1)
    for qi in range(seq_len // tq):
        q = q_ref[qi * tq:(qi + 1) * tq, :] * (C_HEAD_DIM ** -0.5)
        zero = jnp.zeros_like(q)
        parts = []
        if qi >= 2:
            parts.append((0, (qi - 1) * tq, None))
        if qi >= 1:
            parts.append(((qi - 1) * tq, qi * tq, 1))
        parts.append((qi * tq, (qi + 1) * tq, 0))
        outs = []
        for half in range(2):
            keep = (lane < C_HEAD_DIM) if half == 0 else (lane >= C_HEAD_DIM)
            qh = jnp.where(keep, q, zero)
            scores = []
            m = None
            for a, b, bi in parts:
                s = lax.dot_general(qh, k_ref[a:b, :], nt, preferred_element_type=F32)
                if bi is not None:
                    s = s + bias_ref[bi]
                scores.append(s)
                sm = jnp.max(s, axis=1, keepdims=True)
                m = sm if m is None else jnp.maximum(m, sm)
            l = None
            acc = None
            for s, (a, b, _) in zip(scores, parts):
                p = jnp.exp(s - m)
                ps = jnp.sum(p, axis=1, keepdims=True)
                pv = jnp.dot(p.astype(BF16), v_ref[a:b, :], preferred_element_type=F32)
                l = ps if l is None else l + ps
                acc = pv if acc is None else acc + pv
            outs.append(acc / l)
        o = outs[0] - lam * outs[1]
        y = o * lax.rsqrt(jnp.mean(o * o, axis=-1, keepdims=True) + LN_EPS) * g_ref[...]
        o_ref[qi * tq:(qi + 1) * tq, :] = (y * (1.0 - lam_init)).astype(BF16)


def diff_attention(qkv, bias_tiles, lam_params, subln_g, layer, lam_init, batch, seq_len):
    n = qkv.shape[0]
    tq = ATT_TQ
    assert seq_len % tq == 0
    kernel = functools.partial(_diff_attn_kernel, lam_init=lam_init, seq_len=seq_len)
    seq = lambda off: pl.BlockSpec((seq_len, C_V_DIM), lambda b, h: (b, off + h))
    return pl.pallas_call(
        kernel,
        out_shape=jax.ShapeDtypeStruct((n, D), BF16),
        grid=(batch, C_HEADS),
        in_specs=[
            seq(0), seq(C_HEADS), seq(2 * C_HEADS),
            pl.BlockSpec((None, 2, tq, tq), lambda b, h: (h, 0, 0, 0)),
            pl.BlockSpec((None, 4, C_HEAD_DIM), lambda b, h: (layer, 0, 0)),
            pl.BlockSpec((None, 1, C_V_DIM), lambda b, h: (layer, 0, 0)),
        ],
        out_specs=seq(0),
        compiler_params=_params(2, 40 << 20),
        name="diff_attention",
    )(qkv, qkv, qkv, bias_tiles, lam_params, subln_g.reshape(-1, 1, C_V_DIM))


def _xattn_kernel(q_ref, kv_ref, o_ref):
    scale = X_HEAD_DIM ** -0.5
    for h in range(X_HEADS):
        c0 = h * X_HEAD_DIM
        q = q_ref[:, c0:c0 + X_HEAD_DIM]
        k = kv_ref[:, c0:c0 + X_HEAD_DIM]
        v = kv_ref[:, D + c0:D + c0 + X_HEAD_DIM]
        s = lax.dot_general(q, k, (((1,), (1,)), ((), ())), preferred_element_type=F32) * scale
        m = jnp.max(s, axis=1, keepdims=True)
        p = jnp.exp(s - m)
        l = jnp.sum(p, axis=1, keepdims=True)
        o = jnp.dot(p.astype(BF16), v, preferred_element_type=F32) / l
        o_ref[:, c0:c0 + X_HEAD_DIM] = o.astype(BF16)


def cross_attention(q, kv, batch, seq_len, mem_len, *, tm=512):
    n = q.shape[0]
    blocks_per_seq = seq_len // tm
    return pl.pallas_call(
        _xattn_kernel,
        out_shape=jax.ShapeDtypeStruct((n, D), BF16),
        grid=(n // tm,),
        in_specs=[
            pl.BlockSpec((tm, D), lambda i: (i, 0)),
            pl.BlockSpec((mem_len, 2 * D), lambda i: (i // blocks_per_seq, 0)),
        ],
        out_specs=pl.BlockSpec((tm, D), lambda i: (i, 0)),
        compiler_params=_params(1, 32 << 20),
        name="cross_attention",
    )(q, kv)


ROUTE_T = 256


def _router_kernel(h_ref, wrt_ref, e_ref, w_ref, rank_ref, cnt_ref, base_ref):
    i = pl.program_id(0)

    @pl.when(i == 0)
    def _():
        base_ref[...] = jnp.zeros(base_ref.shape, F32)

    t = ROUTE_T
    logits = lax.dot_general(wrt_ref[...], h_ref[...], (((1,), (1,)), ((), ())),
                             precision=lax.Precision.HIGHEST, preferred_element_type=F32)
    eidx = lax.broadcasted_iota(jnp.int32, (N_EXPERTS, t), 0)
    m1 = jnp.max(logits, axis=0, keepdims=True)
    e1 = jnp.min(jnp.where(logits == m1, eidx, N_EXPERTS), axis=0, keepdims=True)
    oh1 = eidx == e1
    rest = jnp.where(oh1, -jnp.inf, logits)
    m2 = jnp.max(rest, axis=0, keepdims=True)
    e2 = jnp.min(jnp.where(rest == m2, eidx, N_EXPERTS), axis=0, keepdims=True)
    oh2 = eidx == e2
    ex = jnp.exp(m2 - m1)
    den = 1.0 + ex
    e_ref[0:1, :] = e1
    e_ref[1:2, :] = e2
    w_ref[0:1, :] = 1.0 / den
    w_ref[1:2, :] = ex / den

    oh = (oh1.astype(F32) + oh2.astype(F32)).astype(BF16)
    r = lax.broadcasted_iota(jnp.int32, (t, t), 0)
    c = lax.broadcasted_iota(jnp.int32, (t, t), 1)
    before = (r < c).astype(BF16)
    ones = jnp.ones((t, t), BF16)
    pos = base_ref[...] + jnp.dot(oh, before, preferred_element_type=F32)
    rank_ref[0:1, :] = jnp.sum(jnp.where(oh1, pos, 0.0), axis=0, keepdims=True).astype(jnp.int32)
    rank_ref[1:2, :] = jnp.sum(jnp.where(oh2, pos, 0.0), axis=0, keepdims=True).astype(jnp.int32)
    total = base_ref[...] + jnp.dot(oh, ones, preferred_element_type=F32)
    base_ref[...] = total
    cnt_ref[...] = total[:, 0:128].astype(jnp.int32)


def moe_router(h, w_router, layer):
    n = h.shape[0]
    t = ROUTE_T
    wrt = jnp.swapaxes(w_router, 1, 2)
    pair = pl.BlockSpec((TOP_K, t), lambda i: (0, i))
    return pl.pallas_call(
        _router_kernel,
        out_shape=(
            jax.ShapeDtypeStruct((TOP_K, n), jnp.int32),
            jax.ShapeDtypeStruct((TOP_K, n), F32),
            jax.ShapeDtypeStruct((TOP_K, n), jnp.int32),
            jax.ShapeDtypeStruct((N_EXPERTS, 128), jnp.int32),
        ),
        grid=(n // t,),
        in_specs=[
            pl.BlockSpec((t, D), lambda i: (i, 0)),
            pl.BlockSpec((None, N_EXPERTS, D), lambda i: (layer, 0, 0)),
        ],
        out_specs=(pair, pair, pair, pl.BlockSpec((N_EXPERTS, 128), lambda i: (0, 0))),
        scratch_shapes=[pltpu.VMEM((N_EXPERTS, t), F32)],
        compiler_params=_params(1, 16 << 20),
        name="moe_router",
    )(h, wrt)


def _row_copy(src_ref, src_row, dst_ref, dst_row, sem):
    return pltpu.make_async_copy(src_ref.at[pl.ds(src_row, 1)], dst_ref.at[pl.ds(dst_row, 1)], sem)


DISPATCH_T = 256


def _dispatch_kernel(dest_ref, h_ref, xb_in_hbm, xb_hbm, sem, *, n_tokens):
    del xb_in_hbm
    i = pl.program_id(0)
    t = DISPATCH_T

    def start(r, carry):
        tok = i * t + r
        _row_copy(h_ref, r, xb_hbm, dest_ref[tok], sem).start()
        _row_copy(h_ref, r, xb_hbm, dest_ref[n_tokens + tok], sem).start()
        return carry

    lax.fori_loop(0, t, start, 0)

    def wait(r, carry):
        _row_copy(h_ref, r, xb_hbm, 0, sem).wait()
        _row_copy(h_ref, r, xb_hbm, 0, sem).wait()
        return carry

    lax.fori_loop(0, t, wait, 0)


def moe_dispatch(h, dest, n_blocks):
    n, d = h.shape
    t = DISPATCH_T
    rows = n_blocks * EXPERT_BLOCK
    kernel = functools.partial(_dispatch_kernel, n_tokens=n)
    return pl.pallas_call(
        kernel,
        out_shape=jax.ShapeDtypeStruct((rows, d), F32),
        grid_spec=pltpu.PrefetchScalarGridSpec(
            num_scalar_prefetch=1,
            grid=(n // t,),
            in_specs=[pl.BlockSpec((t, d), lambda i, ds: (i, 0)), pl.BlockSpec(memory_space=pl.ANY)],
            out_specs=pl.BlockSpec(memory_space=pl.ANY),
            scratch_shapes=[pltpu.SemaphoreType.DMA(())],
        ),
        input_output_aliases={2: 0},
        compiler_params=_params(1, 16 << 20),
        name="moe_dispatch",
    )(dest, h, jnp.zeros((rows, d), F32))


def _expert_changed(be_ref, i):
    return jnp.logical_or(i == 0, be_ref[i] != be_ref[jnp.maximum(i - 1, 0)])


def _moe_up_kernel(be_ref, nused_ref, x_ref, wg_ref, wu_ref, o_ref, wgb_ref, wub_ref):
    i = pl.program_id(1)

    @pl.when(_expert_changed(be_ref, i))
    def _():
        _cast_rows(wg_ref, wgb_ref)
        _cast_rows(wu_ref, wub_ref)

    @pl.when(i < nused_ref[0])
    def _():
        x = x_ref[...].astype(BF16)
        g = jnp.dot(x, wgb_ref[...], preferred_element_type=F32)
        u = jnp.dot(x, wub_ref[...], preferred_element_type=F32)
        o_ref[...] = (jax.nn.silu(g) * u).astype(o_ref.dtype)

    @pl.when(i >= nused_ref[0])
    def _():
        o_ref[...] = jnp.zeros(o_ref.shape, o_ref.dtype)


def moe_up(xb, w_gate, w_up, layer, block_e, nused, *, tn=1408):
    p, k = xb.shape
    rows = EXPERT_BLOCK
    f = w_gate.shape[-1]
    assert f % tn == 0
    wspec = pl.BlockSpec((None, None, k, tn), lambda j, i, be, nu: (layer, be[i], 0, j),
                         pipeline_mode=pl.Buffered(1))
    return pl.pallas_call(
        _moe_up_kernel,
        out_shape=jax.ShapeDtypeStruct((p, f), BF16),
        grid_spec=pltpu.PrefetchScalarGridSpec(
            num_scalar_prefetch=2,
            grid=(f // tn, p // rows),
            in_specs=[pl.BlockSpec((rows, k), lambda j, i, be, nu: (i, 0)), wspec, wspec],
            out_specs=pl.BlockSpec((rows, tn), lambda j, i, be, nu: (i, j)),
            scratch_shapes=[pltpu.VMEM((k, tn), BF16), pltpu.VMEM((k, tn), BF16)],
        ),
        compiler_params=_params(2, 52 << 20),
        name="moe_up",
    )(block_e, nused, xb, w_gate, w_up)


def _moe_down_kernel(be_ref, nused_ref, x_ref, w_ref, o_ref, wb_ref):
    i = pl.program_id(1)

    @pl.when(_expert_changed(be_ref, i))
    def _():
        _cast_rows(w_ref, wb_ref)

    @pl.when(i < nused_ref[0])
    def _():
        o_ref[...] = jnp.dot(x_ref[...], wb_ref[...], preferred_element_type=F32)

    @pl.when(i >= nused_ref[0])
    def _():
        o_ref[...] = jnp.zeros(o_ref.shape, o_ref.dtype)


def moe_down(hb, w_down, layer, block_e, nused, *, tn=1024):
    p, k = hb.shape
    rows = EXPERT_BLOCK
    n = w_down.shape[-1]
    return pl.pallas_call(
        _moe_down_kernel,
        out_shape=jax.ShapeDtypeStruct((p, n), F32),
        grid_spec=pltpu.PrefetchScalarGridSpec(
            num_scalar_prefetch=2,
            grid=(n // tn, p // rows),
            in_specs=[
                pl.BlockSpec((rows, k), lambda j, i, be, nu: (i, 0)),
                pl.BlockSpec((None, None, k, tn), lambda j, i, be, nu: (layer, be[i], 0, j)),
            ],
            out_specs=pl.BlockSpec((rows, tn), lambda j, i, be, nu: (i, j)),
            scratch_shapes=[pltpu.VMEM((k, tn), BF16)],
        ),
        compiler_params=_params(2, 44 << 20),
        name="moe_down",
    )(block_e, nused, hb, w_down)


COMBINE_T = 256


def _combine_kernel(dest_ref, y_hbm, wt_ref, h_ref, g_ref, b_ref, of_ref, ob_ref,
                    buf0_ref, buf1_ref, sem, *, n_tokens):
    i = pl.program_id(0)
    t = COMBINE_T

    def start(r, carry):
        tok = i * t + r
        _row_copy(y_hbm, dest_ref[tok], buf0_ref, r, sem).start()
        _row_copy(y_hbm, dest_ref[n_tokens + tok], buf1_ref, r, sem).start()
        return carry

    lax.fori_loop(0, t, start, 0)

    def wait(r, carry):
        _row_copy(y_hbm, 0, buf0_ref, r, sem).wait()
        _row_copy(y_hbm, 0, buf1_ref, r, sem).wait()
        return carry

    lax.fori_loop(0, t, wait, 0)
    ff = buf0_ref[...] * wt_ref[:, 0:1] + buf1_ref[...] * wt_ref[:, 1:2]
    y = _layer_norm(DN_ALPHA * h_ref[...] + ff, g_ref[...], b_ref[...])
    of_ref[...] = y
    ob_ref[...] = y.astype(BF16)


def moe_combine_ln(yb, dest, wts_t, h, ln_g, ln_b, layer, which):
    n, d = h.shape
    t = COMBINE_T
    row = pl.BlockSpec((t, d), lambda i, ds: (i, 0))
    par = pl.BlockSpec((None, None, 1, d), lambda i, ds: (layer, which, 0, 0))
    kernel = functools.partial(_combine_kernel, n_tokens=n)
    return pl.pallas_call(
        kernel,
        out_shape=(jax.ShapeDtypeStruct((n, d), F32), jax.ShapeDtypeStruct((n, d), BF16)),
        grid_spec=pltpu.PrefetchScalarGridSpec(
            num_scalar_prefetch=1,
            grid=(n // t,),
            in_specs=[
                pl.BlockSpec(memory_space=pl.ANY),
                pl.BlockSpec((t, TOP_K), lambda i, ds: (i, 0)),
                row, par, par,
            ],
            out_specs=(row, row),
            scratch_shapes=[pltpu.VMEM((t, d), F32), pltpu.VMEM((t, d), F32),
                            pltpu.SemaphoreType.DMA(())],
        ),
        compiler_params=_params(1, 24 << 20),
        name="moe_combine_ln",
    )(dest, yb, wts_t, h, ln_g, ln_b)


def moe_layer(h, hb, w_router, w_gate, w_up, w_down, ln_g, ln_b, layer, moe_idx):
    n = h.shape[0]
    m = n * TOP_K
    n_blocks = (m + N_EXPERTS * (EXPERT_BLOCK - 1) + EXPERT_BLOCK - 1) // EXPERT_BLOCK
    e, wts, rank, cnt = moe_router(h, w_router, moe_idx)
    counts = cnt[:, 0]
    padded = (counts + EXPERT_BLOCK - 1) // EXPERT_BLOCK * EXPERT_BLOCK
    pad_ends = jnp.cumsum(padded)
    pad_starts = pad_ends - padded
    expert_ids = jnp.arange(N_EXPERTS, dtype=jnp.int32)[:, None, None]
    dest = jnp.sum(jnp.where(e[None] == expert_ids, pad_starts[:, None, None], 0), axis=0) + rank
    block_start = jnp.arange(n_blocks, dtype=jnp.int32) * EXPERT_BLOCK
    block_e = jnp.minimum(jnp.sum(block_start[:, None] >= pad_ends[None, :], axis=1), N_EXPERTS - 1).astype(jnp.int32)
    nused = (pad_ends[-1:] // EXPERT_BLOCK).astype(jnp.int32)

    dest = dest.reshape(-1)
    xb = moe_dispatch(h, dest, n_blocks)
    hid = moe_up(xb, w_gate, w_up, moe_idx, block_e, nused)
    yb = moe_down(hid, w_down, moe_idx, block_e, nused)
    return moe_combine_ln(yb, dest, wts.T, h, ln_g, ln_b, layer, 2)


def kernel(x, mem, rel_table, mix_w_in, gmlp_ln_g, gmlp_ln_b, gmlp_w_s, gmlp_b_s, pool_w, pool_scale, mix_w_out, diff_w_qkv, diff_lam_q1, diff_lam_k1, diff_lam_q2, diff_lam_k2, diff_subln_g, diff_w_o, xa_w_q, xa_w_kv, xa_w_o, ffn_w_gate, ffn_w_up, ffn_w_down, moe_w_router, moe_w_gate, moe_w_up, moe_w_down, ln_g, ln_b):
    batch, seq_len, d = x.shape
    mem_len = mem.shape[1]
    n = batch * seq_len
    h = x.reshape(n, d)
    hb = h
    memf = mem.reshape(batch * mem_len, d)
    ln_g4 = ln_g.reshape(DEPTH, 3, 1, d)
    ln_b4 = ln_b.reshape(DEPTH, 3, 1, d)
    lam_params = jnp.stack([diff_lam_q1, diff_lam_k1, diff_lam_q2, diff_lam_k2], axis=1)
    bias_tiles = attention_bias_tiles(rel_table)

    for layer in range(DEPTH):
        i = layer // 2
        if layer % 2 == 0:
            hm = matmul(hb, mix_w_in, i, tm=512, tn=1024, out_dtype=F32)
            mixed = gmlp_pool_mixer(hm, seq_len, gmlp_ln_g, gmlp_ln_b, gmlp_w_s, gmlp_b_s, pool_w,
                                    pool_scale, i)
            mix = matmul(mixed, mix_w_out, i, tm=512, tn=1024, out_dtype=F32)
        else:
            lam_init = 0.8 - 0.6 * math.exp(-0.3 * layer)
            qkv = matmul(hb, diff_w_qkv, i, tm=512, tn=1024, out_dtype=BF16)
            att = diff_attention(qkv, bias_tiles, lam_params, diff_subln_g, i, lam_init, batch, seq_len)
            mix = matmul(att, diff_w_o, i, tm=512, tn=1024, out_dtype=F32)
        h, hb = residual_ln(h, mix, ln_g4, ln_b4, layer, 0)

        q = matmul(hb, xa_w_q, layer, tm=512, tn=1024, out_dtype=BF16)
        kv = matmul(memf, xa_w_kv, layer, tm=512, tn=1024, out_dtype=BF16)
        xo = cross_attention(q, kv, batch, seq_len, mem_len)
        xa = matmul(xo, xa_w_o, layer, tm=512, tn=1024, out_dtype=F32)
        h, hb = residual_ln(h, xa, ln_g4, ln_b4, layer, 1)

        if layer % 2 == 0:
            hid = swiglu_up(hb, ffn_w_gate, ffn_w_up, i, tm=512, tn=512)
            ff = matmul(hid, ffn_w_down, i, tm=512, tn=512, out_dtype=F32)
            h, hb = residual_ln(h, ff, ln_g4, ln_b4, layer, 2)
        else:
            h, hb = moe_layer(h, hb, moe_w_router, moe_w_gate, moe_w_up, moe_w_down, ln_g4, ln_b4,
                              layer, i)
    return h.reshape(batch, seq_len, d)
```

```python
import functools
import math

import numpy as np
import jax
import jax.numpy as jnp
from jax import lax
from jax.experimental import pallas as pl
from jax.experimental.pallas import tpu as pltpu

F32 = jnp.float32
BF16 = jnp.bfloat16

D = 2048
DEPTH = 4
CHUNK = 64
GMLP_CHUNK = 128
A_WIDTH = D // 2
A_HEADS = 8
A_HEAD_DIM = A_WIDTH // A_HEADS
B_WIDTH = D // 2
POOL_WINDOWS = (2, 4, 8, 16)
B_GROUP = B_WIDTH // len(POOL_WINDOWS)
MIX_IN = 2 * A_WIDTH + B_WIDTH
C_HEADS = 16
C_HEAD_DIM = D // (2 * C_HEADS)
C_V_DIM = 2 * C_HEAD_DIM
REL_BUCKETS = 32
REL_MAX_DIST = 128
X_HEADS = 4
X_HEAD_DIM = D // X_HEADS
N_EXPERTS = 8
TOP_K = 2
EXPERT_BLOCK = 256
DN_ALPHA = (2 * DEPTH) ** 0.25
LN_EPS = 1e-5
NEG = -1e30

VMEM_CAP_BYTES = 56 * 1024 * 1024
CAST_ROWS = 256
ATT_TQ = 256
POOL_HALO = 16


def _params(n_axes, vmem_bytes):
    return pltpu.CompilerParams(
        dimension_semantics=("arbitrary",) * n_axes,
        vmem_limit_bytes=int(min(max(vmem_bytes, 16 * 1024 * 1024), VMEM_CAP_BYTES)),
    )


def _cast_rows(w_ref, wb_ref):
    k = w_ref.shape[0]
    rows = CAST_ROWS if k % CAST_ROWS == 0 else k

    def body(c, carry):
        r = pl.multiple_of(c * rows, rows)
        wb_ref[pl.ds(r, rows), :] = w_ref[pl.ds(r, rows), :].astype(BF16)
        return carry

    lax.fori_loop(0, k // rows, body, 0)


def _layer_norm(x, g, b):
    mu = jnp.mean(x, axis=-1, keepdims=True)
    xc = x - mu
    var = jnp.mean(xc * xc, axis=-1, keepdims=True)
    return xc * lax.rsqrt(var + LN_EPS) * g + b


def _mm_kernel(a_ref, w_ref, o_ref, wb_ref):
    @pl.when(pl.program_id(1) == 0)
    def _():
        _cast_rows(w_ref, wb_ref)

    a = a_ref[...].astype(BF16)
    o_ref[...] = jnp.dot(a, wb_ref[...], preferred_element_type=F32).astype(o_ref.dtype)


def matmul(a, w, layer, *, tm, tn, out_dtype):
    m, k = a.shape
    n = w.shape[-1]
    assert m % tm == 0 and n % tn == 0 and w.shape[-2] == k
    vmem = 2 * tm * k * a.dtype.itemsize + 2 * k * tn * 4 + k * tn * 2 + 2 * tm * tn * 4 + tm * tn * 4
    return pl.pallas_call(
        _mm_kernel,
        out_shape=jax.ShapeDtypeStruct((m, n), out_dtype),
        grid=(n // tn, m // tm),
        in_specs=[
            pl.BlockSpec((tm, k), lambda j, i: (i, 0)),
            pl.BlockSpec((None, k, tn), lambda j, i: (layer, 0, j)),
        ],
        out_specs=pl.BlockSpec((tm, tn), lambda j, i: (i, j)),
        scratch_shapes=[pltpu.VMEM((k, tn), BF16)],
        compiler_params=_params(2, vmem + (4 << 20)),
        name="matmul",
    )(a, w)


def _mm_ln_kernel(a_ref, w_ref, h_ref, g_ref, b_ref, of_ref, ob_ref, wb_ref):
    @pl.when(pl.program_id(0) == 0)
    def _():
        _cast_rows(w_ref, wb_ref)

    y = jnp.dot(a_ref[...], wb_ref[...], preferred_element_type=F32)
    out = _layer_norm(DN_ALPHA * h_ref[...] + y, g_ref[...], b_ref[...])
    of_ref[...] = out
    ob_ref[...] = out.astype(BF16)


def matmul_residual_ln(a, w, w_layer, h, ln_g, ln_b, layer, which, *, tm=256):
    m, k = a.shape
    d = w.shape[-1]
    assert m % tm == 0 and h.shape == (m, d)
    row = pl.BlockSpec((tm, d), lambda i: (i, 0))
    par = pl.BlockSpec((None, None, 1, d), lambda i: (layer, which, 0, 0))
    vmem = k * d * 4 + k * d * 2 + 2 * tm * k * 2 + 4 * tm * d * 4 + 2 * tm * d * 2 + 3 * tm * d * 4
    return pl.pallas_call(
        _mm_ln_kernel,
        out_shape=(jax.ShapeDtypeStruct((m, d), F32), jax.ShapeDtypeStruct((m, d), BF16)),
        grid=(m // tm,),
        in_specs=[
            pl.BlockSpec((tm, k), lambda i: (i, 0)),
            pl.BlockSpec((None, k, d), lambda i: (w_layer, 0, 0), pipeline_mode=pl.Buffered(1)),
            row, par, par,
        ],
        out_specs=(row, row),
        scratch_shapes=[pltpu.VMEM((k, d), BF16)],
        compiler_params=_params(1, vmem + (4 << 20)),
        name="matmul_residual_ln",
    )(a, w, h, ln_g, ln_b)


def _swiglu_kernel(a_ref, wg_ref, wu_ref, o_ref, wgb_ref, wub_ref):
    @pl.when(pl.program_id(1) == 0)
    def _():
        _cast_rows(wg_ref, wgb_ref)
        _cast_rows(wu_ref, wub_ref)

    a = a_ref[...]
    g = jnp.dot(a, wgb_ref[...], preferred_element_type=F32)
    u = jnp.dot(a, wub_ref[...], preferred_element_type=F32)
    o_ref[...] = (jax.nn.silu(g) * u).astype(o_ref.dtype)


def swiglu_up(a, wg, wu, layer, *, tm, tn):
    m, k = a.shape
    n = wg.shape[-1]
    assert m % tm == 0 and n % tn == 0
    vmem = 2 * tm * k * 2 + 4 * k * tn * 4 + 2 * k * tn * 2 + 2 * tm * tn * 2 + 3 * tm * tn * 4
    wspec = pl.BlockSpec((None, k, tn), lambda j, i: (layer, 0, j))
    return pl.pallas_call(
        _swiglu_kernel,
        out_shape=jax.ShapeDtypeStruct((m, n), BF16),
        grid=(n // tn, m // tm),
        in_specs=[pl.BlockSpec((tm, k), lambda j, i: (i, 0)), wspec, wspec],
        out_specs=pl.BlockSpec((tm, tn), lambda j, i: (i, j)),
        scratch_shapes=[pltpu.VMEM((k, tn), BF16), pltpu.VMEM((k, tn), BF16)],
        compiler_params=_params(2, vmem + (4 << 20)),
        name="swiglu_up",
    )(a, wg, wu)


def _ln_kernel(h_ref, y_ref, g_ref, b_ref, of_ref, ob_ref):
    x = DN_ALPHA * h_ref[...] + y_ref[...]
    y = _layer_norm(x, g_ref[...], b_ref[...])
    of_ref[...] = y
    ob_ref[...] = y.astype(BF16)


def residual_ln(h, y, ln_g, ln_b, layer, which, *, tm=256):
    m, d = h.shape
    row = pl.BlockSpec((tm, d), lambda i: (i, 0))
    par = pl.BlockSpec((None, None, 1, d), lambda i: (layer, which, 0, 0))
    return pl.pallas_call(
        _ln_kernel,
        out_shape=(jax.ShapeDtypeStruct((m, d), F32), jax.ShapeDtypeStruct((m, d), BF16)),
        grid=(m // tm,),
        in_specs=[row, row, par, par],
        out_specs=(row, row),
        compiler_params=_params(1, 16 * tm * d * 4),
        name="residual_ln",
    )(h, y, ln_g, ln_b)


MIX_ROWS = 2 * GMLP_CHUNK


def _mixer_kernel(hm_ref, prev_ref, lng_ref, lnb_ref, ws_ref, bst_ref, pw_ref, ps_ref, o_ref,
                  wbd_ref, ext_ref, *, blocks_per_seq):
    i = pl.program_id(0)

    @pl.when(i == 0)
    def _():
        p = lax.broadcasted_iota(jnp.int32, (GMLP_CHUNK, GMLP_CHUNK), 0)
        q = lax.broadcasted_iota(jnp.int32, (GMLP_CHUNK, GMLP_CHUNK), 1)
        allowed = (q // CHUNK) <= (p // CHUNK)
        wbd_ref[...] = jnp.zeros(wbd_ref.shape, BF16)
        for h in range(A_HEADS):
            w = jnp.where(allowed, ws_ref[h], 0.0).astype(BF16)
            wbd_ref[h, 0:GMLP_CHUNK, 0:GMLP_CHUNK] = w
            wbd_ref[h, GMLP_CHUNK:MIX_ROWS, GMLP_CHUNK:MIX_ROWS] = w

    z = jax.nn.gelu(hm_ref[:, 0:2 * A_WIDTH])
    u = z[:, 0:A_WIDTH]
    v = _layer_norm(z[:, A_WIDTH:2 * A_WIDTH], lng_ref[...], lnb_ref[...]).astype(BF16)
    for h in range(A_HEADS):
        c0 = h * A_HEAD_DIM
        sv = jnp.dot(wbd_ref[h], v[:, c0:c0 + A_HEAD_DIM], preferred_element_type=F32)
        sv = sv + bst_ref[:, h:h + 1]
        o_ref[:, c0:c0 + A_HEAD_DIM] = (u[:, c0:c0 + A_HEAD_DIM] * sv).astype(BF16)

    blk = i % blocks_per_seq
    prev = jnp.where(blk == 0, 0.0, prev_ref[...])
    ext_ref[0:POOL_HALO, :] = prev
    ext_ref[POOL_HALO:POOL_HALO + MIX_ROWS, :] = hm_ref[:, 2 * A_WIDTH:MIX_IN]
    t1 = blk * MIX_ROWS + lax.broadcasted_iota(jnp.int32, (MIX_ROWS, 1), 0) + 1
    for g, w in enumerate(POOL_WINDOWS):
        c0 = g * B_GROUP
        pin = ext_ref[POOL_HALO:POOL_HALO + MIX_ROWS, c0:c0 + B_GROUP]
        s = pin
        for k in range(1, w):
            s = s + ext_ref[POOL_HALO - k:POOL_HALO - k + MIX_ROWS, c0:c0 + B_GROUP]
        cnt = jnp.minimum(t1, w).astype(F32)
        pooled = s / cnt - pin
        b = jnp.dot(pooled.astype(BF16), pw_ref[g].astype(BF16), preferred_element_type=F32)
        b = b * ps_ref[:, c0:c0 + B_GROUP]
        o_ref[:, A_WIDTH + c0:A_WIDTH + c0 + B_GROUP] = b.astype(BF16)


def gmlp_pool_mixer(hm, seq_len, ln_g, ln_b, w_s, b_s, pool_w, pool_scale, layer):
    n = hm.shape[0]
    assert seq_len % MIX_ROWS == 0 and max(POOL_WINDOWS) <= POOL_HALO
    blocks_per_seq = seq_len // MIX_ROWS
    halo_per_block = MIX_ROWS // POOL_HALO
    bst = jnp.tile(b_s[layer].T, (MIX_ROWS // GMLP_CHUNK, 1))
    kernel = functools.partial(_mixer_kernel, blocks_per_seq=blocks_per_seq)
    return pl.pallas_call(
        kernel,
        out_shape=jax.ShapeDtypeStruct((n, D), BF16),
        grid=(n // MIX_ROWS,),
        in_specs=[
            pl.BlockSpec((MIX_ROWS, MIX_IN), lambda i: (i, 0)),
            pl.BlockSpec((POOL_HALO, B_WIDTH),
                         lambda i: (jnp.maximum(i * halo_per_block - 1, 0), 2 * A_WIDTH // B_WIDTH)),
            pl.BlockSpec((None, 1, A_WIDTH), lambda i: (layer, 0, 0)),
            pl.BlockSpec((None, 1, A_WIDTH), lambda i: (layer, 0, 0)),
            pl.BlockSpec((None, A_HEADS, GMLP_CHUNK, GMLP_CHUNK), lambda i: (layer, 0, 0, 0)),
            pl.BlockSpec((MIX_ROWS, A_HEADS), lambda i: (0, 0)),
            pl.BlockSpec((None, len(POOL_WINDOWS), B_GROUP, B_GROUP), lambda i: (layer, 0, 0, 0)),
            pl.BlockSpec((None, 1, B_WIDTH), lambda i: (layer, 0, 0)),
        ],
        out_specs=pl.BlockSpec((MIX_ROWS, D), lambda i: (i, 0)),
        scratch_shapes=[
            pltpu.VMEM((A_HEADS, MIX_ROWS, MIX_ROWS), BF16),
            pltpu.VMEM((POOL_HALO + MIX_ROWS, B_WIDTH), F32),
        ],
        compiler_params=_params(1, 40 << 20),
        name="gmlp_pool_mixer",
    )(hm, hm, ln_g.reshape(-1, 1, A_WIDTH), ln_b.reshape(-1, 1, A_WIDTH), w_s, bst, pool_w,
      pool_scale.reshape(-1, 1, B_WIDTH))


def _t5_bucket_np(rel):
    half = REL_BUCKETS // 2
    max_exact = half // 2
    rel = np.asarray(rel, dtype=np.int64)
    n = np.abs(rel)
    steps = half - max_exact
    ratio = REL_MAX_DIST // max_exact
    large = np.zeros_like(n)
    nn = n.astype(object) ** steps
    for j in range(1, steps + 1):
        large = large + (nn >= (max_exact ** steps) * (ratio ** j)).astype(np.int64)
    large = np.minimum(max_exact + large, half - 1)
    return np.where(rel > 0, half, 0) + np.where(n < max_exact, n, large)


def _bias_bucket_tiles():
    qi = np.arange(ATT_TQ)[:, None]
    kj = np.arange(ATT_TQ)[None, :]
    diag = _t5_bucket_np(kj - qi)
    diag = np.where((kj // CHUNK) <= (qi // CHUNK), diag, -1)
    prev = _t5_bucket_np(kj - ATT_TQ - qi)
    return np.stack([diag, prev], 0).astype(np.int32)


FAR_BUCKET = REL_BUCKETS // 2 - 1


def _bias_kernel(rel_ref, bk_ref, o_ref):
    h = pl.program_id(0)
    bk = bk_ref[...]
    acc = jnp.zeros(bk.shape, F32)
    for b in range(REL_BUCKETS):
        acc = jnp.where(bk == b, rel_ref[b, h], acc)
    o_ref[...] = jnp.where(bk < 0, NEG, acc - rel_ref[FAR_BUCKET, h])


def attention_bias_tiles(rel_table):
    assert ATT_TQ >= REL_MAX_DIST and ATT_TQ % CHUNK == 0
    bk = jnp.asarray(_bias_bucket_tiles())
    return pl.pallas_call(
        _bias_kernel,
        out_shape=jax.ShapeDtypeStruct((C_HEADS, 2, ATT_TQ, ATT_TQ), F32),
        grid=(C_HEADS,),
        in_specs=[
            pl.BlockSpec(memory_space=pltpu.SMEM),
            pl.BlockSpec((2, ATT_TQ, ATT_TQ), lambda h: (0, 0, 0)),
        ],
        out_specs=pl.BlockSpec((None, 2, ATT_TQ, ATT_TQ), lambda h: (h, 0, 0, 0)),
        compiler_params=_params(1, 16 << 20),
        name="attention_bias_tiles",
    )(rel_table, bk)


def _diff_attn_kernel(q_ref, k_ref, v_ref, bias_ref, lam_ref, g_ref, o_ref, *, lam_init, seq_len):
    tq = ATT_TQ
    nt = (((1,), (1,)), ((), ()))
    lp = lam_ref[...]
    lam = (jnp.exp(jnp.sum(lp[0:1] * lp[1:2], axis=1, keepdims=True))
           - jnp.exp(jnp.sum(lp[2:3] * lp[3:4], axis=1, keepdims=True)) + lam_init)
    lane = lax.broadcasted_iota(jnp.int32, (tq, C_V_DIM), 1)
    for qi in range(seq_len // tq):
        q = q_ref[qi * tq:(qi + 1) * tq, :] * (C_HEAD_DIM ** -0.5)
        zero = jnp.zeros_like(q)
        parts = []
        if qi >= 2:
            parts.append((0, (qi - 1) * tq, None))
        if qi >= 1:
            parts.append(((qi - 1) * tq, qi * tq, 1))
        parts.append((qi * tq, (qi + 1) * tq, 0))
        outs = []
        for half in range(2):
            keep = (lane < C_HEAD_DIM) if half == 0 else (lane >= C_HEAD_DIM)
            qh = jnp.where(keep, q, zero)
            scores = []
            m = None
            for a, b, bi in parts:
                s = lax.dot_general(qh, k_ref[a:b, :], nt, preferred_element_type=F32)
                if bi is not None:
                    s = s + bias_ref[bi]
                scores.append(s)
                sm = jnp.max(s, axis=1, keepdims=True)
                m = sm if m is None else jnp.maximum(m, sm)
            l = None
            acc = None
            for s, (a, b, _) in zip(scores, parts):
                p = jnp.exp(s - m)
                ps = jnp.sum(p, axis=1, keepdims=True)
                pv = jnp.dot(p.astype(BF16), v_ref[a:b, :], preferred_element_type=F32)
                l = ps if l is None else l + ps
                acc = pv if acc is None else acc + pv
            outs.append(acc / l)
        o = outs[0] - lam * outs[1]
        y = o * lax.rsqrt(jnp.mean(o * o, axis=-1, keepdims=True) + LN_EPS) * g_ref[...]
        o_ref[qi * tq:(qi + 1) * tq, :] = (y * (1.0 - lam_init)).astype(BF16)


def diff_attention(qkv, bias_tiles, lam_params, subln_g, layer, lam_init, batch, seq_len):
    n = qkv.shape[0]
    tq = ATT_TQ
    assert seq_len % tq == 0
    kernel = functools.partial(_diff_attn_kernel, lam_init=lam_init, seq_len=seq_len)
    seq = lambda off: pl.BlockSpec((seq_len, C_V_DIM), lambda b, h: (b, off + h))
    return pl.pallas_call(
        kernel,
        out_shape=jax.ShapeDtypeStruct((n, D), BF16),
        grid=(batch, C_HEADS),
        in_specs=[
            seq(0), seq(C_HEADS), seq(2 * C_HEADS),
            pl.BlockSpec((None, 2, tq, tq), lambda b, h: (h, 0, 0, 0)),
            pl.BlockSpec((None, 4, C_HEAD_DIM), lambda b, h: (layer, 0, 0)),
            pl.BlockSpec((None, 1, C_V_DIM), lambda b, h: (layer, 0, 0)),
        ],
        out_specs=seq(0),
        compiler_params=_params(2, 40 << 20),
        name="diff_attention",
    )(qkv, qkv, qkv, bias_tiles, lam_params, subln_g.reshape(-1, 1, C_V_DIM))


def _xattn_kernel(q_ref, kv_ref, o_ref):
    scale = X_HEAD_DIM ** -0.5
    for h in range(X_HEADS):
        c0 = h * X_HEAD_DIM
        q = q_ref[:, c0:c0 + X_HEAD_DIM]
        k = kv_ref[:, c0:c0 + X_HEAD_DIM]
        v = kv_ref[:, D + c0:D + c0 + X_HEAD_DIM]
        s = lax.dot_general(q, k, (((1,), (1,)), ((), ())), preferred_element_type=F32) * scale
        m = jnp.max(s, axis=1, keepdims=True)
        p = jnp.exp(s - m)
        l = jnp.sum(p, axis=1, keepdims=True)
        o = jnp.dot(p.astype(BF16), v, preferred_element_type=F32) / l
        o_ref[:, c0:c0 + X_HEAD_DIM] = o.astype(BF16)


def cross_attention(q, kv, batch, seq_len, mem_len, *, tm=512):
    n = q.shape[0]
    blocks_per_seq = seq_len // tm
    return pl.pallas_call(
        _xattn_kernel,
        out_shape=jax.ShapeDtypeStruct((n, D), BF16),
        grid=(n // tm,),
        in_specs=[
            pl.BlockSpec((tm, D), lambda i: (i, 0)),
            pl.BlockSpec((mem_len, 2 * D), lambda i: (i // blocks_per_seq, 0)),
        ],
        out_specs=pl.BlockSpec((tm, D), lambda i: (i, 0)),
        compiler_params=_params(1, 32 << 20),
        name="cross_attention",
    )(q, kv)


ROUTE_T = 256


def _router_kernel(h_ref, wrt_ref, e_ref, w_ref, rank_ref, cnt_ref, base_ref):
    i = pl.program_id(0)

    @pl.when(i == 0)
    def _():
        base_ref[...] = jnp.zeros(base_ref.shape, F32)

    t = ROUTE_T
    logits = lax.dot_general(wrt_ref[...], h_ref[...], (((1,), (1,)), ((), ())),
                             precision=lax.Precision.HIGHEST, preferred_element_type=F32)
    eidx = lax.broadcasted_iota(jnp.int32, (N_EXPERTS, t), 0)
    m1 = jnp.max(logits, axis=0, keepdims=True)
    e1 = jnp.min(jnp.where(logits == m1, eidx, N_EXPERTS), axis=0, keepdims=True)
    oh1 = eidx == e1
    rest = jnp.where(oh1, -jnp.inf, logits)
    m2 = jnp.max(rest, axis=0, keepdims=True)
    e2 = jnp.min(jnp.where(rest == m2, eidx, N_EXPERTS), axis=0, keepdims=True)
    oh2 = eidx == e2
    ex = jnp.exp(m2 - m1)
    den = 1.0 + ex
    e_ref[0:1, :] = e1
    e_ref[1:2, :] = e2
    w_ref[0:1, :] = 1.0 / den
    w_ref[1:2, :] = ex / den

    oh = (oh1.astype(F32) + oh2.astype(F32)).astype(BF16)
    r = lax.broadcasted_iota(jnp.int32, (t, t), 0)
    c = lax.broadcasted_iota(jnp.int32, (t, t), 1)
    before = (r < c).astype(BF16)
    ones = jnp.ones((t, t), BF16)
    pos = base_ref[...] + jnp.dot(oh, before, preferred_element_type=F32)
    rank_ref[0:1, :] = jnp.sum(jnp.where(oh1, pos, 0.0), axis=0, keepdims=True).astype(jnp.int32)
    rank_ref[1:2, :] = jnp.sum(jnp.where(oh2, pos, 0.0), axis=0, keepdims=True).astype(jnp.int32)
    total = base_ref[...] + jnp.dot(oh, ones, preferred_element_type=F32)
    base_ref[...] = total
    cnt_ref[...] = total[:, 0:128].astype(jnp.int32)


def moe_router(h, w_router, layer):
    n = h.shape[0]
    t = ROUTE_T
    wrt = jnp.swapaxes(w_router, 1, 2)
    pair = pl.BlockSpec((TOP_K, t), lambda i: (0, i))
    return pl.pallas_call(
        _router_kernel,
        out_shape=(
            jax.ShapeDtypeStruct((TOP_K, n), jnp.int32),
            jax.ShapeDtypeStruct((TOP_K, n), F32),
            jax.ShapeDtypeStruct((TOP_K, n), jnp.int32),
            jax.ShapeDtypeStruct((N_EXPERTS, 128), jnp.int32),
        ),
        grid=(n // t,),
        in_specs=[
            pl.BlockSpec((t, D), lambda i: (i, 0)),
            pl.BlockSpec((None, N_EXPERTS, D), lambda i: (layer, 0, 0)),
        ],
        out_specs=(pair, pair, pair, pl.BlockSpec((N_EXPERTS, 128), lambda i: (0, 0))),
        scratch_shapes=[pltpu.VMEM((N_EXPERTS, t), F32)],
        compiler_params=_params(1, 16 << 20),
        name="moe_router",
    )(h, wrt)


def _row_copy(src_ref, src_row, dst_ref, dst_row, sem):
    return pltpu.make_async_copy(src_ref.at[pl.ds(src_row, 1)], dst_ref.at[pl.ds(dst_row, 1)], sem)


DISPATCH_T = 256


def _dispatch_kernel(dest_ref, h_hbm, xb_in_hbm, xb_hbm, sem, *, n_tokens):
    del xb_in_hbm
    i = pl.program_id(0)
    t = DISPATCH_T

    def start(r, carry):
        tok = i * t + r
        _row_copy(h_hbm, tok, xb_hbm, dest_ref[tok], sem).start(priority=0)
        _row_copy(h_hbm, tok, xb_hbm, dest_ref[n_tokens + tok], sem).start(priority=1)
        return carry

    def wait_step(r, carry):
        _row_copy(h_hbm, 0, xb_hbm, 0, sem).wait()
        _row_copy(h_hbm, 0, xb_hbm, 0, sem).wait()
        return carry

    lax.fori_loop(0, t, start, 0)

    @pl.when(i > 0)
    def _():
        lax.fori_loop(0, t, wait_step, 0)

    @pl.when(i == pl.num_programs(0) - 1)
    def _():
        lax.fori_loop(0, t, wait_step, 0)


def moe_dispatch(h, dest, n_blocks):
    n, d = h.shape
    t = DISPATCH_T
    rows = n_blocks * EXPERT_BLOCK
    kernel = functools.partial(_dispatch_kernel, n_tokens=n)
    return pl.pallas_call(
        kernel,
        out_shape=jax.ShapeDtypeStruct((rows, d), F32),
        grid_spec=pltpu.PrefetchScalarGridSpec(
            num_scalar_prefetch=1,
            grid=(n // t,),
            in_specs=[pl.BlockSpec(memory_space=pl.ANY), pl.BlockSpec(memory_space=pl.ANY)],
            out_specs=pl.BlockSpec(memory_space=pl.ANY),
            scratch_shapes=[pltpu.SemaphoreType.DMA(())],
        ),
        input_output_aliases={2: 0},
        compiler_params=_params(1, 16 << 20),
        name="moe_dispatch",
    )(dest, h, jnp.zeros((rows, d), F32))


def _expert_changed(be_ref, i):
    return jnp.logical_or(i == 0, be_ref[i] != be_ref[jnp.maximum(i - 1, 0)])


def _moe_up_kernel(be_ref, nused_ref, x_ref, wg_ref, wu_ref, o_ref, wgb_ref, wub_ref):
    i = pl.program_id(1)

    @pl.when(_expert_changed(be_ref, i))
    def _():
        _cast_rows(wg_ref, wgb_ref)
        _cast_rows(wu_ref, wub_ref)

    @pl.when(i < nused_ref[0])
    def _():
        x = x_ref[...].astype(BF16)
        g = jnp.dot(x, wgb_ref[...], preferred_element_type=F32)
        u = jnp.dot(x, wub_ref[...], preferred_element_type=F32)
        o_ref[...] = (jax.nn.silu(g) * u).astype(o_ref.dtype)

    @pl.when(i >= nused_ref[0])
    def _():
        o_ref[...] = jnp.zeros(o_ref.shape, o_ref.dtype)


def moe_up(xb, w_gate, w_up, layer, block_e, nused, *, tn=1408):
    p, k = xb.shape
    rows = EXPERT_BLOCK
    f = w_gate.shape[-1]
    assert f % tn == 0
    wspec = pl.BlockSpec((None, None, k, tn), lambda j, i, be, nu: (layer, be[i], 0, j),
                         pipeline_mode=pl.Buffered(1))
    return pl.pallas_call(
        _moe_up_kernel,
        out_shape=jax.ShapeDtypeStruct((p, f), BF16),
        grid_spec=pltpu.PrefetchScalarGridSpec(
            num_scalar_prefetch=2,
            grid=(f // tn, p // rows),
            in_specs=[pl.BlockSpec((rows, k), lambda j, i, be, nu: (i, 0)), wspec, wspec],
            out_specs=pl.BlockSpec((rows, tn), lambda j, i, be, nu: (i, j)),
            scratch_shapes=[pltpu.VMEM((k, tn), BF16), pltpu.VMEM((k, tn), BF16)],
        ),
        compiler_params=_params(2, 52 << 20),
        name="moe_up",
    )(block_e, nused, xb, w_gate, w_up)


def _moe_down_kernel(be_ref, nused_ref, x_ref, w_ref, o_ref, wb_ref):
    i = pl.program_id(1)

    @pl.when(_expert_changed(be_ref, i))
    def _():
        _cast_rows(w_ref, wb_ref)

    @pl.when(i < nused_ref[0])
    def _():
        o_ref[...] = jnp.dot(x_ref[...], wb_ref[...], preferred_element_type=F32)

    @pl.when(i >= nused_ref[0])
    def _():
        o_ref[...] = jnp.zeros(o_ref.shape, o_ref.dtype)


def moe_down(hb, w_down, layer, block_e, nused, *, tn=1024):
    p, k = hb.shape
    rows = EXPERT_BLOCK
    n = w_down.shape[-1]
    return pl.pallas_call(
        _moe_down_kernel,
        out_shape=jax.ShapeDtypeStruct((p, n), F32),
        grid_spec=pltpu.PrefetchScalarGridSpec(
            num_scalar_prefetch=2,
            grid=(n // tn, p // rows),
            in_specs=[
                pl.BlockSpec((rows, k), lambda j, i, be, nu: (i, 0)),
                pl.BlockSpec((None, None, k, tn), lambda j, i, be, nu: (layer, be[i], 0, j)),
            ],
            out_specs=pl.BlockSpec((rows, tn), lambda j, i, be, nu: (i, j)),
            scratch_shapes=[pltpu.VMEM((k, tn), BF16)],
        ),
        compiler_params=_params(2, 44 << 20),
        name="moe_down",
    )(block_e, nused, hb, w_down)


COMBINE_T = 256


def _combine_kernel(dest_ref, y_hbm, wt_ref, h_ref, g_ref, b_ref, of_ref, ob_ref,
                    buf_ref, sems, *, n_tokens):
    i = pl.program_id(0)
    t = COMBINE_T

    def issue(block, slot):
        def body(r, carry):
            tok = block * t + r
            _row_copy(y_hbm, dest_ref[tok], buf_ref.at[slot, 0], r, sems.at[slot]).start(priority=0)
            _row_copy(y_hbm, dest_ref[n_tokens + tok], buf_ref.at[slot, 1], r, sems.at[slot]).start(priority=1)
            return carry

        lax.fori_loop(0, t, body, 0)

    slot = i % 2

    @pl.when(i == 0)
    def _():
        issue(0, 0)

    @pl.when(i + 1 < pl.num_programs(0))
    def _():
        issue(i + 1, 1 - slot)

    def wait(r, carry):
        _row_copy(y_hbm, 0, buf_ref.at[slot, 0], r, sems.at[slot]).wait()
        _row_copy(y_hbm, 0, buf_ref.at[slot, 1], r, sems.at[slot]).wait()
        return carry

    lax.fori_loop(0, t, wait, 0)
    ff = buf_ref[slot, 0] * wt_ref[:, 0:1] + buf_ref[slot, 1] * wt_ref[:, 1:2]
    y = _layer_norm(DN_ALPHA * h_ref[...] + ff, g_ref[...], b_ref[...])
    of_ref[...] = y
    ob_ref[...] = y.astype(BF16)


def moe_combine_ln(yb, dest, wts_t, h, ln_g, ln_b, layer, which):
    n, d = h.shape
    t = COMBINE_T
    row = pl.BlockSpec((t, d), lambda i, ds: (i, 0))
    par = pl.BlockSpec((None, None, 1, d), lambda i, ds: (layer, which, 0, 0))
    kernel = functools.partial(_combine_kernel, n_tokens=n)
    return pl.pallas_call(
        kernel,
        out_shape=(jax.ShapeDtypeStruct((n, d), F32), jax.ShapeDtypeStruct((n, d), BF16)),
        grid_spec=pltpu.PrefetchScalarGridSpec(
            num_scalar_prefetch=1,
            grid=(n // t,),
            in_specs=[
                pl.BlockSpec(memory_space=pl.ANY),
                pl.BlockSpec((t, TOP_K), lambda i, ds: (i, 0)),
                row, par, par,
            ],
            out_specs=(row, row),
            scratch_shapes=[pltpu.VMEM((2, TOP_K, t, d), F32), pltpu.SemaphoreType.DMA((2,))],
        ),
        compiler_params=_params(1, 32 << 20),
        name="moe_combine_ln",
    )(dest, yb, wts_t, h, ln_g, ln_b)


def moe_layer(h, hb, w_router, w_gate, w_up, w_down, ln_g, ln_b, layer, moe_idx):
    n = h.shape[0]
    m = n * TOP_K
    n_blocks = (m + N_EXPERTS * (EXPERT_BLOCK - 1) + EXPERT_BLOCK - 1) // EXPERT_BLOCK
    e, wts, rank, cnt = moe_router(h, w_router, moe_idx)
    counts = cnt[:, 0]
    padded = (counts + EXPERT_BLOCK - 1) // EXPERT_BLOCK * EXPERT_BLOCK
    pad_ends = jnp.cumsum(padded)
    pad_starts = pad_ends - padded
    expert_ids = jnp.arange(N_EXPERTS, dtype=jnp.int32)[:, None, None]
    dest = jnp.sum(jnp.where(e[None] == expert_ids, pad_starts[:, None, None], 0), axis=0) + rank
    block_start = jnp.arange(n_blocks, dtype=jnp.int32) * EXPERT_BLOCK
    block_e = jnp.minimum(jnp.sum(block_start[:, None] >= pad_ends[None, :], axis=1), N_EXPERTS - 1).astype(jnp.int32)
    nused = (pad_ends[-1:] // EXPERT_BLOCK).astype(jnp.int32)

    dest = dest.reshape(-1)
    xb = moe_dispatch(h, dest, n_blocks)
    hid = moe_up(xb, w_gate, w_up, moe_idx, block_e, nused)
    yb = moe_down(hid, w_down, moe_idx, block_e, nused)
    return moe_combine_ln(yb, dest, wts.T, h, ln_g, ln_b, layer, 2)


def kernel(x, mem, rel_table, mix_w_in, gmlp_ln_g, gmlp_ln_b, gmlp_w_s, gmlp_b_s, pool_w, pool_scale, mix_w_out, diff_w_qkv, diff_lam_q1, diff_lam_k1, diff_lam_q2, diff_lam_k2, diff_subln_g, diff_w_o, xa_w_q, xa_w_kv, xa_w_o, ffn_w_gate, ffn_w_up, ffn_w_down, moe_w_router, moe_w_gate, moe_w_up, moe_w_down, ln_g, ln_b):
    batch, seq_len, d = x.shape
    mem_len = mem.shape[1]
    n = batch * seq_len
    h = x.reshape(n, d)
    hb = h
    memf = mem.reshape(batch * mem_len, d)
    ln_g4 = ln_g.reshape(DEPTH, 3, 1, d)
    ln_b4 = ln_b.reshape(DEPTH, 3, 1, d)
    lam_params = jnp.stack([diff_lam_q1, diff_lam_k1, diff_lam_q2, diff_lam_k2], axis=1)
    bias_tiles = attention_bias_tiles(rel_table)

    for layer in range(DEPTH):
        i = layer // 2
        if layer % 2 == 0:
            hm = matmul(hb, mix_w_in, i, tm=512, tn=1024, out_dtype=F32)
            mixed = gmlp_pool_mixer(hm, seq_len, gmlp_ln_g, gmlp_ln_b, gmlp_w_s, gmlp_b_s, pool_w,
                                    pool_scale, i)
            h, hb = matmul_residual_ln(mixed, mix_w_out, i, h, ln_g4, ln_b4, layer, 0)
        else:
            lam_init = 0.8 - 0.6 * math.exp(-0.3 * layer)
            qkv = matmul(hb, diff_w_qkv, i, tm=1024, tn=1024, out_dtype=BF16)
            att = diff_attention(qkv, bias_tiles, lam_params, diff_subln_g, i, lam_init, batch, seq_len)
            h, hb = matmul_residual_ln(att, diff_w_o, i, h, ln_g4, ln_b4, layer, 0)

        q = matmul(hb, xa_w_q, layer, tm=512, tn=1024, out_dtype=BF16)
        kv = matmul(memf, xa_w_kv, layer, tm=512, tn=1024, out_dtype=BF16)
        xo = cross_attention(q, kv, batch, seq_len, mem_len)
        h, hb = matmul_residual_ln(xo, xa_w_o, layer, h, ln_g4, ln_b4, layer, 1)

        if layer % 2 == 0:
            hid = swiglu_up(hb, ffn_w_gate, ffn_w_up, i, tm=1024, tn=512)
            ff = matmul(hid, ffn_w_down, i, tm=512, tn=512, out_dtype=F32)
            h, hb = residual_ln(h, ff, ln_g4, ln_b4, layer, 2)
        else:
            h, hb = moe_layer(h, hb, moe_w_router, moe_w_gate, moe_w_up, moe_w_down, ln_g4, ln_b4,
                              layer, i)
    return h.reshape(batch, seq_len, d)
```

```python
import functools
import math

import numpy as np
import jax
import jax.numpy as jnp
from jax import lax
from jax.experimental import pallas as pl
from jax.experimental.pallas import tpu as pltpu

F32 = jnp.float32
BF16 = jnp.bfloat16

D = 2048
DEPTH = 4
CHUNK = 64
GMLP_CHUNK = 128
A_WIDTH = D // 2
A_HEADS = 8
A_HEAD_DIM = A_WIDTH // A_HEADS
B_WIDTH = D // 2
POOL_WINDOWS = (2, 4, 8, 16)
B_GROUP = B_WIDTH // len(POOL_WINDOWS)
MIX_IN = 2 * A_WIDTH + B_WIDTH
C_HEADS = 16
C_HEAD_DIM = D // (2 * C_HEADS)
C_V_DIM = 2 * C_HEAD_DIM
REL_BUCKETS = 32
REL_MAX_DIST = 128
X_HEADS = 4
X_HEAD_DIM = D // X_HEADS
N_EXPERTS = 8
TOP_K = 2
EXPERT_BLOCK = 256
DN_ALPHA = (2 * DEPTH) ** 0.25
LN_EPS = 1e-5
NEG = -1e30

VMEM_CAP_BYTES = 56 * 1024 * 1024
CAST_ROWS = 256
ATT_TQ = 256
POOL_HALO = 16


def _params(n_axes, vmem_bytes):
    return pltpu.CompilerParams(
        dimension_semantics=("arbitrary",) * n_axes,
        vmem_limit_bytes=int(min(max(vmem_bytes, 16 * 1024 * 1024), VMEM_CAP_BYTES)),
    )


def _cast_rows(w_ref, wb_ref):
    k = w_ref.shape[0]
    rows = CAST_ROWS if k % CAST_ROWS == 0 else k

    def body(c, carry):
        r = pl.multiple_of(c * rows, rows)
        wb_ref[pl.ds(r, rows), :] = w_ref[pl.ds(r, rows), :].astype(BF16)
        return carry

    lax.fori_loop(0, k // rows, body, 0)


def _layer_norm(x, g, b):
    mu = jnp.mean(x, axis=-1, keepdims=True)
    xc = x - mu
    var = jnp.mean(xc * xc, axis=-1, keepdims=True)
    return xc * lax.rsqrt(var + LN_EPS) * g + b


def _mm_kernel(a_ref, w_ref, o_ref, wb_ref):
    @pl.when(pl.program_id(1) == 0)
    def _():
        _cast_rows(w_ref, wb_ref)

    a = a_ref[...].astype(BF16)
    o_ref[...] = jnp.dot(a, wb_ref[...], preferred_element_type=F32).astype(o_ref.dtype)


def matmul(a, w, layer, *, tm, tn, out_dtype):
    m, k = a.shape
    n = w.shape[-1]
    assert m % tm == 0 and n % tn == 0 and w.shape[-2] == k
    vmem = 2 * tm * k * a.dtype.itemsize + 2 * k * tn * 4 + k * tn * 2 + 2 * tm * tn * 4 + tm * tn * 4
    return pl.pallas_call(
        _mm_kernel,
        out_shape=jax.ShapeDtypeStruct((m, n), out_dtype),
        grid=(n // tn, m // tm),
        in_specs=[
            pl.BlockSpec((tm, k), lambda j, i: (i, 0)),
            pl.BlockSpec((None, k, tn), lambda j, i: (layer, 0, j)),
        ],
        out_specs=pl.BlockSpec((tm, tn), lambda j, i: (i, j)),
        scratch_shapes=[pltpu.VMEM((k, tn), BF16)],
        compiler_params=_params(2, vmem + (4 << 20)),
        name="matmul",
    )(a, w)


MMLN_GROUPS = 2


def _mm_ln_kernel(a_ref, w_ref, h_ref, g_ref, b_ref, of_ref, ob_ref, wb_ref):
    @pl.when(pl.program_id(0) == 0)
    def _():
        _cast_rows(w_ref, wb_ref)

    rows = a_ref.shape[0] // MMLN_GROUPS
    for r in range(MMLN_GROUPS):
        sl = slice(r * rows, (r + 1) * rows)
        y = jnp.dot(a_ref[sl, :], wb_ref[...], preferred_element_type=F32)
        out = _layer_norm(DN_ALPHA * h_ref[sl, :] + y, g_ref[...], b_ref[...])
        of_ref[sl, :] = out
        ob_ref[sl, :] = out.astype(BF16)


def matmul_residual_ln(a, w, w_layer, h, ln_g, ln_b, layer, which, *, tm=512):
    m, k = a.shape
    d = w.shape[-1]
    assert m % tm == 0 and h.shape == (m, d)
    row = pl.BlockSpec((tm, d), lambda i: (i, 0))
    par = pl.BlockSpec((None, None, 1, d), lambda i: (layer, which, 0, 0))
    vmem = k * d * 4 + k * d * 2 + 2 * tm * k * 2 + 4 * tm * d * 4 + 2 * tm * d * 2 + 3 * tm * d * 4
    return pl.pallas_call(
        _mm_ln_kernel,
        out_shape=(jax.ShapeDtypeStruct((m, d), F32), jax.ShapeDtypeStruct((m, d), BF16)),
        grid=(m // tm,),
        in_specs=[
            pl.BlockSpec((tm, k), lambda i: (i, 0)),
            pl.BlockSpec((None, k, d), lambda i: (w_layer, 0, 0), pipeline_mode=pl.Buffered(1)),
            row, par, par,
        ],
        out_specs=(row, row),
        scratch_shapes=[pltpu.VMEM((k, d), BF16)],
        compiler_params=_params(1, vmem + (4 << 20)),
        name="matmul_residual_ln",
    )(a, w, h, ln_g, ln_b)


def _swiglu_kernel(a_ref, wg_ref, wu_ref, o_ref, wgb_ref, wub_ref):
    @pl.when(pl.program_id(1) == 0)
    def _():
        _cast_rows(wg_ref, wgb_ref)
        _cast_rows(wu_ref, wub_ref)

    a = a_ref[...]
    g = jnp.dot(a, wgb_ref[...], preferred_element_type=F32)
    u = jnp.dot(a, wub_ref[...], preferred_element_type=F32)
    o_ref[...] = (jax.nn.silu(g) * u).astype(o_ref.dtype)


def swiglu_up(a, wg, wu, layer, *, tm, tn):
    m, k = a.shape
    n = wg.shape[-1]
    assert m % tm == 0 and n % tn == 0
    vmem = 2 * tm * k * 2 + 4 * k * tn * 4 + 2 * k * tn * 2 + 2 * tm * tn * 2 + 3 * tm * tn * 4
    wspec = pl.BlockSpec((None, k, tn), lambda j, i: (layer, 0, j))
    return pl.pallas_call(
        _swiglu_kernel,
        out_shape=jax.ShapeDtypeStruct((m, n), BF16),
        grid=(n // tn, m // tm),
        in_specs=[pl.BlockSpec((tm, k), lambda j, i: (i, 0)), wspec, wspec],
        out_specs=pl.BlockSpec((tm, tn), lambda j, i: (i, j)),
        scratch_shapes=[pltpu.VMEM((k, tn), BF16), pltpu.VMEM((k, tn), BF16)],
        compiler_params=_params(2, vmem + (4 << 20)),
        name="swiglu_up",
    )(a, wg, wu)


def _ln_kernel(h_ref, y_ref, g_ref, b_ref, of_ref, ob_ref):
    x = DN_ALPHA * h_ref[...] + y_ref[...]
    y = _layer_norm(x, g_ref[...], b_ref[...])
    of_ref[...] = y
    ob_ref[...] = y.astype(BF16)


def residual_ln(h, y, ln_g, ln_b, layer, which, *, tm=256):
    m, d = h.shape
    row = pl.BlockSpec((tm, d), lambda i: (i, 0))
    par = pl.BlockSpec((None, None, 1, d), lambda i: (layer, which, 0, 0))
    return pl.pallas_call(
        _ln_kernel,
        out_shape=(jax.ShapeDtypeStruct((m, d), F32), jax.ShapeDtypeStruct((m, d), BF16)),
        grid=(m // tm,),
        in_specs=[row, row, par, par],
        out_specs=(row, row),
        compiler_params=_params(1, 16 * tm * d * 4),
        name="residual_ln",
    )(h, y, ln_g, ln_b)


MIX_ROWS = 2 * GMLP_CHUNK


def _mixer_kernel(hm_ref, prev_ref, lng_ref, lnb_ref, ws_ref, bst_ref, pw_ref, ps_ref, o_ref,
                  wbd_ref, ext_ref, *, blocks_per_seq):
    i = pl.program_id(0)

    @pl.when(i == 0)
    def _():
        p = lax.broadcasted_iota(jnp.int32, (GMLP_CHUNK, GMLP_CHUNK), 0)
        q = lax.broadcasted_iota(jnp.int32, (GMLP_CHUNK, GMLP_CHUNK), 1)
        allowed = (q // CHUNK) <= (p // CHUNK)
        wbd_ref[...] = jnp.zeros(wbd_ref.shape, BF16)
        for h in range(A_HEADS):
            w = jnp.where(allowed, ws_ref[h], 0.0).astype(BF16)
            wbd_ref[h, 0:GMLP_CHUNK, 0:GMLP_CHUNK] = w
            wbd_ref[h, GMLP_CHUNK:MIX_ROWS, GMLP_CHUNK:MIX_ROWS] = w

    z = jax.nn.gelu(hm_ref[:, 0:2 * A_WIDTH])
    u = z[:, 0:A_WIDTH]
    v = _layer_norm(z[:, A_WIDTH:2 * A_WIDTH], lng_ref[...], lnb_ref[...]).astype(BF16)
    for h in range(A_HEADS):
        c0 = h * A_HEAD_DIM
        sv = jnp.dot(wbd_ref[h], v[:, c0:c0 + A_HEAD_DIM], preferred_element_type=F32)
        sv = sv + bst_ref[:, h:h + 1]
        o_ref[:, c0:c0 + A_HEAD_DIM] = (u[:, c0:c0 + A_HEAD_DIM] * sv).astype(BF16)

    blk = i % blocks_per_seq
    prev = jnp.where(blk == 0, 0.0, prev_ref[...])
    ext_ref[0:POOL_HALO, :] = prev
    ext_ref[POOL_HALO:POOL_HALO + MIX_ROWS, :] = hm_ref[:, 2 * A_WIDTH:MIX_IN]
    t1 = blk * MIX_ROWS + lax.broadcasted_iota(jnp.int32, (MIX_ROWS, 1), 0) + 1
    for g, w in enumerate(POOL_WINDOWS):
        c0 = g * B_GROUP
        pin = ext_ref[POOL_HALO:POOL_HALO + MIX_ROWS, c0:c0 + B_GROUP]
        s = pin
        for k in range(1, w):
            s = s + ext_ref[POOL_HALO - k:POOL_HALO - k + MIX_ROWS, c0:c0 + B_GROUP]
        cnt = jnp.minimum(t1, w).astype(F32)
        pooled = s / cnt - pin
        b = jnp.dot(pooled.astype(BF16), pw_ref[g].astype(BF16), preferred_element_type=F32)
        b = b * ps_ref[:, c0:c0 + B_GROUP]
        o_ref[:, A_WIDTH + c0:A_WIDTH + c0 + B_GROUP] = b.astype(BF16)


def gmlp_pool_mixer(hm, seq_len, ln_g, ln_b, w_s, b_s, pool_w, pool_scale, layer):
    n = hm.shape[0]
    assert seq_len % MIX_ROWS == 0 and max(POOL_WINDOWS) <= POOL_HALO
    blocks_per_seq = seq_len // MIX_ROWS
    halo_per_block = MIX_ROWS // POOL_HALO
    bst = jnp.tile(b_s[layer].T, (MIX_ROWS // GMLP_CHUNK, 1))
    kernel = functools.partial(_mixer_kernel, blocks_per_seq=blocks_per_seq)
    return pl.pallas_call(
        kernel,
        out_shape=jax.ShapeDtypeStruct((n, D), BF16),
        grid=(n // MIX_ROWS,),
        in_specs=[
            pl.BlockSpec((MIX_ROWS, MIX_IN), lambda i: (i, 0)),
            pl.BlockSpec((POOL_HALO, B_WIDTH),
                         lambda i: (jnp.maximum(i * halo_per_block - 1, 0), 2 * A_WIDTH // B_WIDTH)),
            pl.BlockSpec((None, 1, A_WIDTH), lambda i: (layer, 0, 0)),
            pl.BlockSpec((None, 1, A_WIDTH), lambda i: (layer, 0, 0)),
            pl.BlockSpec((None, A_HEADS, GMLP_CHUNK, GMLP_CHUNK), lambda i: (layer, 0, 0, 0)),
            pl.BlockSpec((MIX_ROWS, A_HEADS), lambda i: (0, 0)),
            pl.BlockSpec((None, len(POOL_WINDOWS), B_GROUP, B_GROUP), lambda i: (layer, 0, 0, 0)),
            pl.BlockSpec((None, 1, B_WIDTH), lambda i: (layer, 0, 0)),
        ],
        out_specs=pl.BlockSpec((MIX_ROWS, D), lambda i: (i, 0)),
        scratch_shapes=[
            pltpu.VMEM((A_HEADS, MIX_ROWS, MIX_ROWS), BF16),
            pltpu.VMEM((POOL_HALO + MIX_ROWS, B_WIDTH), F32),
        ],
        compiler_params=_params(1, 40 << 20),
        name="gmlp_pool_mixer",
    )(hm, hm, ln_g.reshape(-1, 1, A_WIDTH), ln_b.reshape(-1, 1, A_WIDTH), w_s, bst, pool_w,
      pool_scale.reshape(-1, 1, B_WIDTH))


def _t5_bucket_np(rel):
    half = REL_BUCKETS // 2
    max_exact = half // 2
    rel = np.asarray(rel, dtype=np.int64)
    n = np.abs(rel)
    steps = half - max_exact
    ratio = REL_MAX_DIST // max_exact
    large = np.zeros_like(n)
    nn = n.astype(object) ** steps
    for j in range(1, steps + 1):
        large = large + (nn >= (max_exact ** steps) * (ratio ** j)).astype(np.int64)
    large = np.minimum(max_exact + large, half - 1)
    return np.where(rel > 0, half, 0) + np.where(n < max_exact, n, large)


def _bias_bucket_tiles():
    qi = np.arange(ATT_TQ)[:, None]
    kj = np.arange(ATT_TQ)[None, :]
    diag = _t5_bucket_np(kj - qi)
    diag = np.where((kj // CHUNK) <= (qi // CHUNK), diag, -1)
    prev = _t5_bucket_np(kj - ATT_TQ - qi)
    return np.stack([diag, prev], 0).astype(np.int32)


FAR_BUCKET = REL_BUCKETS // 2 - 1


def _bias_kernel(rel_ref, bk_ref, o_ref):
    h = pl.program_id(0)
    bk = bk_ref[...]
    acc = jnp.zeros(bk.shape, F32)
    for b in range(REL_BUCKETS):
        acc = jnp.where(bk == b, rel_ref[b, h], acc)
    o_ref[...] = jnp.where(bk < 0, NEG, acc - rel_ref[FAR_BUCKET, h])


def attention_bias_tiles(rel_table):
    assert ATT_TQ >= REL_MAX_DIST and ATT_TQ % CHUNK == 0
    bk = jnp.asarray(_bias_bucket_tiles())
    return pl.pallas_call(
        _bias_kernel,
        out_shape=jax.ShapeDtypeStruct((C_HEADS, 2, ATT_TQ, ATT_TQ), F32),
        grid=(C_HEADS,),
        in_specs=[
            pl.BlockSpec(memory_space=pltpu.SMEM),
            pl.BlockSpec((2, ATT_TQ, ATT_TQ), lambda h: (0, 0, 0)),
        ],
        out_specs=pl.BlockSpec((None, 2, ATT_TQ, ATT_TQ), lambda h: (h, 0, 0, 0)),
        compiler_params=_params(1, 16 << 20),
        name="attention_bias_tiles",
    )(rel_table, bk)


def _diff_attn_kernel(q_ref, k_ref, v_ref, bias_ref, lam_ref, g_ref, o_ref, *, lam_init, seq_len):
    tq = ATT_TQ
    nt = (((1,), (1,)), ((), ()))
    lp = lam_ref[...]
    lam = (jnp.exp(jnp.sum(lp[0:1] * lp[1:2], axis=1, keepdims=True))
           - jnp.exp(jnp.sum(lp[2:3] * lp[3:4], axis=1, keepdims=True)) + lam_init)
    lane = lax.broadcasted_iota(jnp.int32, (tq, C_V_DIM), 1)
    for qi in range(seq_len // tq):
        q = q_ref[qi * tq:(qi + 1) * tq, :] * (C_HEAD_DIM ** -0.5)
        zero = jnp.zeros_like(q)
        parts = []
        if qi >= 2:
            parts.append((0, (qi - 1) * tq, None))
        if qi >= 1:
            parts.append(((qi - 1) * tq, qi * tq, 1))
        parts.append((qi * tq, (qi + 1) * tq, 0))
        outs = []
        for half in range(2):
            keep = (lane < C_HEAD_DIM) if half == 0 else (lane >= C_HEAD_DIM)
            qh = jnp.where(keep, q, zero)
            scores = []
            m = None
            for a, b, bi in parts:
                s = lax.dot_general(qh, k_ref[a:b, :], nt, preferred_element_type=F32)
                if bi is not None:
                    s = s + bias_ref[bi]
                scores.append(s)
                sm = jnp.max(s, axis=1, keepdims=True)
                m = sm if m is None else jnp.maximum(m, sm)
            l = None
            acc = None
            for s, (a, b, _) in zip(scores, parts):
                p = jnp.exp(s - m)
                ps = jnp.sum(p, axis=1, keepdims=True)
                pv = jnp.dot(p.astype(BF16), v_ref[a:b, :], preferred_element_type=F32)
                l = ps if l is None else l + ps
                acc = pv if acc is None else acc + pv
            outs.append(acc / l)
        o = outs[0] - lam * outs[1]
        y = o * lax.rsqrt(jnp.mean(o * o, axis=-1, keepdims=True) + LN_EPS) * g_ref[...]
        o_ref[qi * tq:(qi + 1) * tq, :] = (y * (1.0 - lam_init)).astype(BF16)


def diff_attention(qkv, bias_tiles, lam_params, subln_g, layer, lam_init, batch, seq_len):
    n = qkv.shape[0]
    tq = ATT_TQ
    assert seq_len % tq == 0
    kernel = functools.partial(_diff_attn_kernel, lam_init=lam_init, seq_len=seq_len)
    seq = lambda off: pl.BlockSpec((seq_len, C_V_DIM), lambda b, h: (b, off + h))
    return pl.pallas_call(
        kernel,
        out_shape=jax.ShapeDtypeStruct((n, D), BF16),
        grid=(batch, C_HEADS),
        in_specs=[
            seq(0), seq(C_HEADS), seq(2 * C_HEADS),
            pl.BlockSpec((None, 2, tq, tq), lambda b, h: (h, 0, 0, 0)),
            pl.BlockSpec((None, 4, C_HEAD_DIM), lambda b, h: (layer, 0, 0)),
            pl.BlockSpec((None, 1, C_V_DIM), lambda b, h: (layer, 0, 0)),
        ],
        out_specs=seq(0),
        compiler_params=_params(2, 40 << 20),
        name="diff_attention",
    )(qkv, qkv, qkv, bias_tiles, lam_params, subln_g.reshape(-1, 1, C_V_DIM))


def _xattn_kernel(q_ref, kv_ref, o_ref):
    scale = X_HEAD_DIM ** -0.5
    for h in range(X_HEADS):
        c0 = h * X_HEAD_DIM
        q = q_ref[:, c0:c0 + X_HEAD_DIM]
        k = kv_ref[:, c0:c0 + X_HEAD_DIM]
        v = kv_ref[:, D + c0:D + c0 + X_HEAD_DIM]
        s = lax.dot_general(q, k, (((1,), (1,)), ((), ())), preferred_element_type=F32) * scale
        m = jnp.max(s, axis=1, keepdims=True)
        p = jnp.exp(s - m)
        l = jnp.sum(p, axis=1, keepdims=True)
        o = jnp.dot(p.astype(BF16), v, preferred_element_type=F32) / l
        o_ref[:, c0:c0 + X_HEAD_DIM] = o.astype(BF16)


def cross_attention(q, kv, batch, seq_len, mem_len, *, tm=512):
    n = q.shape[0]
    blocks_per_seq = seq_len // tm
    return pl.pallas_call(
        _xattn_kernel,
        out_shape=jax.ShapeDtypeStruct((n, D), BF16),
        grid=(n // tm,),
        in_specs=[
            pl.BlockSpec((tm, D), lambda i: (i, 0)),
            pl.BlockSpec((mem_len, 2 * D), lambda i: (i // blocks_per_seq, 0)),
        ],
        out_specs=pl.BlockSpec((tm, D), lambda i: (i, 0)),
        compiler_params=_params(1, 32 << 20),
        name="cross_attention",
    )(q, kv)


ROUTE_T = 256


def _router_kernel(h_ref, wrt_ref, e_ref, w_ref, rank_ref, cnt_ref, base_ref):
    i = pl.program_id(0)

    @pl.when(i == 0)
    def _():
        base_ref[...] = jnp.zeros(base_ref.shape, F32)

    t = ROUTE_T
    logits = lax.dot_general(wrt_ref[...], h_ref[...], (((1,), (1,)), ((), ())),
                             precision=lax.Precision.HIGHEST, preferred_element_type=F32)
    eidx = lax.broadcasted_iota(jnp.int32, (N_EXPERTS, t), 0)
    m1 = jnp.max(logits, axis=0, keepdims=True)
    e1 = jnp.min(jnp.where(logits == m1, eidx, N_EXPERTS), axis=0, keepdims=True)
    oh1 = eidx == e1
    rest = jnp.where(oh1, -jnp.inf, logits)
    m2 = jnp.max(rest, axis=0, keepdims=True)
    e2 = jnp.min(jnp.where(rest == m2, eidx, N_EXPERTS), axis=0, keepdims=True)
    oh2 = eidx == e2
    ex = jnp.exp(m2 - m1)
    den = 1.0 + ex
    e_ref[0:1, :] = e1
    e_ref[1:2, :] = e2
    w_ref[0:1, :] = 1.0 / den
    w_ref[1:2, :] = ex / den

    oh = (oh1.astype(F32) + oh2.astype(F32)).astype(BF16)
    r = lax.broadcasted_iota(jnp.int32, (t, t), 0)
    c = lax.broadcasted_iota(jnp.int32, (t, t), 1)
    before = (r < c).astype(BF16)
    ones = jnp.ones((t, t), BF16)
    pos = base_ref[...] + jnp.dot(oh, before, preferred_element_type=F32)
    rank_ref[0:1, :] = jnp.sum(jnp.where(oh1, pos, 0.0), axis=0, keepdims=True).astype(jnp.int32)
    rank_ref[1:2, :] = jnp.sum(jnp.where(oh2, pos, 0.0), axis=0, keepdims=True).astype(jnp.int32)
    total = base_ref[...] + jnp.dot(oh, ones, preferred_element_type=F32)
    base_ref[...] = total
    cnt_ref[...] = total[:, 0:128].astype(jnp.int32)


def moe_router(h, w_router, layer):
    n = h.shape[0]
    t = ROUTE_T
    wrt = jnp.swapaxes(w_router, 1, 2)
    pair = pl.BlockSpec((TOP_K, t), lambda i: (0, i))
    return pl.pallas_call(
        _router_kernel,
        out_shape=(
            jax.ShapeDtypeStruct((TOP_K, n), jnp.int32),
            jax.ShapeDtypeStruct((TOP_K, n), F32),
            jax.ShapeDtypeStruct((TOP_K, n), jnp.int32),
            jax.ShapeDtypeStruct((N_EXPERTS, 128), jnp.int32),
        ),
        grid=(n // t,),
        in_specs=[
            pl.BlockSpec((t, D), lambda i: (i, 0)),
            pl.BlockSpec((None, N_EXPERTS, D), lambda i: (layer, 0, 0)),
        ],
        out_specs=(pair, pair, pair, pl.BlockSpec((N_EXPERTS, 128), lambda i: (0, 0))),
        scratch_shapes=[pltpu.VMEM((N_EXPERTS, t), F32)],
        compiler_params=_params(1, 16 << 20),
        name="moe_router",
    )(h, wrt)


def _row_copy(src_ref, src_row, dst_ref, dst_row, sem):
    return pltpu.make_async_copy(src_ref.at[pl.ds(src_row, 1)], dst_ref.at[pl.ds(dst_row, 1)], sem)


DISPATCH_T = 256
ISSUE_UNROLL = 8


def _dispatch_kernel(dest_ref, h_ref, xb_in_hbm, xb_hbm, sem, *, n_tokens):
    del xb_in_hbm
    i = pl.program_id(0)
    t = DISPATCH_T

    def start(r, carry):
        tok = i * t + r
        _row_copy(h_ref, r, xb_hbm, dest_ref[tok], sem).start(priority=0)
        _row_copy(h_ref, r, xb_hbm, dest_ref[n_tokens + tok], sem).start(priority=1)
        return carry

    lax.fori_loop(0, t, start, 0, unroll=ISSUE_UNROLL)
    for _ in range(TOP_K):
        pltpu.make_async_copy(h_ref, xb_hbm.at[pl.ds(0, t)], sem).wait()


def moe_dispatch(h, dest, n_blocks):
    n, d = h.shape
    t = DISPATCH_T
    rows = n_blocks * EXPERT_BLOCK
    kernel = functools.partial(_dispatch_kernel, n_tokens=n)
    return pl.pallas_call(
        kernel,
        out_shape=jax.ShapeDtypeStruct((rows, d), F32),
        grid_spec=pltpu.PrefetchScalarGridSpec(
            num_scalar_prefetch=1,
            grid=(n // t,),
            in_specs=[pl.BlockSpec((t, d), lambda i, ds: (i, 0)), pl.BlockSpec(memory_space=pl.ANY)],
            out_specs=pl.BlockSpec(memory_space=pl.ANY),
            scratch_shapes=[pltpu.SemaphoreType.DMA(())],
        ),
        input_output_aliases={2: 0},
        compiler_params=_params(1, 16 << 20),
        name="moe_dispatch",
    )(dest, h, jnp.zeros((rows, d), F32))


def _expert_changed(be_ref, i):
    return jnp.logical_or(i == 0, be_ref[i] != be_ref[jnp.maximum(i - 1, 0)])


def _moe_up_kernel(be_ref, nused_ref, first_ref, nxte_ref, last_ref, x_ref, wg_hbm, wu_hbm, o_ref,
                   wgb_ref, wub_ref, sg_ref, su_ref, sems, *, layer, tn, n_col_tiles):
    j = pl.program_id(0)
    i = pl.program_id(1)

    def fetch(e, jj):
        cols = pl.ds(pl.multiple_of(jj * tn, 128), tn)
        return (pltpu.make_async_copy(wg_hbm.at[layer, e, :, cols], sg_ref, sems.at[0]),
                pltpu.make_async_copy(wu_hbm.at[layer, e, :, cols], su_ref, sems.at[1]))

    @pl.when(first_ref[i] == 1)
    def _():
        @pl.when(jnp.logical_and(j == 0, i == 0))
        def _():
            for c in fetch(be_ref[0], 0):
                c.start()

        for c in fetch(be_ref[i], j):
            c.wait()
        _cast_rows(sg_ref, wgb_ref)
        _cast_rows(su_ref, wub_ref)
        nj = j + last_ref[i]

        @pl.when(nj < n_col_tiles)
        def _():
            for c in fetch(nxte_ref[i], nj):
                c.start()

    @pl.when(i < nused_ref[0])
    def _():
        x = x_ref[...].astype(BF16)
        g = jnp.dot(x, wgb_ref[...], preferred_element_type=F32)
        u = jnp.dot(x, wub_ref[...], preferred_element_type=F32)
        o_ref[...] = (jax.nn.silu(g) * u).astype(o_ref.dtype)

    @pl.when(i >= nused_ref[0])
    def _():
        o_ref[...] = jnp.zeros(o_ref.shape, o_ref.dtype)


def _group_schedule(block_e):
    nb = block_e.shape[0]
    idx = jnp.arange(nb, dtype=jnp.int32)
    first = jnp.concatenate([jnp.ones((1,), bool), block_e[1:] != block_e[:-1]])
    later_first = jnp.logical_and(first[None, :], idx[None, :] > idx[:, None])
    nxt_idx = jnp.min(jnp.where(later_first, idx[None, :], nb), axis=1)
    last = nxt_idx >= nb
    nxt_idx = jnp.where(last, 0, nxt_idx)
    nxt_e = jnp.sum(jnp.where(idx[None, :] == nxt_idx[:, None], block_e[None, :], 0), axis=1)
    return first.astype(jnp.int32), nxt_e.astype(jnp.int32), last.astype(jnp.int32)


def moe_up(xb, w_gate, w_up, layer, block_e, nused, *, tn=1408):
    p, k = xb.shape
    rows = EXPERT_BLOCK
    f = w_gate.shape[-1]
    assert f % tn == 0 and tn % 128 == 0
    first, nxt_e, last = _group_schedule(block_e)
    kernel = functools.partial(_moe_up_kernel, layer=layer, tn=tn, n_col_tiles=f // tn)
    hbm = pl.BlockSpec(memory_space=pl.ANY)
    return pl.pallas_call(
        kernel,
        out_shape=jax.ShapeDtypeStruct((p, f), BF16),
        grid_spec=pltpu.PrefetchScalarGridSpec(
            num_scalar_prefetch=5,
            grid=(f // tn, p // rows),
            in_specs=[pl.BlockSpec((rows, k), lambda j, i, *_: (i, 0)), hbm, hbm],
            out_specs=pl.BlockSpec((rows, tn), lambda j, i, *_: (i, j)),
            scratch_shapes=[pltpu.VMEM((k, tn), BF16), pltpu.VMEM((k, tn), BF16),
                            pltpu.VMEM((k, tn), F32), pltpu.VMEM((k, tn), F32),
                            pltpu.SemaphoreType.DMA((2,))],
        ),
        compiler_params=_params(2, 52 << 20),
        name="moe_up",
    )(block_e, nused, first, nxt_e, last, xb, w_gate, w_up)


def _moe_down_kernel(be_ref, nused_ref, x_ref, w_ref, o_ref, wb_ref):
    i = pl.program_id(1)

    @pl.when(_expert_changed(be_ref, i))
    def _():
        _cast_rows(w_ref, wb_ref)

    @pl.when(i < nused_ref[0])
    def _():
        o_ref[...] = jnp.dot(x_ref[...], wb_ref[...], preferred_element_type=F32)

    @pl.when(i >= nused_ref[0])
    def _():
        o_ref[...] = jnp.zeros(o_ref.shape, o_ref.dtype)


def moe_down(hb, w_down, layer, block_e, nused, *, tn=1024):
    p, k = hb.shape
    rows = EXPERT_BLOCK
    n = w_down.shape[-1]
    return pl.pallas_call(
        _moe_down_kernel,
        out_shape=jax.ShapeDtypeStruct((p, n), F32),
        grid_spec=pltpu.PrefetchScalarGridSpec(
            num_scalar_prefetch=2,
            grid=(n // tn, p // rows),
            in_specs=[
                pl.BlockSpec((rows, k), lambda j, i, be, nu: (i, 0)),
                pl.BlockSpec((None, None, k, tn), lambda j, i, be, nu: (layer, be[i], 0, j)),
            ],
            out_specs=pl.BlockSpec((rows, tn), lambda j, i, be, nu: (i, j)),
            scratch_shapes=[pltpu.VMEM((k, tn), BF16)],
        ),
        compiler_params=_params(2, 44 << 20),
        name="moe_down",
    )(block_e, nused, hb, w_down)


COMBINE_T = 256


def _combine_kernel(dest_ref, y_hbm, wt_ref, h_ref, g_ref, b_ref, of_ref, ob_ref,
                    buf_ref, sems, *, n_tokens):
    i = pl.program_id(0)
    t = COMBINE_T

    def issue(block, slot):
        def body(r, carry):
            tok = block * t + r
            _row_copy(y_hbm, dest_ref[tok], buf_ref.at[slot, 0], r, sems.at[slot]).start(priority=0)
            _row_copy(y_hbm, dest_ref[n_tokens + tok], buf_ref.at[slot, 1], r, sems.at[slot]).start(priority=1)
            return carry

        lax.fori_loop(0, t, body, 0, unroll=ISSUE_UNROLL)

    slot = i % 2

    @pl.when(i == 0)
    def _():
        issue(0, 0)

    @pl.when(i + 1 < pl.num_programs(0))
    def _():
        issue(i + 1, 1 - slot)

    for k in range(TOP_K):
        pltpu.make_async_copy(y_hbm.at[pl.ds(0, t)], buf_ref.at[slot, k], sems.at[slot]).wait()
    ff = buf_ref[slot, 0] * wt_ref[:, 0:1] + buf_ref[slot, 1] * wt_ref[:, 1:2]
    y = _layer_norm(DN_ALPHA * h_ref[...] + ff, g_ref[...], b_ref[...])
    of_ref[...] = y
    ob_ref[...] = y.astype(BF16)


def moe_combine_ln(yb, dest, wts_t, h, ln_g, ln_b, layer, which):
    n, d = h.shape
    t = COMBINE_T
    row = pl.BlockSpec((t, d), lambda i, ds: (i, 0))
    par = pl.BlockSpec((None, None, 1, d), lambda i, ds: (layer, which, 0, 0))
    kernel = functools.partial(_combine_kernel, n_tokens=n)
    return pl.pallas_call(
        kernel,
        out_shape=(jax.ShapeDtypeStruct((n, d), F32), jax.ShapeDtypeStruct((n, d), BF16)),
        grid_spec=pltpu.PrefetchScalarGridSpec(
            num_scalar_prefetch=1,
            grid=(n // t,),
            in_specs=[
                pl.BlockSpec(memory_space=pl.ANY),
                pl.BlockSpec((t, TOP_K), lambda i, ds: (i, 0)),
                row, par, par,
            ],
            out_specs=(row, row),
            scratch_shapes=[pltpu.VMEM((2, TOP_K, t, d), F32), pltpu.SemaphoreType.DMA((2,))],
        ),
        compiler_params=_params(1, 32 << 20),
        name="moe_combine_ln",
    )(dest, yb, wts_t, h, ln_g, ln_b)


def moe_layer(h, hb, w_router, w_gate, w_up, w_down, ln_g, ln_b, layer, moe_idx):
    n = h.shape[0]
    m = n * TOP_K
    n_blocks = (m + N_EXPERTS * (EXPERT_BLOCK - 1) + EXPERT_BLOCK - 1) // EXPERT_BLOCK
    e, wts, rank, cnt = moe_router(h, w_router, moe_idx)
    counts = cnt[:, 0]
    padded = (counts + EXPERT_BLOCK - 1) // EXPERT_BLOCK * EXPERT_BLOCK
    pad_ends = jnp.cumsum(padded)
    pad_starts = pad_ends - padded
    expert_ids = jnp.arange(N_EXPERTS, dtype=jnp.int32)[:, None, None]
    dest = jnp.sum(jnp.where(e[None] == expert_ids, pad_starts[:, None, None], 0), axis=0) + rank
    block_start = jnp.arange(n_blocks, dtype=jnp.int32) * EXPERT_BLOCK
    block_e = jnp.minimum(jnp.sum(block_start[:, None] >= pad_ends[None, :], axis=1), N_EXPERTS - 1).astype(jnp.int32)
    nused = (pad_ends[-1:] // EXPERT_BLOCK).astype(jnp.int32)

    dest = dest.reshape(-1)
    xb = moe_dispatch(h, dest, n_blocks)
    hid = moe_up(xb, w_gate, w_up, moe_idx, block_e, nused)
    yb = moe_down(hid, w_down, moe_idx, block_e, nused)
    return moe_combine_ln(yb, dest, wts.T, h, ln_g, ln_b, layer, 2)


def kernel(x, mem, rel_table, mix_w_in, gmlp_ln_g, gmlp_ln_b, gmlp_w_s, gmlp_b_s, pool_w, pool_scale, mix_w_out, diff_w_qkv, diff_lam_q1, diff_lam_k1, diff_lam_q2, diff_lam_k2, diff_subln_g, diff_w_o, xa_w_q, xa_w_kv, xa_w_o, ffn_w_gate, ffn_w_up, ffn_w_down, moe_w_router, moe_w_gate, moe_w_up, moe_w_down, ln_g, ln_b):
    batch, seq_len, d = x.shape
    mem_len = mem.shape[1]
    n = batch * seq_len
    h = x.reshape(n, d)
    hb = h
    memf = mem.reshape(batch * mem_len, d)
    ln_g4 = ln_g.reshape(DEPTH, 3, 1, d)
    ln_b4 = ln_b.reshape(DEPTH, 3, 1, d)
    lam_params = jnp.stack([diff_lam_q1, diff_lam_k1, diff_lam_q2, diff_lam_k2], axis=1)
    bias_tiles = attention_bias_tiles(rel_table)

    for layer in range(DEPTH):
        i = layer // 2
        if layer % 2 == 0:
            hm = matmul(hb, mix_w_in, i, tm=512, tn=1024, out_dtype=F32)
            mixed = gmlp_pool_mixer(hm, seq_len, gmlp_ln_g, gmlp_ln_b, gmlp_w_s, gmlp_b_s, pool_w,
                                    pool_scale, i)
            h, hb = matmul_residual_ln(mixed, mix_w_out, i, h, ln_g4, ln_b4, layer, 0)
        else:
            lam_init = 0.8 - 0.6 * math.exp(-0.3 * layer)
            qkv = matmul(hb, diff_w_qkv, i, tm=1024, tn=1024, out_dtype=BF16)
            att = diff_attention(qkv, bias_tiles, lam_params, diff_subln_g, i, lam_init, batch, seq_len)
            h, hb = matmul_residual_ln(att, diff_w_o, i, h, ln_g4, ln_b4, layer, 0)

        q = matmul(hb, xa_w_q, layer, tm=512, tn=1024, out_dtype=BF16)
        kv = matmul(memf, xa_w_kv, layer, tm=512, tn=1024, out_dtype=BF16)
        xo = cross_attention(q, kv, batch, seq_len, mem_len)
        h, hb = matmul_residual_ln(xo, xa_w_o, layer, h, ln_g4, ln_b4, layer, 1)

        if layer % 2 == 0:
            hid = swiglu_up(hb, ffn_w_gate, ffn_w_up, i, tm=1024, tn=512)
            ff = matmul(hid, ffn_w_down, i, tm=512, tn=512, out_dtype=F32)
            h, hb = residual_ln(h, ff, ln_g4, ln_b4, layer, 2)
        else:
            h, hb = moe_layer(h, hb, moe_w_router, moe_w_gate, moe_w_up, moe_w_down, ln_g4, ln_b4,
                              layer, i)
    return h.reshape(batch, seq_len, d)
```

```python
import functools
import math

import numpy as np
import jax
import jax.numpy as jnp
from jax import lax
from jax.experimental import pallas as pl
from jax.experimental.pallas import tpu as pltpu

F32 = jnp.float32
BF16 = jnp.bfloat16

D = 2048
DEPTH = 4
CHUNK = 64
GMLP_CHUNK = 128
A_WIDTH = D // 2
A_HEADS = 8
A_HEAD_DIM = A_WIDTH // A_HEADS
B_WIDTH = D // 2
POOL_WINDOWS = (2, 4, 8, 16)
B_GROUP = B_WIDTH // len(POOL_WINDOWS)
MIX_IN = 2 * A_WIDTH + B_WIDTH
C_HEADS = 16
C_HEAD_DIM = D // (2 * C_HEADS)
C_V_DIM = 2 * C_HEAD_DIM
REL_BUCKETS = 32
REL_MAX_DIST = 128
X_HEADS = 4
X_HEAD_DIM = D // X_HEADS
N_EXPERTS = 8
TOP_K = 2
EXPERT_BLOCK = 256
DN_ALPHA = (2 * DEPTH) ** 0.25
LN_EPS = 1e-5
NEG = -1e30

VMEM_CAP_BYTES = 56 * 1024 * 1024
CAST_ROWS = 256
ATT_TQ = 256
POOL_HALO = 16


def _params(n_axes, vmem_bytes):
    return pltpu.CompilerParams(
        dimension_semantics=("arbitrary",) * n_axes,
        vmem_limit_bytes=int(min(max(vmem_bytes, 16 * 1024 * 1024), VMEM_CAP_BYTES)),
    )


def _cast_rows(w_ref, wb_ref):
    k = w_ref.shape[0]
    rows = CAST_ROWS if k % CAST_ROWS == 0 else k

    def body(c, carry):
        r = pl.multiple_of(c * rows, rows)
        wb_ref[pl.ds(r, rows), :] = w_ref[pl.ds(r, rows), :].astype(BF16)
        return carry

    lax.fori_loop(0, k // rows, body, 0)


def _layer_norm(x, g, b):
    mu = jnp.mean(x, axis=-1, keepdims=True)
    xc = x - mu
    var = jnp.mean(xc * xc, axis=-1, keepdims=True)
    return xc * lax.rsqrt(var + LN_EPS) * g + b


def _mm_kernel(a_ref, w_ref, o_ref, wb_ref):
    @pl.when(pl.program_id(1) == 0)
    def _():
        _cast_rows(w_ref, wb_ref)

    a = a_ref[...].astype(BF16)
    o_ref[...] = jnp.dot(a, wb_ref[...], preferred_element_type=F32).astype(o_ref.dtype)


def matmul(a, w, layer, *, tm, tn, out_dtype):
    m, k = a.shape
    n = w.shape[-1]
    assert m % tm == 0 and n % tn == 0 and w.shape[-2] == k
    vmem = 2 * tm * k * a.dtype.itemsize + 2 * k * tn * 4 + k * tn * 2 + 2 * tm * tn * 4 + tm * tn * 4
    return pl.pallas_call(
        _mm_kernel,
        out_shape=jax.ShapeDtypeStruct((m, n), out_dtype),
        grid=(n // tn, m // tm),
        in_specs=[
            pl.BlockSpec((tm, k), lambda j, i: (i, 0)),
            pl.BlockSpec((None, k, tn), lambda j, i: (layer, 0, j)),
        ],
        out_specs=pl.BlockSpec((tm, tn), lambda j, i: (i, j)),
        scratch_shapes=[pltpu.VMEM((k, tn), BF16)],
        compiler_params=_params(2, vmem + (4 << 20)),
        name="matmul",
    )(a, w)


MMLN_GROUPS = 2


def _mm_ln_kernel(a_ref, w_ref, h_ref, g_ref, b_ref, of_ref, ob_ref, wb_ref):
    @pl.when(pl.program_id(0) == 0)
    def _():
        _cast_rows(w_ref, wb_ref)

    rows = a_ref.shape[0] // MMLN_GROUPS
    for r in range(MMLN_GROUPS):
        sl = slice(r * rows, (r + 1) * rows)
        y = jnp.dot(a_ref[sl, :], wb_ref[...], preferred_element_type=F32)
        out = _layer_norm(DN_ALPHA * h_ref[sl, :] + y, g_ref[...], b_ref[...])
        of_ref[sl, :] = out
        ob_ref[sl, :] = out.astype(BF16)


def matmul_residual_ln(a, w, w_layer, h, ln_g, ln_b, layer, which, *, tm=512):
    m, k = a.shape
    d = w.shape[-1]
    assert m % tm == 0 and h.shape == (m, d)
    row = pl.BlockSpec((tm, d), lambda i: (i, 0))
    par = pl.BlockSpec((None, None, 1, d), lambda i: (layer, which, 0, 0))
    vmem = k * d * 4 + k * d * 2 + 2 * tm * k * 2 + 4 * tm * d * 4 + 2 * tm * d * 2 + 3 * tm * d * 4
    return pl.pallas_call(
        _mm_ln_kernel,
        out_shape=(jax.ShapeDtypeStruct((m, d), F32), jax.ShapeDtypeStruct((m, d), BF16)),
        grid=(m // tm,),
        in_specs=[
            pl.BlockSpec((tm, k), lambda i: (i, 0)),
            pl.BlockSpec((None, k, d), lambda i: (w_layer, 0, 0), pipeline_mode=pl.Buffered(1)),
            row, par, par,
        ],
        out_specs=(row, row),
        scratch_shapes=[pltpu.VMEM((k, d), BF16)],
        compiler_params=_params(1, vmem + (4 << 20)),
        name="matmul_residual_ln",
    )(a, w, h, ln_g, ln_b)


def _swiglu_kernel(a_ref, wg_ref, wu_ref, o_ref, wgb_ref, wub_ref):
    @pl.when(pl.program_id(1) == 0)
    def _():
        _cast_rows(wg_ref, wgb_ref)
        _cast_rows(wu_ref, wub_ref)

    a = a_ref[...]
    g = jnp.dot(a, wgb_ref[...], preferred_element_type=F32)
    u = jnp.dot(a, wub_ref[...], preferred_element_type=F32)
    o_ref[...] = (jax.nn.silu(g) * u).astype(o_ref.dtype)


def swiglu_up(a, wg, wu, layer, *, tm, tn):
    m, k = a.shape
    n = wg.shape[-1]
    assert m % tm == 0 and n % tn == 0
    vmem = 2 * tm * k * 2 + 4 * k * tn * 4 + 2 * k * tn * 2 + 2 * tm * tn * 2 + 3 * tm * tn * 4
    wspec = pl.BlockSpec((None, k, tn), lambda j, i: (layer, 0, j))
    return pl.pallas_call(
        _swiglu_kernel,
        out_shape=jax.ShapeDtypeStruct((m, n), BF16),
        grid=(n // tn, m // tm),
        in_specs=[pl.BlockSpec((tm, k), lambda j, i: (i, 0)), wspec, wspec],
        out_specs=pl.BlockSpec((tm, tn), lambda j, i: (i, j)),
        scratch_shapes=[pltpu.VMEM((k, tn), BF16), pltpu.VMEM((k, tn), BF16)],
        compiler_params=_params(2, vmem + (4 << 20)),
        name="swiglu_up",
    )(a, wg, wu)


def _ln_kernel(h_ref, y_ref, g_ref, b_ref, of_ref, ob_ref):
    x = DN_ALPHA * h_ref[...] + y_ref[...]
    y = _layer_norm(x, g_ref[...], b_ref[...])
    of_ref[...] = y
    ob_ref[...] = y.astype(BF16)


def residual_ln(h, y, ln_g, ln_b, layer, which, *, tm=256):
    m, d = h.shape
    row = pl.BlockSpec((tm, d), lambda i: (i, 0))
    par = pl.BlockSpec((None, None, 1, d), lambda i: (layer, which, 0, 0))
    return pl.pallas_call(
        _ln_kernel,
        out_shape=(jax.ShapeDtypeStruct((m, d), F32), jax.ShapeDtypeStruct((m, d), BF16)),
        grid=(m // tm,),
        in_specs=[row, row, par, par],
        out_specs=(row, row),
        compiler_params=_params(1, 16 * tm * d * 4),
        name="residual_ln",
    )(h, y, ln_g, ln_b)


MIX_ROWS = 2 * GMLP_CHUNK


def _mixer_kernel(hm_ref, prev_ref, lng_ref, lnb_ref, ws_ref, bst_ref, pw_ref, ps_ref, o_ref,
                  wbd_ref, ext_ref, *, blocks_per_seq):
    i = pl.program_id(0)

    @pl.when(i == 0)
    def _():
        p = lax.broadcasted_iota(jnp.int32, (GMLP_CHUNK, GMLP_CHUNK), 0)
        q = lax.broadcasted_iota(jnp.int32, (GMLP_CHUNK, GMLP_CHUNK), 1)
        allowed = (q // CHUNK) <= (p // CHUNK)
        wbd_ref[...] = jnp.zeros(wbd_ref.shape, BF16)
        for h in range(A_HEADS):
            w = jnp.where(allowed, ws_ref[h], 0.0).astype(BF16)
            wbd_ref[h, 0:GMLP_CHUNK, 0:GMLP_CHUNK] = w
            wbd_ref[h, GMLP_CHUNK:MIX_ROWS, GMLP_CHUNK:MIX_ROWS] = w

    z = jax.nn.gelu(hm_ref[:, 0:2 * A_WIDTH])
    u = z[:, 0:A_WIDTH]
    v = _layer_norm(z[:, A_WIDTH:2 * A_WIDTH], lng_ref[...], lnb_ref[...]).astype(BF16)
    for h in range(A_HEADS):
        c0 = h * A_HEAD_DIM
        sv = jnp.dot(wbd_ref[h], v[:, c0:c0 + A_HEAD_DIM], preferred_element_type=F32)
        sv = sv + bst_ref[:, h:h + 1]
        o_ref[:, c0:c0 + A_HEAD_DIM] = (u[:, c0:c0 + A_HEAD_DIM] * sv).astype(BF16)

    blk = i % blocks_per_seq
    prev = jnp.where(blk == 0, 0.0, prev_ref[...])
    ext_ref[0:POOL_HALO, :] = prev
    ext_ref[POOL_HALO:POOL_HALO + MIX_ROWS, :] = hm_ref[:, 2 * A_WIDTH:MIX_IN]
    t1 = blk * MIX_ROWS + lax.broadcasted_iota(jnp.int32, (MIX_ROWS, 1), 0) + 1
    for g, w in enumerate(POOL_WINDOWS):
        c0 = g * B_GROUP
        pin = ext_ref[POOL_HALO:POOL_HALO + MIX_ROWS, c0:c0 + B_GROUP]
        s = pin
        for k in range(1, w):
            s = s + ext_ref[POOL_HALO - k:POOL_HALO - k + MIX_ROWS, c0:c0 + B_GROUP]
        cnt = jnp.minimum(t1, w).astype(F32)
        pooled = s / cnt - pin
        b = jnp.dot(pooled.astype(BF16), pw_ref[g].astype(BF16), preferred_element_type=F32)
        b = b * ps_ref[:, c0:c0 + B_GROUP]
        o_ref[:, A_WIDTH + c0:A_WIDTH + c0 + B_GROUP] = b.astype(BF16)


def gmlp_pool_mixer(hm, seq_len, ln_g, ln_b, w_s, b_s, pool_w, pool_scale, layer):
    n = hm.shape[0]
    assert seq_len % MIX_ROWS == 0 and max(POOL_WINDOWS) <= POOL_HALO
    blocks_per_seq = seq_len // MIX_ROWS
    halo_per_block = MIX_ROWS // POOL_HALO
    bst = jnp.tile(b_s[layer].T, (MIX_ROWS // GMLP_CHUNK, 1))
    kernel = functools.partial(_mixer_kernel, blocks_per_seq=blocks_per_seq)
    return pl.pallas_call(
        kernel,
        out_shape=jax.ShapeDtypeStruct((n, D), BF16),
        grid=(n // MIX_ROWS,),
        in_specs=[
            pl.BlockSpec((MIX_ROWS, MIX_IN), lambda i: (i, 0)),
            pl.BlockSpec((POOL_HALO, B_WIDTH),
                         lambda i: (jnp.maximum(i * halo_per_block - 1, 0), 2 * A_WIDTH // B_WIDTH)),
            pl.BlockSpec((None, 1, A_WIDTH), lambda i: (layer, 0, 0)),
            pl.BlockSpec((None, 1, A_WIDTH), lambda i: (layer, 0, 0)),
            pl.BlockSpec((None, A_HEADS, GMLP_CHUNK, GMLP_CHUNK), lambda i: (layer, 0, 0, 0)),
            pl.BlockSpec((MIX_ROWS, A_HEADS), lambda i: (0, 0)),
            pl.BlockSpec((None, len(POOL_WINDOWS), B_GROUP, B_GROUP), lambda i: (layer, 0, 0, 0)),
            pl.BlockSpec((None, 1, B_WIDTH), lambda i: (layer, 0, 0)),
        ],
        out_specs=pl.BlockSpec((MIX_ROWS, D), lambda i: (i, 0)),
        scratch_shapes=[
            pltpu.VMEM((A_HEADS, MIX_ROWS, MIX_ROWS), BF16),
            pltpu.VMEM((POOL_HALO + MIX_ROWS, B_WIDTH), F32),
        ],
        compiler_params=_params(1, 40 << 20),
        name="gmlp_pool_mixer",
    )(hm, hm, ln_g.reshape(-1, 1, A_WIDTH), ln_b.reshape(-1, 1, A_WIDTH), w_s, bst, pool_w,
      pool_scale.reshape(-1, 1, B_WIDTH))


def _t5_bucket_np(rel):
    half = REL_BUCKETS // 2
    max_exact = half // 2
    rel = np.asarray(rel, dtype=np.int64)
    n = np.abs(rel)
    steps = half - max_exact
    ratio = REL_MAX_DIST // max_exact
    large = np.zeros_like(n)
    nn = n.astype(object) ** steps
    for j in range(1, steps + 1):
        large = large + (nn >= (max_exact ** steps) * (ratio ** j)).astype(np.int64)
    large = np.minimum(max_exact + large, half - 1)
    return np.where(rel > 0, half, 0) + np.where(n < max_exact, n, large)


def _bias_bucket_tiles():
    qi = np.arange(ATT_TQ)[:, None]
    kj = np.arange(ATT_TQ)[None, :]
    diag = _t5_bucket_np(kj - qi)
    diag = np.where((kj // CHUNK) <= (qi // CHUNK), diag, -1)
    prev = _t5_bucket_np(kj - ATT_TQ - qi)
    return np.stack([diag, prev], 0).astype(np.int32)


FAR_BUCKET = REL_BUCKETS // 2 - 1


def _bias_kernel(rel_ref, bk_ref, o_ref):
    h = pl.program_id(0)
    bk = bk_ref[...]
    acc = jnp.zeros(bk.shape, F32)
    for b in range(REL_BUCKETS):
        acc = jnp.where(bk == b, rel_ref[b, h], acc)
    o_ref[...] = jnp.where(bk < 0, NEG, acc - rel_ref[FAR_BUCKET, h])


def attention_bias_tiles(rel_table):
    assert ATT_TQ >= REL_MAX_DIST and ATT_TQ % CHUNK == 0
    bk = jnp.asarray(_bias_bucket_tiles())
    return pl.pallas_call(
        _bias_kernel,
        out_shape=jax.ShapeDtypeStruct((C_HEADS, 2, ATT_TQ, ATT_TQ), F32),
        grid=(C_HEADS,),
        in_specs=[
            pl.BlockSpec(memory_space=pltpu.SMEM),
            pl.BlockSpec((2, ATT_TQ, ATT_TQ), lambda h: (0, 0, 0)),
        ],
        out_specs=pl.BlockSpec((None, 2, ATT_TQ, ATT_TQ), lambda h: (h, 0, 0, 0)),
        compiler_params=_params(1, 16 << 20),
        name="attention_bias_tiles",
    )(rel_table, bk)


def _diff_attn_kernel(q_ref, k_ref, v_ref, bias_ref, lam_ref, g_ref, o_ref, vx_ref, *, lam_init, seq_len):
    tq = ATT_TQ
    nt = (((1,), (1,)), ((), ()))
    lp = lam_ref[...]
    lam = (jnp.exp(jnp.sum(lp[0:1] * lp[1:2], axis=1, keepdims=True))
           - jnp.exp(jnp.sum(lp[2:3] * lp[3:4], axis=1, keepdims=True)) + lam_init)
    lane = lax.broadcasted_iota(jnp.int32, (tq, C_V_DIM), 1)
    vx_ref[:, 0:C_V_DIM] = v_ref[...]
    vx_ref[:, C_V_DIM:2 * C_V_DIM] = jnp.ones((seq_len, C_V_DIM), BF16)
    for qi in range(seq_len // tq):
        q = q_ref[qi * tq:(qi + 1) * tq, :] * (C_HEAD_DIM ** -0.5)
        zero = jnp.zeros_like(q)
        parts = []
        if qi >= 2:
            parts.append((0, (qi - 1) * tq, None))
        if qi >= 1:
            parts.append(((qi - 1) * tq, qi * tq, 1))
        parts.append((qi * tq, (qi + 1) * tq, 0))
        outs = []
        for half in range(2):
            keep = (lane < C_HEAD_DIM) if half == 0 else (lane >= C_HEAD_DIM)
            qh = jnp.where(keep, q, zero)
            scores = []
            m = None
            for a, b, bi in parts:
                s = lax.dot_general(qh, k_ref[a:b, :], nt, preferred_element_type=F32)
                if bi is not None:
                    s = s + bias_ref[bi]
                scores.append(s)
                sm = jnp.max(s, axis=1, keepdims=True)
                m = sm if m is None else jnp.maximum(m, sm)
            acc = None
            for s, (a, b, _) in zip(scores, parts):
                p = jnp.exp(s - m).astype(BF16)
                pv = jnp.dot(p, vx_ref[a:b, :], preferred_element_type=F32)
                acc = pv if acc is None else acc + pv
            outs.append(acc[:, 0:C_V_DIM] / acc[:, C_V_DIM:C_V_DIM + 1])
        o = outs[0] - lam * outs[1]
        y = o * lax.rsqrt(jnp.mean(o * o, axis=-1, keepdims=True) + LN_EPS) * g_ref[...]
        o_ref[qi * tq:(qi + 1) * tq, :] = (y * (1.0 - lam_init)).astype(BF16)


def diff_attention(qkv, bias_tiles, lam_params, subln_g, layer, lam_init, batch, seq_len):
    n = qkv.shape[0]
    tq = ATT_TQ
    assert seq_len % tq == 0
    kernel = functools.partial(_diff_attn_kernel, lam_init=lam_init, seq_len=seq_len)
    seq = lambda off: pl.BlockSpec((seq_len, C_V_DIM), lambda b, h: (b, off + h))
    return pl.pallas_call(
        kernel,
        out_shape=jax.ShapeDtypeStruct((n, D), BF16),
        grid=(batch, C_HEADS),
        in_specs=[
            seq(0), seq(C_HEADS), seq(2 * C_HEADS),
            pl.BlockSpec((None, 2, tq, tq), lambda b, h: (h, 0, 0, 0)),
            pl.BlockSpec((None, 4, C_HEAD_DIM), lambda b, h: (layer, 0, 0)),
            pl.BlockSpec((None, 1, C_V_DIM), lambda b, h: (layer, 0, 0)),
        ],
        out_specs=seq(0),
        scratch_shapes=[pltpu.VMEM((seq_len, 2 * C_V_DIM), BF16)],
        compiler_params=_params(2, 40 << 20),
        name="diff_attention",
    )(qkv, qkv, qkv, bias_tiles, lam_params, subln_g.reshape(-1, 1, C_V_DIM))


def _xattn_kernel(q_ref, kv_ref, o_ref):
    scale = X_HEAD_DIM ** -0.5
    for h in range(X_HEADS):
        c0 = h * X_HEAD_DIM
        q = q_ref[:, c0:c0 + X_HEAD_DIM]
        k = kv_ref[:, c0:c0 + X_HEAD_DIM]
        v = kv_ref[:, D + c0:D + c0 + X_HEAD_DIM]
        s = lax.dot_general(q, k, (((1,), (1,)), ((), ())), preferred_element_type=F32) * scale
        m = jnp.max(s, axis=1, keepdims=True)
        p = jnp.exp(s - m)
        l = jnp.sum(p, axis=1, keepdims=True)
        o = jnp.dot(p.astype(BF16), v, preferred_element_type=F32) / l
        o_ref[:, c0:c0 + X_HEAD_DIM] = o.astype(BF16)


def cross_attention(q, kv, batch, seq_len, mem_len, *, tm=512):
    n = q.shape[0]
    blocks_per_seq = seq_len // tm
    return pl.pallas_call(
        _xattn_kernel,
        out_shape=jax.ShapeDtypeStruct((n, D), BF16),
        grid=(n // tm,),
        in_specs=[
            pl.BlockSpec((tm, D), lambda i: (i, 0)),
            pl.BlockSpec((mem_len, 2 * D), lambda i: (i // blocks_per_seq, 0)),
        ],
        out_specs=pl.BlockSpec((tm, D), lambda i: (i, 0)),
        compiler_params=_params(1, 32 << 20),
        name="cross_attention",
    )(q, kv)


ROUTE_T = 256


def _router_kernel(h_ref, wrt_ref, e_ref, w_ref, rank_ref, cnt_ref, base_ref):
    i = pl.program_id(0)

    @pl.when(i == 0)
    def _():
        base_ref[...] = jnp.zeros(base_ref.shape, F32)

    t = ROUTE_T
    nt = (((1,), (1,)), ((), ()))
    hf = h_ref[...]
    wf = wrt_ref[...]
    hh = hf.astype(BF16)
    wh = wf.astype(BF16)
    hl = (hf - hh.astype(F32)).astype(BF16)
    wl = (wf - wh.astype(F32)).astype(BF16)
    logits = (lax.dot_general(wh, hh, nt, preferred_element_type=F32)
              + lax.dot_general(wh, hl, nt, preferred_element_type=F32)
              + lax.dot_general(wl, hh, nt, preferred_element_type=F32))
    eidx = lax.broadcasted_iota(jnp.int32, (N_EXPERTS, t), 0)
    m1 = jnp.max(logits, axis=0, keepdims=True)
    e1 = jnp.min(jnp.where(logits == m1, eidx, N_EXPERTS), axis=0, keepdims=True)
    oh1 = eidx == e1
    rest = jnp.where(oh1, -jnp.inf, logits)
    m2 = jnp.max(rest, axis=0, keepdims=True)
    e2 = jnp.min(jnp.where(rest == m2, eidx, N_EXPERTS), axis=0, keepdims=True)
    oh2 = eidx == e2
    ex = jnp.exp(m2 - m1)
    den = 1.0 + ex
    e_ref[0:1, :] = e1
    e_ref[1:2, :] = e2
    w_ref[0:1, :] = 1.0 / den
    w_ref[1:2, :] = ex / den

    oh = (oh1.astype(F32) + oh2.astype(F32)).astype(BF16)
    r = lax.broadcasted_iota(jnp.int32, (t, t), 0)
    c = lax.broadcasted_iota(jnp.int32, (t, t), 1)
    before = (r < c).astype(BF16)
    ones = jnp.ones((t, t), BF16)
    pos = base_ref[...] + jnp.dot(oh, before, preferred_element_type=F32)
    rank_ref[0:1, :] = jnp.sum(jnp.where(oh1, pos, 0.0), axis=0, keepdims=True).astype(jnp.int32)
    rank_ref[1:2, :] = jnp.sum(jnp.where(oh2, pos, 0.0), axis=0, keepdims=True).astype(jnp.int32)
    total = base_ref[...] + jnp.dot(oh, ones, preferred_element_type=F32)
    base_ref[...] = total
    cnt_ref[...] = total[:, 0:128].astype(jnp.int32)


def moe_router(h, w_router, layer):
    n = h.shape[0]
    t = ROUTE_T
    wrt = jnp.swapaxes(w_router, 1, 2)
    pair = pl.BlockSpec((TOP_K, t), lambda i: (0, i))
    return pl.pallas_call(
        _router_kernel,
        out_shape=(
            jax.ShapeDtypeStruct((TOP_K, n), jnp.int32),
            jax.ShapeDtypeStruct((TOP_K, n), F32),
            jax.ShapeDtypeStruct((TOP_K, n), jnp.int32),
            jax.ShapeDtypeStruct((N_EXPERTS, 128), jnp.int32),
        ),
        grid=(n // t,),
        in_specs=[
            pl.BlockSpec((t, D), lambda i: (i, 0)),
            pl.BlockSpec((None, N_EXPERTS, D), lambda i: (layer, 0, 0)),
        ],
        out_specs=(pair, pair, pair, pl.BlockSpec((N_EXPERTS, 128), lambda i: (0, 0))),
        scratch_shapes=[pltpu.VMEM((N_EXPERTS, t), F32)],
        compiler_params=_params(1, 16 << 20),
        name="moe_router",
    )(h, wrt)


def _row_copy(src_ref, src_row, dst_ref, dst_row, sem):
    return pltpu.make_async_copy(src_ref.at[pl.ds(src_row, 1)], dst_ref.at[pl.ds(dst_row, 1)], sem)


DISPATCH_T = 256
ISSUE_UNROLL = 8


def _dispatch_kernel(dest_ref, pad_ref, nused_ref, h_ref, xb_hbm, zero_ref, sem, zsem, *, n_tokens, n_blocks):
    i = pl.program_id(0)
    t = DISPATCH_T
    rows = EXPERT_BLOCK

    @pl.when(i == 0)
    def _():
        zero_ref[...] = jnp.zeros(zero_ref.shape, F32)

        def zero_row(r, carry):
            _row_copy(zero_ref, 0, xb_hbm, r, zsem).start()
            return carry

        def wait_row(r, carry):
            _row_copy(zero_ref, 0, xb_hbm, 0, zsem).wait()
            return carry

        def block_copy(blk):
            return pltpu.make_async_copy(zero_ref, xb_hbm.at[pl.ds(pl.multiple_of(blk * rows, rows), rows)], zsem)

        def zero_block(blk, carry):
            block_copy(blk).start()
            return carry

        def wait_block(blk, carry):
            block_copy(0).wait()
            return carry

        for e in range(N_EXPERTS):
            lax.fori_loop(pad_ref[e], pad_ref[N_EXPERTS + e], zero_row, 0)
        lax.fori_loop(nused_ref[0], n_blocks, zero_block, 0)
        for e in range(N_EXPERTS):
            lax.fori_loop(pad_ref[e], pad_ref[N_EXPERTS + e], wait_row, 0)
        lax.fori_loop(nused_ref[0], n_blocks, wait_block, 0)

    def start(r, carry):
        tok = i * t + r
        _row_copy(h_ref, r, xb_hbm, dest_ref[tok], sem).start(priority=0)
        _row_copy(h_ref, r, xb_hbm, dest_ref[n_tokens + tok], sem).start(priority=1)
        return carry

    lax.fori_loop(0, t, start, 0, unroll=ISSUE_UNROLL)
    for _ in range(TOP_K):
        pltpu.make_async_copy(h_ref, xb_hbm.at[pl.ds(0, t)], sem).wait()


def moe_dispatch(h, dest, pad_rows, nused, n_blocks):
    n, d = h.shape
    t = DISPATCH_T
    rows = n_blocks * EXPERT_BLOCK
    kernel = functools.partial(_dispatch_kernel, n_tokens=n, n_blocks=n_blocks)
    return pl.pallas_call(
        kernel,
        out_shape=jax.ShapeDtypeStruct((rows, d), F32),
        grid_spec=pltpu.PrefetchScalarGridSpec(
            num_scalar_prefetch=3,
            grid=(n // t,),
            in_specs=[pl.BlockSpec((t, d), lambda i, *_: (i, 0))],
            out_specs=pl.BlockSpec(memory_space=pl.ANY),
            scratch_shapes=[pltpu.VMEM((EXPERT_BLOCK, d), F32), pltpu.SemaphoreType.DMA(()),
                            pltpu.SemaphoreType.DMA(())],
        ),
        compiler_params=_params(1, 16 << 20),
        name="moe_dispatch",
    )(dest, pad_rows, nused, h)


def _moe_up_kernel(be_ref, nused_ref, first_ref, nxte_ref, last_ref, x_ref, wg_hbm, wu_hbm, o_ref,
                   wgb_ref, wub_ref, sg_ref, su_ref, sems, *, layer, tn, n_col_tiles):
    j = pl.program_id(0)
    i = pl.program_id(1)

    def fetch(e, jj):
        cols = pl.ds(pl.multiple_of(jj * tn, 128), tn)
        return (pltpu.make_async_copy(wg_hbm.at[layer, e, :, cols], sg_ref, sems.at[0]),
                pltpu.make_async_copy(wu_hbm.at[layer, e, :, cols], su_ref, sems.at[1]))

    @pl.when(first_ref[i] == 1)
    def _():
        @pl.when(jnp.logical_and(j == 0, i == 0))
        def _():
            for c in fetch(be_ref[0], 0):
                c.start()

        for c in fetch(be_ref[i], j):
            c.wait()
        _cast_rows(sg_ref, wgb_ref)
        _cast_rows(su_ref, wub_ref)
        nj = j + last_ref[i]

        @pl.when(nj < n_col_tiles)
        def _():
            for c in fetch(nxte_ref[i], nj):
                c.start()

    @pl.when(i < nused_ref[0])
    def _():
        x = x_ref[...].astype(BF16)
        g = jnp.dot(x, wgb_ref[...], preferred_element_type=F32)
        u = jnp.dot(x, wub_ref[...], preferred_element_type=F32)
        o_ref[...] = (jax.nn.silu(g) * u).astype(o_ref.dtype)

    @pl.when(i >= nused_ref[0])
    def _():
        o_ref[...] = jnp.zeros(o_ref.shape, o_ref.dtype)


def _group_schedule(block_e):
    nb = block_e.shape[0]
    idx = jnp.arange(nb, dtype=jnp.int32)
    first = jnp.concatenate([jnp.ones((1,), bool), block_e[1:] != block_e[:-1]])
    later_first = jnp.logical_and(first[None, :], idx[None, :] > idx[:, None])
    nxt_idx = jnp.min(jnp.where(later_first, idx[None, :], nb), axis=1)
    last = nxt_idx >= nb
    nxt_idx = jnp.where(last, 0, nxt_idx)
    nxt_e = jnp.sum(jnp.where(idx[None, :] == nxt_idx[:, None], block_e[None, :], 0), axis=1)
    return first.astype(jnp.int32), nxt_e.astype(jnp.int32), last.astype(jnp.int32)


def moe_up(xb, w_gate, w_up, layer, block_e, nused, schedule, *, tn=1408):
    p, k = xb.shape
    rows = EXPERT_BLOCK
    f = w_gate.shape[-1]
    assert f % tn == 0 and tn % 128 == 0
    first, nxt_e, last = schedule
    kernel = functools.partial(_moe_up_kernel, layer=layer, tn=tn, n_col_tiles=f // tn)
    hbm = pl.BlockSpec(memory_space=pl.ANY)
    return pl.pallas_call(
        kernel,
        out_shape=jax.ShapeDtypeStruct((p, f), BF16),
        grid_spec=pltpu.PrefetchScalarGridSpec(
            num_scalar_prefetch=5,
            grid=(f // tn, p // rows),
            in_specs=[pl.BlockSpec((rows, k), lambda j, i, *_: (i, 0)), hbm, hbm],
            out_specs=pl.BlockSpec((rows, tn), lambda j, i, *_: (i, j)),
            scratch_shapes=[pltpu.VMEM((k, tn), BF16), pltpu.VMEM((k, tn), BF16),
                            pltpu.VMEM((k, tn), F32), pltpu.VMEM((k, tn), F32),
                            pltpu.SemaphoreType.DMA((2,))],
        ),
        compiler_params=_params(2, 52 << 20),
        name="moe_up",
    )(block_e, nused, first, nxt_e, last, xb, w_gate, w_up)


def _moe_down_kernel(be_ref, nused_ref, first_ref, nxte_ref, last_ref, x_ref, w_hbm, o_ref,
                     wb_ref, st_ref, sem, *, layer):
    i = pl.program_id(0)

    def fetch(e):
        return pltpu.make_async_copy(w_hbm.at[layer, e], st_ref, sem)

    @pl.when(first_ref[i] == 1)
    def _():
        @pl.when(i == 0)
        def _():
            fetch(be_ref[0]).start()

        fetch(be_ref[i]).wait()
        _cast_rows(st_ref, wb_ref)

        @pl.when(last_ref[i] == 0)
        def _():
            fetch(nxte_ref[i]).start()

    @pl.when(i < nused_ref[0])
    def _():
        o_ref[...] = jnp.dot(x_ref[...], wb_ref[...], preferred_element_type=F32)

    @pl.when(i >= nused_ref[0])
    def _():
        o_ref[...] = jnp.zeros(o_ref.shape, o_ref.dtype)


def moe_down(hb, w_down, layer, block_e, nused, schedule):
    p, k = hb.shape
    rows = EXPERT_BLOCK
    n = w_down.shape[-1]
    first, nxt_e, last = schedule
    kernel = functools.partial(_moe_down_kernel, layer=layer)
    return pl.pallas_call(
        kernel,
        out_shape=jax.ShapeDtypeStruct((p, n), F32),
        grid_spec=pltpu.PrefetchScalarGridSpec(
            num_scalar_prefetch=5,
            grid=(p // rows,),
            in_specs=[pl.BlockSpec((rows, k), lambda i, *_: (i, 0)), pl.BlockSpec(memory_space=pl.ANY)],
            out_specs=pl.BlockSpec((rows, n), lambda i, *_: (i, 0)),
            scratch_shapes=[pltpu.VMEM((k, n), BF16), pltpu.VMEM((k, n), F32), pltpu.SemaphoreType.DMA(())],
        ),
        compiler_params=_params(1, 50 << 20),
        name="moe_down",
    )(block_e, nused, first, nxt_e, last, hb, w_down)


COMBINE_T = 256


def _combine_kernel(dest_ref, y_hbm, wt_ref, h_ref, g_ref, b_ref, of_ref, ob_ref,
                    buf_ref, sems, *, n_tokens):
    i = pl.program_id(0)
    t = COMBINE_T

    def issue(block, slot):
        def body(r, carry):
            tok = block * t + r
            _row_copy(y_hbm, dest_ref[tok], buf_ref.at[slot, 0], r, sems.at[slot]).start(priority=0)
            _row_copy(y_hbm, dest_ref[n_tokens + tok], buf_ref.at[slot, 1], r, sems.at[slot]).start(priority=1)
            return carry

        lax.fori_loop(0, t, body, 0, unroll=ISSUE_UNROLL)

    slot = i % 2

    @pl.when(i == 0)
    def _():
        issue(0, 0)

    @pl.when(i + 1 < pl.num_programs(0))
    def _():
        issue(i + 1, 1 - slot)

    for k in range(TOP_K):
        pltpu.make_async_copy(y_hbm.at[pl.ds(0, t)], buf_ref.at[slot, k], sems.at[slot]).wait()
    ff = buf_ref[slot, 0] * wt_ref[:, 0:1] + buf_ref[slot, 1] * wt_ref[:, 1:2]
    y = _layer_norm(DN_ALPHA * h_ref[...] + ff, g_ref[...], b_ref[...])
    of_ref[...] = y
    ob_ref[...] = y.astype(BF16)


def moe_combine_ln(yb, dest, wts_t, h, ln_g, ln_b, layer, which):
    n, d = h.shape
    t = COMBINE_T
    row = pl.BlockSpec((t, d), lambda i, ds: (i, 0))
    par = pl.BlockSpec((None, None, 1, d), lambda i, ds: (layer, which, 0, 0))
    kernel = functools.partial(_combine_kernel, n_tokens=n)
    return pl.pallas_call(
        kernel,
        out_shape=(jax.ShapeDtypeStruct((n, d), F32), jax.ShapeDtypeStruct((n, d), BF16)),
        grid_spec=pltpu.PrefetchScalarGridSpec(
            num_scalar_prefetch=1,
            grid=(n // t,),
            in_specs=[
                pl.BlockSpec(memory_space=pl.ANY),
                pl.BlockSpec((t, TOP_K), lambda i, ds: (i, 0)),
                row, par, par,
            ],
            out_specs=(row, row),
            scratch_shapes=[pltpu.VMEM((2, TOP_K, t, d), F32), pltpu.SemaphoreType.DMA((2,))],
        ),
        compiler_params=_params(1, 32 << 20),
        name="moe_combine_ln",
    )(dest, yb, wts_t, h, ln_g, ln_b)


def moe_layer(h, hb, w_router, w_gate, w_up, w_down, ln_g, ln_b, layer, moe_idx):
    n = h.shape[0]
    m = n * TOP_K
    n_blocks = (m + N_EXPERTS * (EXPERT_BLOCK - 1) + EXPERT_BLOCK - 1) // EXPERT_BLOCK
    e, wts, rank, cnt = moe_router(h, w_router, moe_idx)
    counts = cnt[:, 0]
    padded = (counts + EXPERT_BLOCK - 1) // EXPERT_BLOCK * EXPERT_BLOCK
    pad_ends = jnp.cumsum(padded)
    pad_starts = pad_ends - padded
    expert_ids = jnp.arange(N_EXPERTS, dtype=jnp.int32)[:, None, None]
    dest = jnp.sum(jnp.where(e[None] == expert_ids, pad_starts[:, None, None], 0), axis=0) + rank
    block_start = jnp.arange(n_blocks, dtype=jnp.int32) * EXPERT_BLOCK
    block_e = jnp.minimum(jnp.sum(block_start[:, None] >= pad_ends[None, :], axis=1), N_EXPERTS - 1).astype(jnp.int32)
    nused = (pad_ends[-1:] // EXPERT_BLOCK).astype(jnp.int32)

    dest = dest.reshape(-1)
    pad_rows = jnp.concatenate([pad_starts + counts, pad_ends]).astype(jnp.int32)
    schedule = _group_schedule(block_e)
    xb = moe_dispatch(h, dest, pad_rows, nused, n_blocks)
    hid = moe_up(xb, w_gate, w_up, moe_idx, block_e, nused, schedule)
    yb = moe_down(hid, w_down, moe_idx, block_e, nused, schedule)
    return moe_combine_ln(yb, dest, wts.T, h, ln_g, ln_b, layer, 2)


def kernel(x, mem, rel_table, mix_w_in, gmlp_ln_g, gmlp_ln_b, gmlp_w_s, gmlp_b_s, pool_w, pool_scale, mix_w_out, diff_w_qkv, diff_lam_q1, diff_lam_k1, diff_lam_q2, diff_lam_k2, diff_subln_g, diff_w_o, xa_w_q, xa_w_kv, xa_w_o, ffn_w_gate, ffn_w_up, ffn_w_down, moe_w_router, moe_w_gate, moe_w_up, moe_w_down, ln_g, ln_b):
    batch, seq_len, d = x.shape
    mem_len = mem.shape[1]
    n = batch * seq_len
    h = x.reshape(n, d)
    hb = h
    memf = mem.reshape(batch * mem_len, d)
    ln_g4 = ln_g.reshape(DEPTH, 3, 1, d)
    ln_b4 = ln_b.reshape(DEPTH, 3, 1, d)
    lam_params = jnp.stack([diff_lam_q1, diff_lam_k1, diff_lam_q2, diff_lam_k2], axis=1)
    bias_tiles = attention_bias_tiles(rel_table)

    for layer in range(DEPTH):
        i = layer // 2
        if layer % 2 == 0:
            hm = matmul(hb, mix_w_in, i, tm=512, tn=1024, out_dtype=F32)
            mixed = gmlp_pool_mixer(hm, seq_len, gmlp_ln_g, gmlp_ln_b, gmlp_w_s, gmlp_b_s, pool_w,
                                    pool_scale, i)
            h, hb = matmul_residual_ln(mixed, mix_w_out, i, h, ln_g4, ln_b4, layer, 0)
        else:
            lam_init = 0.8 - 0.6 * math.exp(-0.3 * layer)
            qkv = matmul(hb, diff_w_qkv, i, tm=1024, tn=1024, out_dtype=BF16)
            att = diff_attention(qkv, bias_tiles, lam_params, diff_subln_g, i, lam_init, batch, seq_len)
            h, hb = matmul_residual_ln(att, diff_w_o, i, h, ln_g4, ln_b4, layer, 0)

        q = matmul(hb, xa_w_q, layer, tm=512, tn=1024, out_dtype=BF16)
        kv = matmul(memf, xa_w_kv, layer, tm=512, tn=1024, out_dtype=BF16)
        xo = cross_attention(q, kv, batch, seq_len, mem_len)
        h, hb = matmul_residual_ln(xo, xa_w_o, layer, h, ln_g4, ln_b4, layer, 1)

        if layer % 2 == 0:
            hid = swiglu_up(hb, ffn_w_gate, ffn_w_up, i, tm=1024, tn=512)
            ff = matmul(hid, ffn_w_down, i, tm=512, tn=512, out_dtype=F32)
            h, hb = residual_ln(h, ff, ln_g4, ln_b4, layer, 2)
        else:
            h, hb = moe_layer(h, hb, moe_w_router, moe_w_gate, moe_w_up, moe_w_down, ln_g4, ln_b4,
                              layer, i)
    return h.reshape(batch, seq_len, d)
```

```python
import functools
import math

import numpy as np
import jax
import jax.numpy as jnp
from jax import lax
from jax.experimental import pallas as pl
from jax.experimental.pallas import tpu as pltpu

F32 = jnp.float32
BF16 = jnp.bfloat16

D = 2048
DEPTH = 4
CHUNK = 64
GMLP_CHUNK = 128
A_WIDTH = D // 2
A_HEADS = 8
A_HEAD_DIM = A_WIDTH // A_HEADS
B_WIDTH = D // 2
POOL_WINDOWS = (2, 4, 8, 16)
B_GROUP = B_WIDTH // len(POOL_WINDOWS)
MIX_IN = 2 * A_WIDTH + B_WIDTH
C_HEADS = 16
C_HEAD_DIM = D // (2 * C_HEADS)
C_V_DIM = 2 * C_HEAD_DIM
REL_BUCKETS = 32
REL_MAX_DIST = 128
X_HEADS = 4
X_HEAD_DIM = D // X_HEADS
N_EXPERTS = 8
TOP_K = 2
EXPERT_BLOCK = 256
DN_ALPHA = (2 * DEPTH) ** 0.25
LN_EPS = 1e-5
NEG = -1e30

VMEM_CAP_BYTES = 56 * 1024 * 1024
CAST_ROWS = 256
ATT_TQ = 256
POOL_HALO = 16


def _params(n_axes, vmem_bytes):
    return pltpu.CompilerParams(
        dimension_semantics=("arbitrary",) * n_axes,
        vmem_limit_bytes=int(min(max(vmem_bytes, 16 * 1024 * 1024), VMEM_CAP_BYTES)),
    )


def _cast_rows(w_ref, wb_ref):
    k = w_ref.shape[0]
    rows = CAST_ROWS if k % CAST_ROWS == 0 else k

    def body(c, carry):
        r = pl.multiple_of(c * rows, rows)
        wb_ref[pl.ds(r, rows), :] = w_ref[pl.ds(r, rows), :].astype(BF16)
        return carry

    lax.fori_loop(0, k // rows, body, 0)


def _layer_norm(x, g, b):
    mu = jnp.mean(x, axis=-1, keepdims=True)
    xc = x - mu
    var = jnp.mean(xc * xc, axis=-1, keepdims=True)
    return xc * lax.rsqrt(var + LN_EPS) * g + b


def _mm_kernel(a_ref, w_ref, o_ref, wb_ref, *, scaled_tiles, scale):
    j = pl.program_id(0)

    @pl.when(pl.program_id(1) == 0)
    def _():
        _cast_rows(w_ref, wb_ref)

    a = a_ref[...].astype(BF16)
    y = jnp.dot(a, wb_ref[...], preferred_element_type=F32)
    if scaled_tiles:
        y = y * jnp.where(j < scaled_tiles, scale, 1.0)
    o_ref[...] = y.astype(o_ref.dtype)


def matmul(a, w, layer, *, tm, tn, out_dtype, scaled_cols=0, scale=1.0):
    m, k = a.shape
    n = w.shape[-1]
    assert m % tm == 0 and n % tn == 0 and w.shape[-2] == k and scaled_cols % tn == 0
    vmem = 2 * tm * k * a.dtype.itemsize + 2 * k * tn * 4 + k * tn * 2 + 2 * tm * tn * 4 + tm * tn * 4
    kernel = functools.partial(_mm_kernel, scaled_tiles=scaled_cols // tn, scale=scale)
    return pl.pallas_call(
        kernel,
        out_shape=jax.ShapeDtypeStruct((m, n), out_dtype),
        grid=(n // tn, m // tm),
        in_specs=[
            pl.BlockSpec((tm, k), lambda j, i: (i, 0)),
            pl.BlockSpec((None, k, tn), lambda j, i: (layer, 0, j)),
        ],
        out_specs=pl.BlockSpec((tm, tn), lambda j, i: (i, j)),
        scratch_shapes=[pltpu.VMEM((k, tn), BF16)],
        compiler_params=_params(2, vmem + (4 << 20)),
        name="matmul",
    )(a, w)


MMLN_GROUPS = 2


def _mm_ln_kernel(a_ref, w_ref, h_ref, g_ref, b_ref, of_ref, ob_ref, wb_ref):
    @pl.when(pl.program_id(0) == 0)
    def _():
        _cast_rows(w_ref, wb_ref)

    rows = a_ref.shape[0] // MMLN_GROUPS
    for r in range(MMLN_GROUPS):
        sl = slice(r * rows, (r + 1) * rows)
        y = jnp.dot(a_ref[sl, :], wb_ref[...], preferred_element_type=F32)
        out = _layer_norm(DN_ALPHA * h_ref[sl, :] + y, g_ref[...], b_ref[...])
        of_ref[sl, :] = out
        ob_ref[sl, :] = out.astype(BF16)


def matmul_residual_ln(a, w, w_layer, h, ln_g, ln_b, layer, which, *, tm=512):
    m, k = a.shape
    d = w.shape[-1]
    assert m % tm == 0 and h.shape == (m, d)
    row = pl.BlockSpec((tm, d), lambda i: (i, 0))
    par = pl.BlockSpec((None, None, 1, d), lambda i: (layer, which, 0, 0))
    vmem = k * d * 4 + k * d * 2 + 2 * tm * k * 2 + 4 * tm * d * 4 + 2 * tm * d * 2 + 3 * tm * d * 4
    return pl.pallas_call(
        _mm_ln_kernel,
        out_shape=(jax.ShapeDtypeStruct((m, d), F32), jax.ShapeDtypeStruct((m, d), BF16)),
        grid=(m // tm,),
        in_specs=[
            pl.BlockSpec((tm, k), lambda i: (i, 0)),
            pl.BlockSpec((None, k, d), lambda i: (w_layer, 0, 0), pipeline_mode=pl.Buffered(1)),
            row, par, par,
        ],
        out_specs=(row, row),
        scratch_shapes=[pltpu.VMEM((k, d), BF16)],
        compiler_params=_params(1, vmem + (4 << 20)),
        name="matmul_residual_ln",
    )(a, w, h, ln_g, ln_b)


def _mm_layers_kernel(a_ref, w_ref, o_ref):
    o_ref[...] = jnp.dot(a_ref[...].astype(BF16), w_ref[...].astype(BF16),
                         preferred_element_type=F32).astype(o_ref.dtype)


def matmul_all_layers(a, w, *, tn, out_dtype):
    m, k = a.shape
    layers, _, n = w.shape
    assert n % tn == 0
    return pl.pallas_call(
        _mm_layers_kernel,
        out_shape=jax.ShapeDtypeStruct((layers, m, n), out_dtype),
        grid=(layers, n // tn),
        in_specs=[
            pl.BlockSpec((m, k), lambda l, j: (0, 0)),
            pl.BlockSpec((None, k, tn), lambda l, j: (l, 0, j)),
        ],
        out_specs=pl.BlockSpec((None, m, tn), lambda l, j: (l, 0, j)),
        compiler_params=_params(2, 2 * m * k * 4 + 3 * k * tn * 4 + 3 * m * tn * 4 + (4 << 20)),
        name="matmul_all_layers",
    )(a, w)


def _swiglu_kernel(a_ref, wg_ref, wu_ref, o_ref, wgb_ref, wub_ref):
    @pl.when(pl.program_id(1) == 0)
    def _():
        _cast_rows(wg_ref, wgb_ref)
        _cast_rows(wu_ref, wub_ref)

    a = a_ref[...]
    g = jnp.dot(a, wgb_ref[...], preferred_element_type=F32)
    u = jnp.dot(a, wub_ref[...], preferred_element_type=F32)
    o_ref[...] = (jax.nn.silu(g) * u).astype(o_ref.dtype)


def swiglu_up(a, wg, wu, layer, *, tm, tn):
    m, k = a.shape
    n = wg.shape[-1]
    assert m % tm == 0 and n % tn == 0
    vmem = 2 * tm * k * 2 + 4 * k * tn * 4 + 2 * k * tn * 2 + 2 * tm * tn * 2 + 3 * tm * tn * 4
    wspec = pl.BlockSpec((None, k, tn), lambda j, i: (layer, 0, j))
    return pl.pallas_call(
        _swiglu_kernel,
        out_shape=jax.ShapeDtypeStruct((m, n), BF16),
        grid=(n // tn, m // tm),
        in_specs=[pl.BlockSpec((tm, k), lambda j, i: (i, 0)), wspec, wspec],
        out_specs=pl.BlockSpec((tm, tn), lambda j, i: (i, j)),
        scratch_shapes=[pltpu.VMEM((k, tn), BF16), pltpu.VMEM((k, tn), BF16)],
        compiler_params=_params(2, vmem + (4 << 20)),
        name="swiglu_up",
    )(a, wg, wu)


def _ln_kernel(h_ref, y_ref, g_ref, b_ref, of_ref, ob_ref):
    x = DN_ALPHA * h_ref[...] + y_ref[...]
    y = _layer_norm(x, g_ref[...], b_ref[...])
    of_ref[...] = y
    ob_ref[...] = y.astype(BF16)


def residual_ln(h, y, ln_g, ln_b, layer, which, *, tm=256):
    m, d = h.shape
    row = pl.BlockSpec((tm, d), lambda i: (i, 0))
    par = pl.BlockSpec((None, None, 1, d), lambda i: (layer, which, 0, 0))
    return pl.pallas_call(
        _ln_kernel,
        out_shape=(jax.ShapeDtypeStruct((m, d), F32), jax.ShapeDtypeStruct((m, d), BF16)),
        grid=(m // tm,),
        in_specs=[row, row, par, par],
        out_specs=(row, row),
        compiler_params=_params(1, 16 * tm * d * 4),
        name="residual_ln",
    )(h, y, ln_g, ln_b)


MIX_ROWS = 2 * GMLP_CHUNK


def _mixer_kernel(hm_ref, prev_ref, lng_ref, lnb_ref, ws_ref, bst_ref, pw_ref, ps_ref, o_ref,
                  wbd_ref, ext_ref, dbl_ref, *, blocks_per_seq):
    i = pl.program_id(0)

    @pl.when(i == 0)
    def _():
        p = lax.broadcasted_iota(jnp.int32, (GMLP_CHUNK, GMLP_CHUNK), 0)
        q = lax.broadcasted_iota(jnp.int32, (GMLP_CHUNK, GMLP_CHUNK), 1)
        allowed = (q // CHUNK) <= (p // CHUNK)
        wbd_ref[...] = jnp.zeros(wbd_ref.shape, BF16)
        for h in range(A_HEADS):
            w = jnp.where(allowed, ws_ref[h], 0.0).astype(BF16)
            wbd_ref[h, 0:GMLP_CHUNK, 0:GMLP_CHUNK] = w
            wbd_ref[h, GMLP_CHUNK:MIX_ROWS, GMLP_CHUNK:MIX_ROWS] = w

    z = jax.nn.gelu(hm_ref[:, 0:2 * A_WIDTH])
    u = z[:, 0:A_WIDTH]
    v = _layer_norm(z[:, A_WIDTH:2 * A_WIDTH], lng_ref[...], lnb_ref[...]).astype(BF16)
    for h in range(A_HEADS):
        c0 = h * A_HEAD_DIM
        sv = jnp.dot(wbd_ref[h], v[:, c0:c0 + A_HEAD_DIM], preferred_element_type=F32)
        sv = sv + bst_ref[:, h:h + 1]
        o_ref[:, c0:c0 + A_HEAD_DIM] = (u[:, c0:c0 + A_HEAD_DIM] * sv).astype(BF16)

    blk = i % blocks_per_seq
    prev = jnp.where(blk == 0, 0.0, prev_ref[...])
    ext_ref[0:POOL_HALO, :] = prev
    ext_ref[POOL_HALO:POOL_HALO + MIX_ROWS, :] = hm_ref[:, 2 * A_WIDTH:MIX_IN]
    t1 = blk * MIX_ROWS + lax.broadcasted_iota(jnp.int32, (MIX_ROWS, 1), 0) + 1
    for g, w in enumerate(POOL_WINDOWS):
        c0 = g * B_GROUP
        cols = slice(c0, c0 + B_GROUP)
        pin = ext_ref[POOL_HALO:POOL_HALO + MIX_ROWS, cols]
        src_ref, src_cols = ext_ref, cols
        k = 1
        while k < w:
            lo = 2 * k - 1
            n = POOL_HALO + MIX_ROWS - lo
            dbl_ref[lo:lo + n, :] = src_ref[lo:lo + n, src_cols] + src_ref[lo - k:lo - k + n, src_cols]
            src_ref, src_cols = dbl_ref, slice(None)
            k *= 2
        s = dbl_ref[POOL_HALO:POOL_HALO + MIX_ROWS, :]
        cnt = jnp.minimum(t1, w).astype(F32)
        pooled = s / cnt - pin
        b = jnp.dot(pooled.astype(BF16), pw_ref[g].astype(BF16), preferred_element_type=F32)
        b = b * ps_ref[:, c0:c0 + B_GROUP]
        o_ref[:, A_WIDTH + c0:A_WIDTH + c0 + B_GROUP] = b.astype(BF16)


def gmlp_pool_mixer(hm, seq_len, ln_g, ln_b, w_s, b_s, pool_w, pool_scale, layer):
    n = hm.shape[0]
    assert seq_len % MIX_ROWS == 0 and max(POOL_WINDOWS) <= POOL_HALO
    blocks_per_seq = seq_len // MIX_ROWS
    halo_per_block = MIX_ROWS // POOL_HALO
    bst = jnp.tile(b_s[layer].T, (MIX_ROWS // GMLP_CHUNK, 1))
    kernel = functools.partial(_mixer_kernel, blocks_per_seq=blocks_per_seq)
    return pl.pallas_call(
        kernel,
        out_shape=jax.ShapeDtypeStruct((n, D), BF16),
        grid=(n // MIX_ROWS,),
        in_specs=[
            pl.BlockSpec((MIX_ROWS, MIX_IN), lambda i: (i, 0)),
            pl.BlockSpec((POOL_HALO, B_WIDTH),
                         lambda i: (jnp.maximum(i * halo_per_block - 1, 0), 2 * A_WIDTH // B_WIDTH)),
            pl.BlockSpec((None, 1, A_WIDTH), lambda i: (layer, 0, 0)),
            pl.BlockSpec((None, 1, A_WIDTH), lambda i: (layer, 0, 0)),
            pl.BlockSpec((None, A_HEADS, GMLP_CHUNK, GMLP_CHUNK), lambda i: (layer, 0, 0, 0)),
            pl.BlockSpec((MIX_ROWS, A_HEADS), lambda i: (0, 0)),
            pl.BlockSpec((None, len(POOL_WINDOWS), B_GROUP, B_GROUP), lambda i: (layer, 0, 0, 0)),
            pl.BlockSpec((None, 1, B_WIDTH), lambda i: (layer, 0, 0)),
        ],
        out_specs=pl.BlockSpec((MIX_ROWS, D), lambda i: (i, 0)),
        scratch_shapes=[
            pltpu.VMEM((A_HEADS, MIX_ROWS, MIX_ROWS), BF16),
            pltpu.VMEM((POOL_HALO + MIX_ROWS, B_WIDTH), F32),
            pltpu.VMEM((POOL_HALO + MIX_ROWS, B_GROUP), F32),
        ],
        compiler_params=_params(1, 40 << 20),
        name="gmlp_pool_mixer",
    )(hm, hm, ln_g.reshape(-1, 1, A_WIDTH), ln_b.reshape(-1, 1, A_WIDTH), w_s, bst, pool_w,
      pool_scale.reshape(-1, 1, B_WIDTH))


def _t5_bucket_np(rel):
    half = REL_BUCKETS // 2
    max_exact = half // 2
    rel = np.asarray(rel, dtype=np.int64)
    n = np.abs(rel)
    steps = half - max_exact
    ratio = REL_MAX_DIST // max_exact
    large = np.zeros_like(n)
    nn = n.astype(object) ** steps
    for j in range(1, steps + 1):
        large = large + (nn >= (max_exact ** steps) * (ratio ** j)).astype(np.int64)
    large = np.minimum(max_exact + large, half - 1)
    return np.where(rel > 0, half, 0) + np.where(n < max_exact, n, large)


def _bias_bucket_tiles():
    qi = np.arange(ATT_TQ)[:, None]
    kj = np.arange(ATT_TQ)[None, :]
    diag = _t5_bucket_np(kj - qi)
    diag = np.where((kj // CHUNK) <= (qi // CHUNK), diag, -1)
    prev = _t5_bucket_np(kj - ATT_TQ - qi)
    return np.stack([diag, prev], 0).astype(np.int32)


FAR_BUCKET = REL_BUCKETS // 2 - 1
LOG2E = math.log2(math.e)


def _bias_kernel(rel_ref, bk_ref, o_ref):
    h = pl.program_id(0)
    bk = bk_ref[...]
    acc = jnp.zeros(bk.shape, F32)
    for b in range(REL_BUCKETS):
        acc = jnp.where(bk == b, rel_ref[b, h], acc)
    o_ref[...] = jnp.where(bk < 0, NEG, (acc - rel_ref[FAR_BUCKET, h]) * LOG2E)


def attention_bias_tiles(rel_table):
    assert ATT_TQ >= REL_MAX_DIST and ATT_TQ % CHUNK == 0
    bk = jnp.asarray(_bias_bucket_tiles())
    return pl.pallas_call(
        _bias_kernel,
        out_shape=jax.ShapeDtypeStruct((C_HEADS, 2, ATT_TQ, ATT_TQ), F32),
        grid=(C_HEADS,),
        in_specs=[
            pl.BlockSpec(memory_space=pltpu.SMEM),
            pl.BlockSpec((2, ATT_TQ, ATT_TQ), lambda h: (0, 0, 0)),
        ],
        out_specs=pl.BlockSpec((None, 2, ATT_TQ, ATT_TQ), lambda h: (h, 0, 0, 0)),
        compiler_params=_params(1, 16 << 20),
        name="attention_bias_tiles",
    )(rel_table, bk)


def _diff_attn_kernel(q_ref, k_ref, v_ref, bias_ref, lam_ref, g_ref, o_ref, vx_ref, *, lam_init, seq_len):
    tq = ATT_TQ
    nt = (((1,), (1,)), ((), ()))
    lp = lam_ref[...]
    lam = (jnp.exp(jnp.sum(lp[0:1] * lp[1:2], axis=1, keepdims=True))
           - jnp.exp(jnp.sum(lp[2:3] * lp[3:4], axis=1, keepdims=True)) + lam_init)
    lane = lax.broadcasted_iota(jnp.int32, (tq, C_V_DIM), 1)
    vx_ref[:, 0:C_V_DIM] = v_ref[...]
    vx_ref[:, C_V_DIM:2 * C_V_DIM] = jnp.ones((seq_len, C_V_DIM), BF16)
    for qi in range(seq_len // tq):
        q = q_ref[qi * tq:(qi + 1) * tq, :]
        zero = jnp.zeros_like(q)
        parts = []
        if qi >= 2:
            parts.append((0, (qi - 1) * tq, None))
        if qi >= 1:
            parts.append(((qi - 1) * tq, qi * tq, 1))
        parts.append((qi * tq, (qi + 1) * tq, 0))
        outs = []
        for half in range(2):
            keep = (lane < C_HEAD_DIM) if half == 0 else (lane >= C_HEAD_DIM)
            qh = jnp.where(keep, q, zero)
            scores = []
            m = None
            for a, b, bi in parts:
                s = lax.dot_general(qh, k_ref[a:b, :], nt, preferred_element_type=F32)
                if bi is not None:
                    s = s + bias_ref[bi]
                scores.append(s)
                sm = jnp.max(s, axis=1, keepdims=True)
                m = sm if m is None else jnp.maximum(m, sm)
            acc = None
            for s, (a, b, _) in zip(scores, parts):
                p = jnp.exp2(s - m).astype(BF16)
                pv = jnp.dot(p, vx_ref[a:b, :], preferred_element_type=F32)
                acc = pv if acc is None else acc + pv
            outs.append(acc[:, 0:C_V_DIM] / acc[:, C_V_DIM:C_V_DIM + 1])
        o = outs[0] - lam * outs[1]
        y = o * lax.rsqrt(jnp.mean(o * o, axis=-1, keepdims=True) + LN_EPS) * g_ref[...]
        o_ref[qi * tq:(qi + 1) * tq, :] = (y * (1.0 - lam_init)).astype(BF16)


def diff_attention(qkv, bias_tiles, lam_params, subln_g, layer, lam_init, batch, seq_len):
    n = qkv.shape[0]
    tq = ATT_TQ
    assert seq_len % tq == 0
    kernel = functools.partial(_diff_attn_kernel, lam_init=lam_init, seq_len=seq_len)
    seq = lambda off: pl.BlockSpec((seq_len, C_V_DIM), lambda b, h: (b, off + h))
    return pl.pallas_call(
        kernel,
        out_shape=jax.ShapeDtypeStruct((n, D), BF16),
        grid=(batch, C_HEADS),
        in_specs=[
            seq(0), seq(C_HEADS), seq(2 * C_HEADS),
            pl.BlockSpec((None, 2, tq, tq), lambda b, h: (h, 0, 0, 0)),
            pl.BlockSpec((None, 4, C_HEAD_DIM), lambda b, h: (layer, 0, 0)),
            pl.BlockSpec((None, 1, C_V_DIM), lambda b, h: (layer, 0, 0)),
        ],
        out_specs=seq(0),
        scratch_shapes=[pltpu.VMEM((seq_len, 2 * C_V_DIM), BF16)],
        compiler_params=_params(2, 40 << 20),
        name="diff_attention",
    )(qkv, qkv, qkv, bias_tiles, lam_params, subln_g.reshape(-1, 1, C_V_DIM))


def _xattn_kernel(q_ref, kv_ref, o_ref):
    for h in range(X_HEADS):
        c0 = h * X_HEAD_DIM
        q = q_ref[:, c0:c0 + X_HEAD_DIM]
        k = kv_ref[:, c0:c0 + X_HEAD_DIM]
        v = kv_ref[:, D + c0:D + c0 + X_HEAD_DIM]
        s = lax.dot_general(q, k, (((1,), (1,)), ((), ())), preferred_element_type=F32)
        m = jnp.max(s, axis=1, keepdims=True)
        p = jnp.exp2(s - m)
        l = jnp.sum(p, axis=1, keepdims=True)
        o = jnp.dot(p.astype(BF16), v, preferred_element_type=F32) / l
        o_ref[:, c0:c0 + X_HEAD_DIM] = o.astype(BF16)


def cross_attention(q, kv, layer, batch, seq_len, mem_len, *, tm=512):
    n = q.shape[0]
    blocks_per_seq = seq_len // tm
    return pl.pallas_call(
        _xattn_kernel,
        out_shape=jax.ShapeDtypeStruct((n, D), BF16),
        grid=(n // tm,),
        in_specs=[
            pl.BlockSpec((tm, D), lambda i: (i, 0)),
            pl.BlockSpec((None, mem_len, 2 * D), lambda i: (layer, i // blocks_per_seq, 0)),
        ],
        out_specs=pl.BlockSpec((tm, D), lambda i: (i, 0)),
        compiler_params=_params(1, 32 << 20),
        name="cross_attention",
    )(q, kv)


ROUTE_T = 256


def _router_kernel(h_ref, wrt_ref, e_ref, w_ref, rank_ref, cnt_ref, base_ref):
    i = pl.program_id(0)

    @pl.when(i == 0)
    def _():
        base_ref[...] = jnp.zeros(base_ref.shape, F32)

    t = ROUTE_T
    nt = (((1,), (1,)), ((), ()))
    hf = h_ref[...]
    wf = wrt_ref[...]
    hh = hf.astype(BF16)
    wh = wf.astype(BF16)
    hl = (hf - hh.astype(F32)).astype(BF16)
    wl = (wf - wh.astype(F32)).astype(BF16)
    logits = (lax.dot_general(wh, hh, nt, preferred_element_type=F32)
              + lax.dot_general(wh, hl, nt, preferred_element_type=F32)
              + lax.dot_general(wl, hh, nt, preferred_element_type=F32))
    eidx = lax.broadcasted_iota(jnp.int32, (N_EXPERTS, t), 0)
    m1 = jnp.max(logits, axis=0, keepdims=True)
    e1 = jnp.min(jnp.where(logits == m1, eidx, N_EXPERTS), axis=0, keepdims=True)
    oh1 = eidx == e1
    rest = jnp.where(oh1, -jnp.inf, logits)
    m2 = jnp.max(rest, axis=0, keepdims=True)
    e2 = jnp.min(jnp.where(rest == m2, eidx, N_EXPERTS), axis=0, keepdims=True)
    oh2 = eidx == e2
    ex = jnp.exp(m2 - m1)
    den = 1.0 + ex
    e_ref[0:1, :] = e1
    e_ref[1:2, :] = e2
    w_ref[0:1, :] = 1.0 / den
    w_ref[1:2, :] = ex / den

    oh = (oh1.astype(F32) + oh2.astype(F32)).astype(BF16)
    r = lax.broadcasted_iota(jnp.int32, (t, t), 0)
    c = lax.broadcasted_iota(jnp.int32, (t, t), 1)
    before = (r < c).astype(BF16)
    ones = jnp.ones((t, t), BF16)
    pos = base_ref[...] + jnp.dot(oh, before, preferred_element_type=F32)
    rank_ref[0:1, :] = jnp.sum(jnp.where(oh1, pos, 0.0), axis=0, keepdims=True).astype(jnp.int32)
    rank_ref[1:2, :] = jnp.sum(jnp.where(oh2, pos, 0.0), axis=0, keepdims=True).astype(jnp.int32)
    total = base_ref[...] + jnp.dot(oh, ones, preferred_element_type=F32)
    base_ref[...] = total
    cnt_ref[...] = total[:, 0:128].astype(jnp.int32)


def moe_router(h, w_router, layer):
    n = h.shape[0]
    t = ROUTE_T
    wrt = jnp.swapaxes(w_router, 1, 2)
    pair = pl.BlockSpec((TOP_K, t), lambda i: (0, i))
    return pl.pallas_call(
        _router_kernel,
        out_shape=(
            jax.ShapeDtypeStruct((TOP_K, n), jnp.int32),
            jax.ShapeDtypeStruct((TOP_K, n), F32),
            jax.ShapeDtypeStruct((TOP_K, n), jnp.int32),
            jax.ShapeDtypeStruct((N_EXPERTS, 128), jnp.int32),
        ),
        grid=(n // t,),
        in_specs=[
            pl.BlockSpec((t, D), lambda i: (i, 0)),
            pl.BlockSpec((None, N_EXPERTS, D), lambda i: (layer, 0, 0)),
        ],
        out_specs=(pair, pair, pair, pl.BlockSpec((N_EXPERTS, 128), lambda i: (0, 0))),
        scratch_shapes=[pltpu.VMEM((N_EXPERTS, t), F32)],
        compiler_params=_params(1, 16 << 20),
        name="moe_router",
    )(h, wrt)


def _row_copy(src_ref, src_row, dst_ref, dst_row, sem):
    return pltpu.make_async_copy(src_ref.at[pl.ds(src_row, 1)], dst_ref.at[pl.ds(dst_row, 1)], sem)


DISPATCH_T = 256
ISSUE_UNROLL = 8


def _dispatch_kernel(dest_ref, pad_ref, nused_ref, h_ref, xb_hbm, zero_ref, sem, zsem, *, n_tokens, n_blocks):
    i = pl.program_id(0)
    t = DISPATCH_T
    rows = EXPERT_BLOCK

    @pl.when(i == 0)
    def _():
        zero_ref[...] = jnp.zeros(zero_ref.shape, F32)

        def zero_row(r, carry):
            _row_copy(zero_ref, 0, xb_hbm, r, zsem).start()
            return carry

        def wait_row(r, carry):
            _row_copy(zero_ref, 0, xb_hbm, 0, zsem).wait()
            return carry

        def block_copy(blk):
            return pltpu.make_async_copy(zero_ref, xb_hbm.at[pl.ds(pl.multiple_of(blk * rows, rows), rows)], zsem)

        def zero_block(blk, carry):
            block_copy(blk).start()
            return carry

        def wait_block(blk, carry):
            block_copy(0).wait()
            return carry

        for e in range(N_EXPERTS):
            lax.fori_loop(pad_ref[e], pad_ref[N_EXPERTS + e], zero_row, 0)
        lax.fori_loop(nused_ref[0], n_blocks, zero_block, 0)
        for e in range(N_EXPERTS):
            lax.fori_loop(pad_ref[e], pad_ref[N_EXPERTS + e], wait_row, 0)
        lax.fori_loop(nused_ref[0], n_blocks, wait_block, 0)

    def start(r, carry):
        tok = i * t + r
        _row_copy(h_ref, r, xb_hbm, dest_ref[tok], sem).start(priority=0)
        _row_copy(h_ref, r, xb_hbm, dest_ref[n_tokens + tok], sem).start(priority=1)
        return carry

    lax.fori_loop(0, t, start, 0, unroll=ISSUE_UNROLL)
    for _ in range(TOP_K):
        pltpu.make_async_copy(h_ref, xb_hbm.at[pl.ds(0, t)], sem).wait()


def moe_dispatch(h, dest, pad_rows, nused, n_blocks):
    n, d = h.shape
    t = DISPATCH_T
    rows = n_blocks * EXPERT_BLOCK
    kernel = functools.partial(_dispatch_kernel, n_tokens=n, n_blocks=n_blocks)
    return pl.pallas_call(
        kernel,
        out_shape=jax.ShapeDtypeStruct((rows, d), F32),
        grid_spec=pltpu.PrefetchScalarGridSpec(
            num_scalar_prefetch=3,
            grid=(n // t,),
            in_specs=[pl.BlockSpec((t, d), lambda i, *_: (i, 0))],
            out_specs=pl.BlockSpec(memory_space=pl.ANY),
            scratch_shapes=[pltpu.VMEM((EXPERT_BLOCK, d), F32), pltpu.SemaphoreType.DMA(()),
                            pltpu.SemaphoreType.DMA(())],
        ),
        compiler_params=_params(1, 16 << 20),
        name="moe_dispatch",
    )(dest, pad_rows, nused, h)


def _moe_up_kernel(be_ref, nused_ref, first_ref, nxte_ref, last_ref, x_ref, wg_hbm, wu_hbm, o_ref,
                   wgb_ref, wub_ref, sg_ref, su_ref, sems, *, layer, tn, n_col_tiles):
    j = pl.program_id(0)
    i = pl.program_id(1)

    def fetch(e, jj):
        cols = pl.ds(pl.multiple_of(jj * tn, 128), tn)
        return (pltpu.make_async_copy(wg_hbm.at[layer, e, :, cols], sg_ref, sems.at[0]),
                pltpu.make_async_copy(wu_hbm.at[layer, e, :, cols], su_ref, sems.at[1]))

    @pl.when(first_ref[i] == 1)
    def _():
        @pl.when(jnp.logical_and(j == 0, i == 0))
        def _():
            for c in fetch(be_ref[0], 0):
                c.start()

        for c in fetch(be_ref[i], j):
            c.wait()
        _cast_rows(sg_ref, wgb_ref)
        _cast_rows(su_ref, wub_ref)
        nj = j + last_ref[i]

        @pl.when(nj < n_col_tiles)
        def _():
            for c in fetch(nxte_ref[i], nj):
                c.start()

    @pl.when(i < nused_ref[0])
    def _():
        x = x_ref[...].astype(BF16)
        g = jnp.dot(x, wgb_ref[...], preferred_element_type=F32)
        u = jnp.dot(x, wub_ref[...], preferred_element_type=F32)
        o_ref[...] = (jax.nn.silu(g) * u).astype(o_ref.dtype)

    @pl.when(i >= nused_ref[0])
    def _():
        o_ref[...] = jnp.zeros(o_ref.shape, o_ref.dtype)


def _group_schedule(block_e):
    nb = block_e.shape[0]
    idx = jnp.arange(nb, dtype=jnp.int32)
    first = jnp.concatenate([jnp.ones((1,), bool), block_e[1:] != block_e[:-1]])
    later_first = jnp.logical_and(first[None, :], idx[None, :] > idx[:, None])
    nxt_idx = jnp.min(jnp.where(later_first, idx[None, :], nb), axis=1)
    last = nxt_idx >= nb
    nxt_idx = jnp.where(last, 0, nxt_idx)
    nxt_e = jnp.sum(jnp.where(idx[None, :] == nxt_idx[:, None], block_e[None, :], 0), axis=1)
    return first.astype(jnp.int32), nxt_e.astype(jnp.int32), last.astype(jnp.int32)


def moe_up(xb, w_gate, w_up, layer, block_e, nused, schedule, *, tn=1408):
    p, k = xb.shape
    rows = EXPERT_BLOCK
    f = w_gate.shape[-1]
    assert f % tn == 0 and tn % 128 == 0
    first, nxt_e, last = schedule
    kernel = functools.partial(_moe_up_kernel, layer=layer, tn=tn, n_col_tiles=f // tn)
    hbm = pl.BlockSpec(memory_space=pl.ANY)
    return pl.pallas_call(
        kernel,
        out_shape=jax.ShapeDtypeStruct((p, f), BF16),
        grid_spec=pltpu.PrefetchScalarGridSpec(
            num_scalar_prefetch=5,
            grid=(f // tn, p // rows),
            in_specs=[pl.BlockSpec((rows, k), lambda j, i, *_: (i, 0)), hbm, hbm],
            out_specs=pl.BlockSpec((rows, tn), lambda j, i, *_: (i, j)),
            scratch_shapes=[pltpu.VMEM((k, tn), BF16), pltpu.VMEM((k, tn), BF16),
                            pltpu.VMEM((k, tn), F32), pltpu.VMEM((k, tn), F32),
                            pltpu.SemaphoreType.DMA((2,))],
        ),
        compiler_params=_params(2, 52 << 20),
        name="moe_up",
    )(block_e, nused, first, nxt_e, last, xb, w_gate, w_up)


def _moe_down_kernel(be_ref, nused_ref, first_ref, nxte_ref, last_ref, x_ref, w_hbm, o_ref,
                     wb_ref, st_ref, sem, *, layer):
    i = pl.program_id(0)

    def fetch(e):
        return pltpu.make_async_copy(w_hbm.at[layer, e], st_ref, sem)

    @pl.when(first_ref[i] == 1)
    def _():
        @pl.when(i == 0)
        def _():
            fetch(be_ref[0]).start()

        fetch(be_ref[i]).wait()
        _cast_rows(st_ref, wb_ref)

        @pl.when(last_ref[i] == 0)
        def _():
            fetch(nxte_ref[i]).start()

    @pl.when(i < nused_ref[0])
    def _():
        o_ref[...] = jnp.dot(x_ref[...], wb_ref[...], preferred_element_type=F32)

    @pl.when(i >= nused_ref[0])
    def _():
        o_ref[...] = jnp.zeros(o_ref.shape, o_ref.dtype)


def moe_down(hb, w_down, layer, block_e, nused, schedule):
    p, k = hb.shape
    rows = EXPERT_BLOCK
    n = w_down.shape[-1]
    first, nxt_e, last = schedule
    kernel = functools.partial(_moe_down_kernel, layer=layer)
    return pl.pallas_call(
        kernel,
        out_shape=jax.ShapeDtypeStruct((p, n), F32),
        grid_spec=pltpu.PrefetchScalarGridSpec(
            num_scalar_prefetch=5,
            grid=(p // rows,),
            in_specs=[pl.BlockSpec((rows, k), lambda i, *_: (i, 0)), pl.BlockSpec(memory_space=pl.ANY)],
            out_specs=pl.BlockSpec((rows, n), lambda i, *_: (i, 0)),
            scratch_shapes=[pltpu.VMEM((k, n), BF16), pltpu.VMEM((k, n), F32), pltpu.SemaphoreType.DMA(())],
        ),
        compiler_params=_params(1, 50 << 20),
        name="moe_down",
    )(block_e, nused, first, nxt_e, last, hb, w_down)


COMBINE_T = 256


def _combine_kernel(dest_ref, y_hbm, wt_ref, h_ref, g_ref, b_ref, of_ref, ob_ref,
                    buf_ref, sems, *, n_tokens):
    i = pl.program_id(0)
    t = COMBINE_T

    def issue(block, slot):
        def body(r, carry):
            tok = block * t + r
            _row_copy(y_hbm, dest_ref[tok], buf_ref.at[slot, 0], r, sems.at[slot]).start(priority=0)
            _row_copy(y_hbm, dest_ref[n_tokens + tok], buf_ref.at[slot, 1], r, sems.at[slot]).start(priority=1)
            return carry

        lax.fori_loop(0, t, body, 0, unroll=ISSUE_UNROLL)

    slot = i % 2

    @pl.when(i == 0)
    def _():
        issue(0, 0)

    @pl.when(i + 1 < pl.num_programs(0))
    def _():
        issue(i + 1, 1 - slot)

    for k in range(TOP_K):
        pltpu.make_async_copy(y_hbm.at[pl.ds(0, t)], buf_ref.at[slot, k], sems.at[slot]).wait()
    ff = buf_ref[slot, 0] * wt_ref[:, 0:1] + buf_ref[slot, 1] * wt_ref[:, 1:2]
    y = _layer_norm(DN_ALPHA * h_ref[...] + ff, g_ref[...], b_ref[...])
    of_ref[...] = y
    ob_ref[...] = y.astype(BF16)


def moe_combine_ln(yb, dest, wts_t, h, ln_g, ln_b, layer, which):
    n, d = h.shape
    t = COMBINE_T
    row = pl.BlockSpec((t, d), lambda i, ds: (i, 0))
    par = pl.BlockSpec((None, None, 1, d), lambda i, ds: (layer, which, 0, 0))
    kernel = functools.partial(_combine_kernel, n_tokens=n)
    return pl.pallas_call(
        kernel,
        out_shape=(jax.ShapeDtypeStruct((n, d), F32), jax.ShapeDtypeStruct((n, d), BF16)),
        grid_spec=pltpu.PrefetchScalarGridSpec(
            num_scalar_prefetch=1,
            grid=(n // t,),
            in_specs=[
                pl.BlockSpec(memory_space=pl.ANY),
                pl.BlockSpec((t, TOP_K), lambda i, ds: (i, 0)),
                row, par, par,
            ],
            out_specs=(row, row),
            scratch_shapes=[pltpu.VMEM((2, TOP_K, t, d), F32), pltpu.SemaphoreType.DMA((2,))],
        ),
        compiler_params=_params(1, 32 << 20),
        name="moe_combine_ln",
    )(dest, yb, wts_t, h, ln_g, ln_b)


def moe_layer(h, hb, w_router, w_gate, w_up, w_down, ln_g, ln_b, layer, moe_idx):
    n = h.shape[0]
    m = n * TOP_K
    n_blocks = (m + N_EXPERTS * (EXPERT_BLOCK - 1) + EXPERT_BLOCK - 1) // EXPERT_BLOCK
    e, wts, rank, cnt = moe_router(h, w_router, moe_idx)
    counts = cnt[:, 0]
    padded = (counts + EXPERT_BLOCK - 1) // EXPERT_BLOCK * EXPERT_BLOCK
    pad_ends = jnp.cumsum(padded)
    pad_starts = pad_ends - padded
    expert_ids = jnp.arange(N_EXPERTS, dtype=jnp.int32)[:, None, None]
    dest = jnp.sum(jnp.where(e[None] == expert_ids, pad_starts[:, None, None], 0), axis=0) + rank
    block_start = jnp.arange(n_blocks, dtype=jnp.int32) * EXPERT_BLOCK
    block_e = jnp.minimum(jnp.sum(block_start[:, None] >= pad_ends[None, :], axis=1), N_EXPERTS - 1).astype(jnp.int32)
    nused = (pad_ends[-1:] // EXPERT_BLOCK).astype(jnp.int32)

    dest = dest.reshape(-1)
    pad_rows = jnp.concatenate([pad_starts + counts, pad_ends]).astype(jnp.int32)
    schedule = _group_schedule(block_e)
    xb = moe_dispatch(h, dest, pad_rows, nused, n_blocks)
    hid = moe_up(xb, w_gate, w_up, moe_idx, block_e, nused, schedule)
    yb = moe_down(hid, w_down, moe_idx, block_e, nused, schedule)
    return moe_combine_ln(yb, dest, wts.T, h, ln_g, ln_b, layer, 2)


def kernel(x, mem, rel_table, mix_w_in, gmlp_ln_g, gmlp_ln_b, gmlp_w_s, gmlp_b_s, pool_w, pool_scale, mix_w_out, diff_w_qkv, diff_lam_q1, diff_lam_k1, diff_lam_q2, diff_lam_k2, diff_subln_g, diff_w_o, xa_w_q, xa_w_kv, xa_w_o, ffn_w_gate, ffn_w_up, ffn_w_down, moe_w_router, moe_w_gate, moe_w_up, moe_w_down, ln_g, ln_b):
    batch, seq_len, d = x.shape
    mem_len = mem.shape[1]
    n = batch * seq_len
    h = x.reshape(n, d)
    hb = h
    memf = mem.reshape(batch * mem_len, d)
    ln_g4 = ln_g.reshape(DEPTH, 3, 1, d)
    ln_b4 = ln_b.reshape(DEPTH, 3, 1, d)
    lam_params = jnp.stack([diff_lam_q1, diff_lam_k1, diff_lam_q2, diff_lam_k2], axis=1)
    bias_tiles = attention_bias_tiles(rel_table)
    kv_all = matmul_all_layers(memf, xa_w_kv, tn=1024, out_dtype=BF16)

    for layer in range(DEPTH):
        i = layer // 2
        if layer % 2 == 0:
            hm = matmul(hb, mix_w_in, i, tm=1024 if hb.dtype == BF16 else 512, tn=1024, out_dtype=F32)
            mixed = gmlp_pool_mixer(hm, seq_len, gmlp_ln_g, gmlp_ln_b, gmlp_w_s, gmlp_b_s, pool_w,
                                    pool_scale, i)
            h, hb = matmul_residual_ln(mixed, mix_w_out, i, h, ln_g4, ln_b4, layer, 0)
        else:
            lam_init = 0.8 - 0.6 * math.exp(-0.3 * layer)
            qkv = matmul(hb, diff_w_qkv, i, tm=1024, tn=1024, out_dtype=BF16,
                         scaled_cols=D, scale=LOG2E * C_HEAD_DIM ** -0.5)
            att = diff_attention(qkv, bias_tiles, lam_params, diff_subln_g, i, lam_init, batch, seq_len)
            h, hb = matmul_residual_ln(att, diff_w_o, i, h, ln_g4, ln_b4, layer, 0)

        q = matmul(hb, xa_w_q, layer, tm=1024, tn=1024, out_dtype=BF16,
                   scaled_cols=D, scale=LOG2E * X_HEAD_DIM ** -0.5)
        xo = cross_attention(q, kv_all, layer, batch, seq_len, mem_len)
        h, hb = matmul_residual_ln(xo, xa_w_o, layer, h, ln_g4, ln_b4, layer, 1)

        if layer % 2 == 0:
            hid = swiglu_up(hb, ffn_w_gate, ffn_w_up, i, tm=1024, tn=512)
            ff = matmul(hid, ffn_w_down, i, tm=512, tn=512, out_dtype=F32)
            h, hb = residual_ln(h, ff, ln_g4, ln_b4, layer, 2)
        else:
            h, hb = moe_layer(h, hb, moe_w_router, moe_w_gate, moe_w_up, moe_w_down, ln_g4, ln_b4,
                              layer, i)
    return h.reshape(batch, seq_len, d)
```

```python
import functools
import math

import numpy as np
import jax
import jax.numpy as jnp
from jax import lax
from jax.experimental import pallas as pl
from jax.experimental.pallas import tpu as pltpu

F32 = jnp.float32
BF16 = jnp.bfloat16

D = 2048
DEPTH = 4
CHUNK = 64
GMLP_CHUNK = 128
A_WIDTH = D // 2
A_HEADS = 8
A_HEAD_DIM = A_WIDTH // A_HEADS
B_WIDTH = D // 2
POOL_WINDOWS = (2, 4, 8, 16)
B_GROUP = B_WIDTH // len(POOL_WINDOWS)
MIX_IN = 2 * A_WIDTH + B_WIDTH
C_HEADS = 16
C_HEAD_DIM = D // (2 * C_HEADS)
C_V_DIM = 2 * C_HEAD_DIM
REL_BUCKETS = 32
REL_MAX_DIST = 128
X_HEADS = 4
X_HEAD_DIM = D // X_HEADS
N_EXPERTS = 8
TOP_K = 2
EXPERT_BLOCK = 256
DN_ALPHA = (2 * DEPTH) ** 0.25
LN_EPS = 1e-5
NEG = -1e30

VMEM_CAP_BYTES = 56 * 1024 * 1024
CAST_ROWS = 256
ATT_TQ = 256
POOL_HALO = 16


def _params(n_axes, vmem_bytes):
    return pltpu.CompilerParams(
        dimension_semantics=("arbitrary",) * n_axes,
        vmem_limit_bytes=int(min(max(vmem_bytes, 16 * 1024 * 1024), VMEM_CAP_BYTES)),
    )


def _cast_rows(w_ref, wb_ref):
    k = w_ref.shape[0]
    rows = CAST_ROWS if k % CAST_ROWS == 0 else k

    def body(c, carry):
        r = pl.multiple_of(c * rows, rows)
        wb_ref[pl.ds(r, rows), :] = w_ref[pl.ds(r, rows), :].astype(BF16)
        return carry

    lax.fori_loop(0, k // rows, body, 0)


def _layer_norm(x, g, b):
    mu = jnp.mean(x, axis=-1, keepdims=True)
    xc = x - mu
    var = jnp.mean(xc * xc, axis=-1, keepdims=True)
    return xc * lax.rsqrt(var + LN_EPS) * g + b


def _mm_kernel(a_ref, w_ref, o_ref, wb_ref, *, scaled_tiles, scale):
    j = pl.program_id(0)

    @pl.when(pl.program_id(1) == 0)
    def _():
        _cast_rows(w_ref, wb_ref)

    a = a_ref[...].astype(BF16)
    y = jnp.dot(a, wb_ref[...], preferred_element_type=F32)
    if scaled_tiles:
        y = y * jnp.where(j < scaled_tiles, scale, 1.0)
    o_ref[...] = y.astype(o_ref.dtype)


def matmul(a, w, layer, *, tm, tn, out_dtype, scaled_cols=0, scale=1.0):
    m, k = a.shape
    n = w.shape[-1]
    assert m % tm == 0 and n % tn == 0 and w.shape[-2] == k and scaled_cols % tn == 0
    vmem = 2 * tm * k * a.dtype.itemsize + 2 * k * tn * 4 + k * tn * 2 + 2 * tm * tn * 4 + tm * tn * 4
    kernel = functools.partial(_mm_kernel, scaled_tiles=scaled_cols // tn, scale=scale)
    return pl.pallas_call(
        kernel,
        out_shape=jax.ShapeDtypeStruct((m, n), out_dtype),
        grid=(n // tn, m // tm),
        in_specs=[
            pl.BlockSpec((tm, k), lambda j, i: (i, 0)),
            pl.BlockSpec((None, k, tn), lambda j, i: (layer, 0, j)),
        ],
        out_specs=pl.BlockSpec((tm, tn), lambda j, i: (i, j)),
        scratch_shapes=[pltpu.VMEM((k, tn), BF16)],
        compiler_params=_params(2, vmem + (4 << 20)),
        name="matmul",
    )(a, w)


MMLN_GROUPS = 2


def _mm_ln_kernel(a_ref, w_ref, h_ref, g_ref, b_ref, of_ref, ob_ref, wb_ref):
    @pl.when(pl.program_id(0) == 0)
    def _():
        _cast_rows(w_ref, wb_ref)

    rows = a_ref.shape[0] // MMLN_GROUPS
    for r in range(MMLN_GROUPS):
        sl = slice(r * rows, (r + 1) * rows)
        y = jnp.dot(a_ref[sl, :], wb_ref[...], preferred_element_type=F32)
        out = _layer_norm(DN_ALPHA * h_ref[sl, :] + y, g_ref[...], b_ref[...])
        of_ref[sl, :] = out
        ob_ref[sl, :] = out.astype(BF16)


def matmul_residual_ln(a, w, w_layer, h, ln_g, ln_b, layer, which, *, tm=512):
    m, k = a.shape
    d = w.shape[-1]
    assert m % tm == 0 and h.shape == (m, d)
    row = pl.BlockSpec((tm, d), lambda i: (i, 0))
    par = pl.BlockSpec((None, None, 1, d), lambda i: (layer, which, 0, 0))
    vmem = k * d * 4 + k * d * 2 + 2 * tm * k * 2 + 4 * tm * d * 4 + 2 * tm * d * 2 + 3 * tm * d * 4
    return pl.pallas_call(
        _mm_ln_kernel,
        out_shape=(jax.ShapeDtypeStruct((m, d), F32), jax.ShapeDtypeStruct((m, d), BF16)),
        grid=(m // tm,),
        in_specs=[
            pl.BlockSpec((tm, k), lambda i: (i, 0)),
            pl.BlockSpec((None, k, d), lambda i: (w_layer, 0, 0), pipeline_mode=pl.Buffered(1)),
            row, par, par,
        ],
        out_specs=(row, row),
        scratch_shapes=[pltpu.VMEM((k, d), BF16)],
        compiler_params=_params(1, vmem + (4 << 20)),
        name="matmul_residual_ln",
    )(a, w, h, ln_g, ln_b)


def _mm_layers_kernel(a_ref, w_ref, o_ref):
    o_ref[...] = jnp.dot(a_ref[...].astype(BF16), w_ref[...].astype(BF16),
                         preferred_element_type=F32).astype(o_ref.dtype)


def matmul_all_layers(a, w, *, tn, out_dtype):
    m, k = a.shape
    layers, _, n = w.shape
    assert n % tn == 0
    return pl.pallas_call(
        _mm_layers_kernel,
        out_shape=jax.ShapeDtypeStruct((layers, m, n), out_dtype),
        grid=(layers, n // tn),
        in_specs=[
            pl.BlockSpec((m, k), lambda l, j: (0, 0)),
            pl.BlockSpec((None, k, tn), lambda l, j: (l, 0, j)),
        ],
        out_specs=pl.BlockSpec((None, m, tn), lambda l, j: (l, 0, j)),
        compiler_params=_params(2, 2 * m * k * 4 + 3 * k * tn * 4 + 3 * m * tn * 4 + (4 << 20)),
        name="matmul_all_layers",
    )(a, w)


def _swiglu_kernel(a_ref, wg_ref, wu_ref, o_ref, wgb_ref, wub_ref):
    @pl.when(pl.program_id(1) == 0)
    def _():
        _cast_rows(wg_ref, wgb_ref)
        _cast_rows(wu_ref, wub_ref)

    a = a_ref[...]
    g = jnp.dot(a, wgb_ref[...], preferred_element_type=F32)
    u = jnp.dot(a, wub_ref[...], preferred_element_type=F32)
    o_ref[...] = (jax.nn.silu(g) * u).astype(o_ref.dtype)


def swiglu_up(a, wg, wu, layer, *, tm, tn):
    m, k = a.shape
    n = wg.shape[-1]
    assert m % tm == 0 and n % tn == 0
    vmem = 2 * tm * k * 2 + 4 * k * tn * 4 + 2 * k * tn * 2 + 2 * tm * tn * 2 + 3 * tm * tn * 4
    wspec = pl.BlockSpec((None, k, tn), lambda j, i: (layer, 0, j))
    return pl.pallas_call(
        _swiglu_kernel,
        out_shape=jax.ShapeDtypeStruct((m, n), BF16),
        grid=(n // tn, m // tm),
        in_specs=[pl.BlockSpec((tm, k), lambda j, i: (i, 0)), wspec, wspec],
        out_specs=pl.BlockSpec((tm, tn), lambda j, i: (i, j)),
        scratch_shapes=[pltpu.VMEM((k, tn), BF16), pltpu.VMEM((k, tn), BF16)],
        compiler_params=_params(2, vmem + (4 << 20)),
        name="swiglu_up",
    )(a, wg, wu)


def _ln_kernel(h_ref, y_ref, g_ref, b_ref, of_ref, ob_ref):
    x = DN_ALPHA * h_ref[...] + y_ref[...]
    y = _layer_norm(x, g_ref[...], b_ref[...])
    of_ref[...] = y
    ob_ref[...] = y.astype(BF16)


def residual_ln(h, y, ln_g, ln_b, layer, which, *, tm=256):
    m, d = h.shape
    row = pl.BlockSpec((tm, d), lambda i: (i, 0))
    par = pl.BlockSpec((None, None, 1, d), lambda i: (layer, which, 0, 0))
    return pl.pallas_call(
        _ln_kernel,
        out_shape=(jax.ShapeDtypeStruct((m, d), F32), jax.ShapeDtypeStruct((m, d), BF16)),
        grid=(m // tm,),
        in_specs=[row, row, par, par],
        out_specs=(row, row),
        compiler_params=_params(1, 16 * tm * d * 4),
        name="residual_ln",
    )(h, y, ln_g, ln_b)


MIX_ROWS = 2 * GMLP_CHUNK


def _mixer_kernel(hm_ref, prev_ref, lng_ref, lnb_ref, ws_ref, bst_ref, pw_ref, ps_ref, o_ref,
                  wbd_ref, ext_ref, dbl_ref, *, blocks_per_seq):
    i = pl.program_id(0)

    @pl.when(i == 0)
    def _():
        p = lax.broadcasted_iota(jnp.int32, (GMLP_CHUNK, GMLP_CHUNK), 0)
        q = lax.broadcasted_iota(jnp.int32, (GMLP_CHUNK, GMLP_CHUNK), 1)
        allowed = (q // CHUNK) <= (p // CHUNK)
        wbd_ref[...] = jnp.zeros(wbd_ref.shape, BF16)
        for h in range(A_HEADS):
            w = jnp.where(allowed, ws_ref[h], 0.0).astype(BF16)
            wbd_ref[h, 0:GMLP_CHUNK, 0:GMLP_CHUNK] = w
            wbd_ref[h, GMLP_CHUNK:MIX_ROWS, GMLP_CHUNK:MIX_ROWS] = w

    z = jax.nn.gelu(hm_ref[:, 0:2 * A_WIDTH])
    u = z[:, 0:A_WIDTH]
    v = _layer_norm(z[:, A_WIDTH:2 * A_WIDTH], lng_ref[...], lnb_ref[...]).astype(BF16)
    for h in range(A_HEADS):
        c0 = h * A_HEAD_DIM
        sv = jnp.dot(wbd_ref[h], v[:, c0:c0 + A_HEAD_DIM], preferred_element_type=F32)
        sv = sv + bst_ref[:, h:h + 1]
        o_ref[:, c0:c0 + A_HEAD_DIM] = (u[:, c0:c0 + A_HEAD_DIM] * sv).astype(BF16)

    blk = i % blocks_per_seq
    prev = jnp.where(blk == 0, 0.0, prev_ref[...])
    ext_ref[0:POOL_HALO, :] = prev
    ext_ref[POOL_HALO:POOL_HALO + MIX_ROWS, :] = hm_ref[:, 2 * A_WIDTH:MIX_IN]
    t1 = blk * MIX_ROWS + lax.broadcasted_iota(jnp.int32, (MIX_ROWS, 1), 0) + 1
    for g, w in enumerate(POOL_WINDOWS):
        c0 = g * B_GROUP
        cols = slice(c0, c0 + B_GROUP)
        pin = ext_ref[POOL_HALO:POOL_HALO + MIX_ROWS, cols]
        src_ref, src_cols = ext_ref, cols
        k = 1
        while k < w:
            lo = 2 * k - 1
            n = POOL_HALO + MIX_ROWS - lo
            dbl_ref[lo:lo + n, :] = src_ref[lo:lo + n, src_cols] + src_ref[lo - k:lo - k + n, src_cols]
            src_ref, src_cols = dbl_ref, slice(None)
            k *= 2
        s = dbl_ref[POOL_HALO:POOL_HALO + MIX_ROWS, :]
        cnt = jnp.minimum(t1, w).astype(F32)
        pooled = s / cnt - pin
        b = jnp.dot(pooled.astype(BF16), pw_ref[g].astype(BF16), preferred_element_type=F32)
        b = b * ps_ref[:, c0:c0 + B_GROUP]
        o_ref[:, A_WIDTH + c0:A_WIDTH + c0 + B_GROUP] = b.astype(BF16)


def gmlp_pool_mixer(hm, seq_len, ln_g, ln_b, w_s, b_s, pool_w, pool_scale, layer):
    n = hm.shape[0]
    assert seq_len % MIX_ROWS == 0 and max(POOL_WINDOWS) <= POOL_HALO
    blocks_per_seq = seq_len // MIX_ROWS
    halo_per_block = MIX_ROWS // POOL_HALO
    bst = jnp.tile(b_s[layer].T, (MIX_ROWS // GMLP_CHUNK, 1))
    kernel = functools.partial(_mixer_kernel, blocks_per_seq=blocks_per_seq)
    return pl.pallas_call(
        kernel,
        out_shape=jax.ShapeDtypeStruct((n, D), BF16),
        grid=(n // MIX_ROWS,),
        in_specs=[
            pl.BlockSpec((MIX_ROWS, MIX_IN), lambda i: (i, 0)),
            pl.BlockSpec((POOL_HALO, B_WIDTH),
                         lambda i: (jnp.maximum(i * halo_per_block - 1, 0), 2 * A_WIDTH // B_WIDTH)),
            pl.BlockSpec((None, 1, A_WIDTH), lambda i: (layer, 0, 0)),
            pl.BlockSpec((None, 1, A_WIDTH), lambda i: (layer, 0, 0)),
            pl.BlockSpec((None, A_HEADS, GMLP_CHUNK, GMLP_CHUNK), lambda i: (layer, 0, 0, 0)),
            pl.BlockSpec((MIX_ROWS, A_HEADS), lambda i: (0, 0)),
            pl.BlockSpec((None, len(POOL_WINDOWS), B_GROUP, B_GROUP), lambda i: (layer, 0, 0, 0)),
            pl.BlockSpec((None, 1, B_WIDTH), lambda i: (layer, 0, 0)),
        ],
        out_specs=pl.BlockSpec((MIX_ROWS, D), lambda i: (i, 0)),
        scratch_shapes=[
            pltpu.VMEM((A_HEADS, MIX_ROWS, MIX_ROWS), BF16),
            pltpu.VMEM((POOL_HALO + MIX_ROWS, B_WIDTH), F32),
            pltpu.VMEM((POOL_HALO + MIX_ROWS, B_GROUP), F32),
        ],
        compiler_params=_params(1, 40 << 20),
        name="gmlp_pool_mixer",
    )(hm, hm, ln_g.reshape(-1, 1, A_WIDTH), ln_b.reshape(-1, 1, A_WIDTH), w_s, bst, pool_w,
      pool_scale.reshape(-1, 1, B_WIDTH))


def _t5_bucket_np(rel):
    half = REL_BUCKETS // 2
    max_exact = half // 2
    rel = np.asarray(rel, dtype=np.int64)
    n = np.abs(rel)
    steps = half - max_exact
    ratio = REL_MAX_DIST // max_exact
    large = np.zeros_like(n)
    nn = n.astype(object) ** steps
    for j in range(1, steps + 1):
        large = large + (nn >= (max_exact ** steps) * (ratio ** j)).astype(np.int64)
    large = np.minimum(max_exact + large, half - 1)
    return np.where(rel > 0, half, 0) + np.where(n < max_exact, n, large)


def _bias_bucket_tiles():
    qi = np.arange(ATT_TQ)[:, None]
    kj = np.arange(ATT_TQ)[None, :]
    diag = _t5_bucket_np(kj - qi)
    diag = np.where((kj // CHUNK) <= (qi // CHUNK), diag, -1)
    prev = _t5_bucket_np(kj - ATT_TQ - qi)
    return np.stack([diag, prev], 0).astype(np.int32)


FAR_BUCKET = REL_BUCKETS // 2 - 1
LOG2E = math.log2(math.e)


def _bias_kernel(rel_ref, bk_ref, o_ref):
    h = pl.program_id(0)
    bk = bk_ref[...]
    acc = jnp.zeros(bk.shape, F32)
    for b in range(REL_BUCKETS):
        acc = jnp.where(bk == b, rel_ref[b, h], acc)
    o_ref[...] = jnp.where(bk < 0, NEG, (acc - rel_ref[FAR_BUCKET, h]) * LOG2E)


def attention_bias_tiles(rel_table):
    assert ATT_TQ >= REL_MAX_DIST and ATT_TQ % CHUNK == 0
    bk = jnp.asarray(_bias_bucket_tiles())
    return pl.pallas_call(
        _bias_kernel,
        out_shape=jax.ShapeDtypeStruct((C_HEADS, 2, ATT_TQ, ATT_TQ), F32),
        grid=(C_HEADS,),
        in_specs=[
            pl.BlockSpec(memory_space=pltpu.SMEM),
            pl.BlockSpec((2, ATT_TQ, ATT_TQ), lambda h: (0, 0, 0)),
        ],
        out_specs=pl.BlockSpec((None, 2, ATT_TQ, ATT_TQ), lambda h: (h, 0, 0, 0)),
        compiler_params=_params(1, 16 << 20),
        name="attention_bias_tiles",
    )(rel_table, bk)


def _diff_attn_kernel(q_ref, k_ref, v_ref, bias_ref, lam_ref, g_ref, o_ref, vx_ref, *, lam_init, seq_len):
    tq = ATT_TQ
    nt = (((1,), (1,)), ((), ()))
    lp = lam_ref[...]
    lam = (jnp.exp(jnp.sum(lp[0:1] * lp[1:2], axis=1, keepdims=True))
           - jnp.exp(jnp.sum(lp[2:3] * lp[3:4], axis=1, keepdims=True)) + lam_init)
    lane = lax.broadcasted_iota(jnp.int32, (tq, C_V_DIM), 1)
    vx_ref[:, 0:C_V_DIM] = v_ref[...]
    vx_ref[:, C_V_DIM:2 * C_V_DIM] = jnp.ones((seq_len, C_V_DIM), BF16)
    for qi in range(seq_len // tq):
        q = q_ref[qi * tq:(qi + 1) * tq, :]
        zero = jnp.zeros_like(q)
        parts = []
        if qi >= 2:
            parts.append((0, (qi - 1) * tq, None))
        if qi >= 1:
            parts.append(((qi - 1) * tq, qi * tq, 1))
        parts.append((qi * tq, (qi + 1) * tq, 0))
        outs = []
        for half in range(2):
            keep = (lane < C_HEAD_DIM) if half == 0 else (lane >= C_HEAD_DIM)
            qh = jnp.where(keep, q, zero)
            scores = []
            m = None
            for a, b, bi in parts:
                s = lax.dot_general(qh, k_ref[a:b, :], nt, preferred_element_type=F32)
                if bi is not None:
                    s = s + bias_ref[bi]
                scores.append(s)
                sm = jnp.max(s, axis=1, keepdims=True)
                m = sm if m is None else jnp.maximum(m, sm)
            acc = None
            for s, (a, b, _) in zip(scores, parts):
                p = jnp.exp2(s - m).astype(BF16)
                pv = jnp.dot(p, vx_ref[a:b, :], preferred_element_type=F32)
                acc = pv if acc is None else acc + pv
            outs.append(acc[:, 0:C_V_DIM] / acc[:, C_V_DIM:C_V_DIM + 1])
        o = outs[0] - lam * outs[1]
        y = o * lax.rsqrt(jnp.mean(o * o, axis=-1, keepdims=True) + LN_EPS) * g_ref[...]
        o_ref[qi * tq:(qi + 1) * tq, :] = (y * (1.0 - lam_init)).astype(BF16)


def diff_attention(qkv, bias_tiles, lam_params, subln_g, layer, lam_init, batch, seq_len):
    n = qkv.shape[0]
    tq = ATT_TQ
    assert seq_len % tq == 0
    kernel = functools.partial(_diff_attn_kernel, lam_init=lam_init, seq_len=seq_len)
    seq = lambda off: pl.BlockSpec((seq_len, C_V_DIM), lambda b, h: (b, off + h))
    return pl.pallas_call(
        kernel,
        out_shape=jax.ShapeDtypeStruct((n, D), BF16),
        grid=(batch, C_HEADS),
        in_specs=[
            seq(0), seq(C_HEADS), seq(2 * C_HEADS),
            pl.BlockSpec((None, 2, tq, tq), lambda b, h: (h, 0, 0, 0)),
            pl.BlockSpec((None, 4, C_HEAD_DIM), lambda b, h: (layer, 0, 0)),
            pl.BlockSpec((None, 1, C_V_DIM), lambda b, h: (layer, 0, 0)),
        ],
        out_specs=seq(0),
        scratch_shapes=[pltpu.VMEM((seq_len, 2 * C_V_DIM), BF16)],
        compiler_params=_params(2, 40 << 20),
        name="diff_attention",
    )(qkv, qkv, qkv, bias_tiles, lam_params, subln_g.reshape(-1, 1, C_V_DIM))


def _qk_fold_kernel(wq_ref, k_ref, o_ref, wb_ref):
    @pl.when(pl.program_id(2) == 0)
    def _():
        _cast_rows(wq_ref, wb_ref)

    qk = lax.dot_general(wb_ref[...], k_ref[...], (((1,), (1,)), ((), ())), preferred_element_type=F32)
    o_ref[...] = (qk * (LOG2E * X_HEAD_DIM ** -0.5)).astype(o_ref.dtype)


def _vo_fold_kernel(v_ref, wo_ref, o_ref, wb_ref):
    @pl.when(pl.program_id(2) == 0)
    def _():
        _cast_rows(wo_ref, wb_ref)

    o_ref[...] = jnp.dot(v_ref[...], wb_ref[...], preferred_element_type=F32).astype(o_ref.dtype)


def cross_attention_folds(kv_all, w_q, w_o, batch, mem_len):
    layers = kv_all.shape[0]
    grid = (layers, X_HEADS, batch)
    qk = pl.pallas_call(
        _qk_fold_kernel,
        out_shape=jax.ShapeDtypeStruct((layers, batch, D, X_HEADS * mem_len), BF16),
        grid=grid,
        in_specs=[
            pl.BlockSpec((None, D, X_HEAD_DIM), lambda l, h, b: (l, 0, h)),
            pl.BlockSpec((None, mem_len, X_HEAD_DIM), lambda l, h, b: (l, b, h)),
        ],
        out_specs=pl.BlockSpec((None, None, D, mem_len), lambda l, h, b: (l, b, 0, h)),
        scratch_shapes=[pltpu.VMEM((D, X_HEAD_DIM), BF16)],
        compiler_params=_params(3, 24 << 20),
        name="cross_attention_qk_fold",
    )(w_q, kv_all)
    vo = pl.pallas_call(
        _vo_fold_kernel,
        out_shape=jax.ShapeDtypeStruct((layers, batch, X_HEADS * mem_len, D), BF16),
        grid=grid,
        in_specs=[
            pl.BlockSpec((None, mem_len, X_HEAD_DIM), lambda l, h, b: (l, b, X_HEADS + h)),
            pl.BlockSpec((None, X_HEAD_DIM, D), lambda l, h, b: (l, h, 0)),
        ],
        out_specs=pl.BlockSpec((None, None, mem_len, D), lambda l, h, b: (l, b, h, 0)),
        scratch_shapes=[pltpu.VMEM((X_HEAD_DIM, D), BF16)],
        compiler_params=_params(3, 24 << 20),
        name="cross_attention_vo_fold",
    )(kv_all, w_o)
    return qk, vo


XATTN_GROUPS = 2


def _xattn_ln_kernel(hb_ref, qk_ref, vo_ref, h_ref, g_ref, b_ref, of_ref, ob_ref, *, mem_len):
    rows = hb_ref.shape[0] // XATTN_GROUPS
    for r in range(XATTN_GROUPS):
        sl = slice(r * rows, (r + 1) * rows)
        s = jnp.dot(hb_ref[sl, :], qk_ref[...], preferred_element_type=F32)
        probs = []
        for hd in range(X_HEADS):
            sh = s[:, hd * mem_len:(hd + 1) * mem_len]
            e = jnp.exp2(sh - jnp.max(sh, axis=1, keepdims=True))
            inv = 1.0 / jnp.sum(e, axis=1, keepdims=True)
            probs.append((e * inv).astype(BF16))
        p = jnp.concatenate(probs, axis=1)
        y = jnp.dot(p, vo_ref[...], preferred_element_type=F32)
        out = _layer_norm(DN_ALPHA * h_ref[sl, :] + y, g_ref[...], b_ref[...])
        of_ref[sl, :] = out
        ob_ref[sl, :] = out.astype(BF16)


def cross_attention_residual_ln(hb, h, qk, vo, ln_g, ln_b, layer, which, seq_len, mem_len, *, tm=512):
    n, d = h.shape
    assert seq_len % tm == 0 and tm % XATTN_GROUPS == 0
    blocks_per_seq = seq_len // tm
    row = pl.BlockSpec((tm, d), lambda i: (i, 0))
    par = pl.BlockSpec((None, None, 1, d), lambda i: (layer, which, 0, 0))
    kw = X_HEADS * mem_len
    kernel = functools.partial(_xattn_ln_kernel, mem_len=mem_len)
    return pl.pallas_call(
        kernel,
        out_shape=(jax.ShapeDtypeStruct((n, d), F32), jax.ShapeDtypeStruct((n, d), BF16)),
        grid=(n // tm,),
        in_specs=[
            row,
            pl.BlockSpec((None, None, d, kw), lambda i: (layer, i // blocks_per_seq, 0, 0)),
            pl.BlockSpec((None, None, kw, d), lambda i: (layer, i // blocks_per_seq, 0, 0)),
            row, par, par,
        ],
        out_specs=(row, row),
        compiler_params=_params(1, 52 << 20),
        name="cross_attention_residual_ln",
    )(hb, qk, vo, h, ln_g, ln_b)


ROUTE_T = 256


def _router_kernel(h_ref, wrt_ref, e_ref, w_ref, rank_ref, cnt_ref, base_ref):
    i = pl.program_id(0)

    @pl.when(i == 0)
    def _():
        base_ref[...] = jnp.zeros(base_ref.shape, F32)

    t = ROUTE_T
    nt = (((1,), (1,)), ((), ()))
    hf = h_ref[...]
    wf = wrt_ref[...]
    hh = hf.astype(BF16)
    wh = wf.astype(BF16)
    hl = (hf - hh.astype(F32)).astype(BF16)
    wl = (wf - wh.astype(F32)).astype(BF16)
    logits = (lax.dot_general(wh, hh, nt, preferred_element_type=F32)
              + lax.dot_general(wh, hl, nt, preferred_element_type=F32)
              + lax.dot_general(wl, hh, nt, preferred_element_type=F32))
    eidx = lax.broadcasted_iota(jnp.int32, (N_EXPERTS, t), 0)
    m1 = jnp.max(logits, axis=0, keepdims=True)
    e1 = jnp.min(jnp.where(logits == m1, eidx, N_EXPERTS), axis=0, keepdims=True)
    oh1 = eidx == e1
    rest = jnp.where(oh1, -jnp.inf, logits)
    m2 = jnp.max(rest, axis=0, keepdims=True)
    e2 = jnp.min(jnp.where(rest == m2, eidx, N_EXPERTS), axis=0, keepdims=True)
    oh2 = eidx == e2
    ex = jnp.exp(m2 - m1)
    den = 1.0 + ex
    e_ref[0:1, :] = e1
    e_ref[1:2, :] = e2
    w_ref[0:1, :] = 1.0 / den
    w_ref[1:2, :] = ex / den

    oh = (oh1.astype(F32) + oh2.astype(F32)).astype(BF16)
    r = lax.broadcasted_iota(jnp.int32, (t, t), 0)
    c = lax.broadcasted_iota(jnp.int32, (t, t), 1)
    before = (r < c).astype(BF16)
    ones = jnp.ones((t, t), BF16)
    pos = base_ref[...] + jnp.dot(oh, before, preferred_element_type=F32)
    rank_ref[0:1, :] = jnp.sum(jnp.where(oh1, pos, 0.0), axis=0, keepdims=True).astype(jnp.int32)
    rank_ref[1:2, :] = jnp.sum(jnp.where(oh2, pos, 0.0), axis=0, keepdims=True).astype(jnp.int32)
    total = base_ref[...] + jnp.dot(oh, ones, preferred_element_type=F32)
    base_ref[...] = total
    cnt_ref[...] = total[:, 0:128].astype(jnp.int32)


def moe_router(h, w_router, layer):
    n = h.shape[0]
    t = ROUTE_T
    wrt = jnp.swapaxes(w_router, 1, 2)
    pair = pl.BlockSpec((TOP_K, t), lambda i: (0, i))
    return pl.pallas_call(
        _router_kernel,
        out_shape=(
            jax.ShapeDtypeStruct((TOP_K, n), jnp.int32),
            jax.ShapeDtypeStruct((TOP_K, n), F32),
            jax.ShapeDtypeStruct((TOP_K, n), jnp.int32),
            jax.ShapeDtypeStruct((N_EXPERTS, 128), jnp.int32),
        ),
        grid=(n // t,),
        in_specs=[
            pl.BlockSpec((t, D), lambda i: (i, 0)),
            pl.BlockSpec((None, N_EXPERTS, D), lambda i: (layer, 0, 0)),
        ],
        out_specs=(pair, pair, pair, pl.BlockSpec((N_EXPERTS, 128), lambda i: (0, 0))),
        scratch_shapes=[pltpu.VMEM((N_EXPERTS, t), F32)],
        compiler_params=_params(1, 16 << 20),
        name="moe_router",
    )(h, wrt)


def _row_copy(src_ref, src_row, dst_ref, dst_row, sem):
    return pltpu.make_async_copy(src_ref.at[pl.ds(src_row, 1)], dst_ref.at[pl.ds(dst_row, 1)], sem)


DISPATCH_T = 256
ISSUE_UNROLL = 8


def _dispatch_kernel(dest_ref, pad_ref, nused_ref, h_ref, xb_hbm, zero_ref, sem, zsem, *, n_tokens, n_blocks):
    i = pl.program_id(0)
    t = DISPATCH_T
    rows = EXPERT_BLOCK

    @pl.when(i == 0)
    def _():
        zero_ref[...] = jnp.zeros(zero_ref.shape, F32)

        def zero_row(r, carry):
            _row_copy(zero_ref, 0, xb_hbm, r, zsem).start()
            return carry

        def wait_row(r, carry):
            _row_copy(zero_ref, 0, xb_hbm, 0, zsem).wait()
            return carry

        def block_copy(blk):
            return pltpu.make_async_copy(zero_ref, xb_hbm.at[pl.ds(pl.multiple_of(blk * rows, rows), rows)], zsem)

        def zero_block(blk, carry):
            block_copy(blk).start()
            return carry

        def wait_block(blk, carry):
            block_copy(0).wait()
            return carry

        for e in range(N_EXPERTS):
            lax.fori_loop(pad_ref[e], pad_ref[N_EXPERTS + e], zero_row, 0)
        lax.fori_loop(nused_ref[0], n_blocks, zero_block, 0)
        for e in range(N_EXPERTS):
            lax.fori_loop(pad_ref[e], pad_ref[N_EXPERTS + e], wait_row, 0)
        lax.fori_loop(nused_ref[0], n_blocks, wait_block, 0)

    def start(r, carry):
        tok = i * t + r
        _row_copy(h_ref, r, xb_hbm, dest_ref[tok], sem).start(priority=0)
        _row_copy(h_ref, r, xb_hbm, dest_ref[n_tokens + tok], sem).start(priority=1)
        return carry

    lax.fori_loop(0, t, start, 0, unroll=ISSUE_UNROLL)
    for _ in range(TOP_K):
        pltpu.make_async_copy(h_ref, xb_hbm.at[pl.ds(0, t)], sem).wait()


def moe_dispatch(h, dest, pad_rows, nused, n_blocks):
    n, d = h.shape
    t = DISPATCH_T
    rows = n_blocks * EXPERT_BLOCK
    kernel = functools.partial(_dispatch_kernel, n_tokens=n, n_blocks=n_blocks)
    return pl.pallas_call(
        kernel,
        out_shape=jax.ShapeDtypeStruct((rows, d), F32),
        grid_spec=pltpu.PrefetchScalarGridSpec(
            num_scalar_prefetch=3,
            grid=(n // t,),
            in_specs=[pl.BlockSpec((t, d), lambda i, *_: (i, 0))],
            out_specs=pl.BlockSpec(memory_space=pl.ANY),
            scratch_shapes=[pltpu.VMEM((EXPERT_BLOCK, d), F32), pltpu.SemaphoreType.DMA(()),
                            pltpu.SemaphoreType.DMA(())],
        ),
        compiler_params=_params(1, 16 << 20),
        name="moe_dispatch",
    )(dest, pad_rows, nused, h)


def _moe_up_kernel(be_ref, nused_ref, first_ref, nxte_ref, last_ref, x_ref, wg_hbm, wu_hbm, o_ref,
                   wgb_ref, wub_ref, sg_ref, su_ref, sems, *, layer, tn, n_col_tiles):
    j = pl.program_id(0)
    i = pl.program_id(1)

    def fetch(e, jj):
        cols = pl.ds(pl.multiple_of(jj * tn, 128), tn)
        return (pltpu.make_async_copy(wg_hbm.at[layer, e, :, cols], sg_ref, sems.at[0]),
                pltpu.make_async_copy(wu_hbm.at[layer, e, :, cols], su_ref, sems.at[1]))

    @pl.when(first_ref[i] == 1)
    def _():
        @pl.when(jnp.logical_and(j == 0, i == 0))
        def _():
            for c in fetch(be_ref[0], 0):
                c.start()

        for c in fetch(be_ref[i], j):
            c.wait()
        _cast_rows(sg_ref, wgb_ref)
        _cast_rows(su_ref, wub_ref)
        nj = j + last_ref[i]

        @pl.when(nj < n_col_tiles)
        def _():
            for c in fetch(nxte_ref[i], nj):
                c.start()

    @pl.when(i < nused_ref[0])
    def _():
        x = x_ref[...].astype(BF16)
        g = jnp.dot(x, wgb_ref[...], preferred_element_type=F32)
        u = jnp.dot(x, wub_ref[...], preferred_element_type=F32)
        o_ref[...] = (jax.nn.silu(g) * u).astype(o_ref.dtype)

    @pl.when(i >= nused_ref[0])
    def _():
        o_ref[...] = jnp.zeros(o_ref.shape, o_ref.dtype)


def _group_schedule(block_e):
    nb = block_e.shape[0]
    idx = jnp.arange(nb, dtype=jnp.int32)
    first = jnp.concatenate([jnp.ones((1,), bool), block_e[1:] != block_e[:-1]])
    later_first = jnp.logical_and(first[None, :], idx[None, :] > idx[:, None])
    nxt_idx = jnp.min(jnp.where(later_first, idx[None, :], nb), axis=1)
    last = nxt_idx >= nb
    nxt_idx = jnp.where(last, 0, nxt_idx)
    nxt_e = jnp.sum(jnp.where(idx[None, :] == nxt_idx[:, None], block_e[None, :], 0), axis=1)
    return first.astype(jnp.int32), nxt_e.astype(jnp.int32), last.astype(jnp.int32)


def moe_up(xb, w_gate, w_up, layer, block_e, nused, schedule, *, tn=1408):
    p, k = xb.shape
    rows = EXPERT_BLOCK
    f = w_gate.shape[-1]
    assert f % tn == 0 and tn % 128 == 0
    first, nxt_e, last = schedule
    kernel = functools.partial(_moe_up_kernel, layer=layer, tn=tn, n_col_tiles=f // tn)
    hbm = pl.BlockSpec(memory_space=pl.ANY)
    return pl.pallas_call(
        kernel,
        out_shape=jax.ShapeDtypeStruct((p, f), BF16),
        grid_spec=pltpu.PrefetchScalarGridSpec(
            num_scalar_prefetch=5,
            grid=(f // tn, p // rows),
            in_specs=[pl.BlockSpec((rows, k), lambda j, i, *_: (i, 0)), hbm, hbm],
            out_specs=pl.BlockSpec((rows, tn), lambda j, i, *_: (i, j)),
            scratch_shapes=[pltpu.VMEM((k, tn), BF16), pltpu.VMEM((k, tn), BF16),
                            pltpu.VMEM((k, tn), F32), pltpu.VMEM((k, tn), F32),
                            pltpu.SemaphoreType.DMA((2,))],
        ),
        compiler_params=_params(2, 52 << 20),
        name="moe_up",
    )(block_e, nused, first, nxt_e, last, xb, w_gate, w_up)


def _moe_down_kernel(be_ref, nused_ref, first_ref, nxte_ref, last_ref, x_ref, w_hbm, o_ref,
                     wb_ref, st_ref, sem, *, layer):
    i = pl.program_id(0)

    def fetch(e):
        return pltpu.make_async_copy(w_hbm.at[layer, e], st_ref, sem)

    @pl.when(first_ref[i] == 1)
    def _():
        @pl.when(i == 0)
        def _():
            fetch(be_ref[0]).start()

        fetch(be_ref[i]).wait()
        _cast_rows(st_ref, wb_ref)

        @pl.when(last_ref[i] == 0)
        def _():
            fetch(nxte_ref[i]).start()

    @pl.when(i < nused_ref[0])
    def _():
        o_ref[...] = jnp.dot(x_ref[...], wb_ref[...], preferred_element_type=F32)

    @pl.when(i >= nused_ref[0])
    def _():
        o_ref[...] = jnp.zeros(o_ref.shape, o_ref.dtype)


def moe_down(hb, w_down, layer, block_e, nused, schedule):
    p, k = hb.shape
    rows = EXPERT_BLOCK
    n = w_down.shape[-1]
    first, nxt_e, last = schedule
    kernel = functools.partial(_moe_down_kernel, layer=layer)
    return pl.pallas_call(
        kernel,
        out_shape=jax.ShapeDtypeStruct((p, n), F32),
        grid_spec=pltpu.PrefetchScalarGridSpec(
            num_scalar_prefetch=5,
            grid=(p // rows,),
            in_specs=[pl.BlockSpec((rows, k), lambda i, *_: (i, 0)), pl.BlockSpec(memory_space=pl.ANY)],
            out_specs=pl.BlockSpec((rows, n), lambda i, *_: (i, 0)),
            scratch_shapes=[pltpu.VMEM((k, n), BF16), pltpu.VMEM((k, n), F32), pltpu.SemaphoreType.DMA(())],
        ),
        compiler_params=_params(1, 50 << 20),
        name="moe_down",
    )(block_e, nused, first, nxt_e, last, hb, w_down)


COMBINE_T = 256


def _combine_kernel(dest_ref, y_hbm, wt_ref, h_ref, g_ref, b_ref, of_ref, ob_ref,
                    buf_ref, sems, *, n_tokens):
    i = pl.program_id(0)
    t = COMBINE_T

    def issue(block, slot):
        def body(r, carry):
            tok = block * t + r
            _row_copy(y_hbm, dest_ref[tok], buf_ref.at[slot, 0], r, sems.at[slot]).start(priority=0)
            _row_copy(y_hbm, dest_ref[n_tokens + tok], buf_ref.at[slot, 1], r, sems.at[slot]).start(priority=1)
            return carry

        lax.fori_loop(0, t, body, 0, unroll=ISSUE_UNROLL)

    slot = i % 2

    @pl.when(i == 0)
    def _():
        issue(0, 0)

    @pl.when(i + 1 < pl.num_programs(0))
    def _():
        issue(i + 1, 1 - slot)

    for k in range(TOP_K):
        pltpu.make_async_copy(y_hbm.at[pl.ds(0, t)], buf_ref.at[slot, k], sems.at[slot]).wait()
    ff = buf_ref[slot, 0] * wt_ref[:, 0:1] + buf_ref[slot, 1] * wt_ref[:, 1:2]
    y = _layer_norm(DN_ALPHA * h_ref[...] + ff, g_ref[...], b_ref[...])
    of_ref[...] = y
    ob_ref[...] = y.astype(BF16)


def moe_combine_ln(yb, dest, wts_t, h, ln_g, ln_b, layer, which):
    n, d = h.shape
    t = COMBINE_T
    row = pl.BlockSpec((t, d), lambda i, ds: (i, 0))
    par = pl.BlockSpec((None, None, 1, d), lambda i, ds: (layer, which, 0, 0))
    kernel = functools.partial(_combine_kernel, n_tokens=n)
    return pl.pallas_call(
        kernel,
        out_shape=(jax.ShapeDtypeStruct((n, d), F32), jax.ShapeDtypeStruct((n, d), BF16)),
        grid_spec=pltpu.PrefetchScalarGridSpec(
            num_scalar_prefetch=1,
            grid=(n // t,),
            in_specs=[
                pl.BlockSpec(memory_space=pl.ANY),
                pl.BlockSpec((t, TOP_K), lambda i, ds: (i, 0)),
                row, par, par,
            ],
            out_specs=(row, row),
            scratch_shapes=[pltpu.VMEM((2, TOP_K, t, d), F32), pltpu.SemaphoreType.DMA((2,))],
        ),
        compiler_params=_params(1, 32 << 20),
        name="moe_combine_ln",
    )(dest, yb, wts_t, h, ln_g, ln_b)


def moe_layer(h, hb, w_router, w_gate, w_up, w_down, ln_g, ln_b, layer, moe_idx):
    n = h.shape[0]
    m = n * TOP_K
    n_blocks = (m + N_EXPERTS * (EXPERT_BLOCK - 1) + EXPERT_BLOCK - 1) // EXPERT_BLOCK
    e, wts, rank, cnt = moe_router(h, w_router, moe_idx)
    counts = cnt[:, 0]
    padded = (counts + EXPERT_BLOCK - 1) // EXPERT_BLOCK * EXPERT_BLOCK
    pad_ends = jnp.cumsum(padded)
    pad_starts = pad_ends - padded
    expert_ids = jnp.arange(N_EXPERTS, dtype=jnp.int32)[:, None, None]
    dest = jnp.sum(jnp.where(e[None] == expert_ids, pad_starts[:, None, None], 0), axis=0) + rank
    block_start = jnp.arange(n_blocks, dtype=jnp.int32) * EXPERT_BLOCK
    block_e = jnp.minimum(jnp.sum(block_start[:, None] >= pad_ends[None, :], axis=1), N_EXPERTS - 1).astype(jnp.int32)
    nused = (pad_ends[-1:] // EXPERT_BLOCK).astype(jnp.int32)

    dest = dest.reshape(-1)
    pad_rows = jnp.concatenate([pad_starts + counts, pad_ends]).astype(jnp.int32)
    schedule = _group_schedule(block_e)
    xb = moe_dispatch(h, dest, pad_rows, nused, n_blocks)
    hid = moe_up(xb, w_gate, w_up, moe_idx, block_e, nused, schedule)
    yb = moe_down(hid, w_down, moe_idx, block_e, nused, schedule)
    return moe_combine_ln(yb, dest, wts.T, h, ln_g, ln_b, layer, 2)


def kernel(x, mem, rel_table, mix_w_in, gmlp_ln_g, gmlp_ln_b, gmlp_w_s, gmlp_b_s, pool_w, pool_scale, mix_w_out, diff_w_qkv, diff_lam_q1, diff_lam_k1, diff_lam_q2, diff_lam_k2, diff_subln_g, diff_w_o, xa_w_q, xa_w_kv, xa_w_o, ffn_w_gate, ffn_w_up, ffn_w_down, moe_w_router, moe_w_gate, moe_w_up, moe_w_down, ln_g, ln_b):
    batch, seq_len, d = x.shape
    mem_len = mem.shape[1]
    n = batch * seq_len
    h = x.reshape(n, d)
    hb = h
    memf = mem.reshape(batch * mem_len, d)
    ln_g4 = ln_g.reshape(DEPTH, 3, 1, d)
    ln_b4 = ln_b.reshape(DEPTH, 3, 1, d)
    lam_params = jnp.stack([diff_lam_q1, diff_lam_k1, diff_lam_q2, diff_lam_k2], axis=1)
    bias_tiles = attention_bias_tiles(rel_table)
    kv_all = matmul_all_layers(memf, xa_w_kv, tn=1024, out_dtype=BF16)
    xa_qk, xa_vo = cross_attention_folds(kv_all, xa_w_q, xa_w_o, batch, mem_len)

    for layer in range(DEPTH):
        i = layer // 2
        if layer % 2 == 0:
            hm = matmul(hb, mix_w_in, i, tm=1024 if hb.dtype == BF16 else 512, tn=1024, out_dtype=F32)
            mixed = gmlp_pool_mixer(hm, seq_len, gmlp_ln_g, gmlp_ln_b, gmlp_w_s, gmlp_b_s, pool_w,
                                    pool_scale, i)
            h, hb = matmul_residual_ln(mixed, mix_w_out, i, h, ln_g4, ln_b4, layer, 0)
        else:
            lam_init = 0.8 - 0.6 * math.exp(-0.3 * layer)
            qkv = matmul(hb, diff_w_qkv, i, tm=1024, tn=1024, out_dtype=BF16,
                         scaled_cols=D, scale=LOG2E * C_HEAD_DIM ** -0.5)
            att = diff_attention(qkv, bias_tiles, lam_params, diff_subln_g, i, lam_init, batch, seq_len)
            h, hb = matmul_residual_ln(att, diff_w_o, i, h, ln_g4, ln_b4, layer, 0)

        h, hb = cross_attention_residual_ln(hb, h, xa_qk, xa_vo, ln_g4, ln_b4, layer, 1, seq_len, mem_len)

        if layer % 2 == 0:
            hid = swiglu_up(hb, ffn_w_gate, ffn_w_up, i, tm=1024, tn=512)
            ff = matmul(hid, ffn_w_down, i, tm=512, tn=512, out_dtype=F32)
            h, hb = residual_ln(h, ff, ln_g4, ln_b4, layer, 2)
        else:
            h, hb = moe_layer(h, hb, moe_w_router, moe_w_gate, moe_w_up, moe_w_down, ln_g4, ln_b4,
                              layer, i)
    return h.reshape(batch, seq_len, d)
```

```python
import functools
import math

import numpy as np
import jax
import jax.numpy as jnp
from jax import lax
from jax.experimental import pallas as pl
from jax.experimental.pallas import tpu as pltpu

F32 = jnp.float32
BF16 = jnp.bfloat16

D = 2048
DEPTH = 4
CHUNK = 64
GMLP_CHUNK = 128
A_WIDTH = D // 2
A_HEADS = 8
A_HEAD_DIM = A_WIDTH // A_HEADS
B_WIDTH = D // 2
POOL_WINDOWS = (2, 4, 8, 16)
B_GROUP = B_WIDTH // len(POOL_WINDOWS)
MIX_IN = 2 * A_WIDTH + B_WIDTH
C_HEADS = 16
C_HEAD_DIM = D // (2 * C_HEADS)
C_V_DIM = 2 * C_HEAD_DIM
REL_BUCKETS = 32
REL_MAX_DIST = 128
X_HEADS = 4
X_HEAD_DIM = D // X_HEADS
N_EXPERTS = 8
TOP_K = 2
EXPERT_BLOCK = 256
DN_ALPHA = (2 * DEPTH) ** 0.25
LN_EPS = 1e-5
NEG = -1e30

VMEM_CAP_BYTES = 56 * 1024 * 1024
CAST_ROWS = 256
ATT_TQ = 256
POOL_HALO = 16


def _params(n_axes, vmem_bytes):
    return pltpu.CompilerParams(
        dimension_semantics=("arbitrary",) * n_axes,
        vmem_limit_bytes=int(min(max(vmem_bytes, 16 * 1024 * 1024), VMEM_CAP_BYTES)),
    )


def _cast_rows(w_ref, wb_ref):
    k = w_ref.shape[0]
    rows = CAST_ROWS if k % CAST_ROWS == 0 else k

    def body(c, carry):
        r = pl.multiple_of(c * rows, rows)
        wb_ref[pl.ds(r, rows), :] = w_ref[pl.ds(r, rows), :].astype(BF16)
        return carry

    lax.fori_loop(0, k // rows, body, 0)


def _layer_norm(x, g, b):
    mu = jnp.mean(x, axis=-1, keepdims=True)
    xc = x - mu
    var = jnp.mean(xc * xc, axis=-1, keepdims=True)
    return xc * lax.rsqrt(var + LN_EPS) * g + b


def _mm_kernel(a_ref, w_ref, o_ref, wb_ref, *, scaled_tiles, scale):
    j = pl.program_id(0)

    @pl.when(pl.program_id(1) == 0)
    def _():
        _cast_rows(w_ref, wb_ref)

    a = a_ref[...].astype(BF16)
    y = jnp.dot(a, wb_ref[...], preferred_element_type=F32)
    if scaled_tiles:
        y = y * jnp.where(j < scaled_tiles, scale, 1.0)
    o_ref[...] = y.astype(o_ref.dtype)


def matmul(a, w, layer, *, tm, tn, out_dtype, scaled_cols=0, scale=1.0):
    m, k = a.shape
    n = w.shape[-1]
    assert m % tm == 0 and n % tn == 0 and w.shape[-2] == k and scaled_cols % tn == 0
    vmem = 2 * tm * k * a.dtype.itemsize + 2 * k * tn * 4 + k * tn * 2 + 2 * tm * tn * 4 + tm * tn * 4
    kernel = functools.partial(_mm_kernel, scaled_tiles=scaled_cols // tn, scale=scale)
    return pl.pallas_call(
        kernel,
        out_shape=jax.ShapeDtypeStruct((m, n), out_dtype),
        grid=(n // tn, m // tm),
        in_specs=[
            pl.BlockSpec((tm, k), lambda j, i: (i, 0)),
            pl.BlockSpec((None, k, tn), lambda j, i: (layer, 0, j)),
        ],
        out_specs=pl.BlockSpec((tm, tn), lambda j, i: (i, j)),
        scratch_shapes=[pltpu.VMEM((k, tn), BF16)],
        compiler_params=_params(2, vmem + (4 << 20)),
        name="matmul",
    )(a, w)


MMLN_GROUPS = 2


def _mm_ln_kernel(a_ref, w_ref, h_ref, g_ref, b_ref, of_ref, ob_ref, wb_ref):
    @pl.when(pl.program_id(0) == 0)
    def _():
        _cast_rows(w_ref, wb_ref)

    rows = a_ref.shape[0] // MMLN_GROUPS
    for r in range(MMLN_GROUPS):
        sl = slice(r * rows, (r + 1) * rows)
        y = jnp.dot(a_ref[sl, :], wb_ref[...], preferred_element_type=F32)
        out = _layer_norm(DN_ALPHA * h_ref[sl, :] + y, g_ref[...], b_ref[...])
        of_ref[sl, :] = out
        ob_ref[sl, :] = out.astype(BF16)


def matmul_residual_ln(a, w, w_layer, h, ln_g, ln_b, layer, which, *, tm=512):
    m, k = a.shape
    d = w.shape[-1]
    assert m % tm == 0 and h.shape == (m, d)
    row = pl.BlockSpec((tm, d), lambda i: (i, 0))
    par = pl.BlockSpec((None, None, 1, d), lambda i: (layer, which, 0, 0))
    vmem = k * d * 4 + k * d * 2 + 2 * tm * k * 2 + 4 * tm * d * 4 + 2 * tm * d * 2 + 3 * tm * d * 4
    return pl.pallas_call(
        _mm_ln_kernel,
        out_shape=(jax.ShapeDtypeStruct((m, d), F32), jax.ShapeDtypeStruct((m, d), BF16)),
        grid=(m // tm,),
        in_specs=[
            pl.BlockSpec((tm, k), lambda i: (i, 0)),
            pl.BlockSpec((None, k, d), lambda i: (w_layer, 0, 0), pipeline_mode=pl.Buffered(1)),
            row, par, par,
        ],
        out_specs=(row, row),
        scratch_shapes=[pltpu.VMEM((k, d), BF16)],
        compiler_params=_params(1, vmem + (4 << 20)),
        name="matmul_residual_ln",
    )(a, w, h, ln_g, ln_b)


KACC_GROUP_ROWS = 256


def _mm_kacc_ln_kernel(a_ref, w_ref, h_ref, g_ref, b_ref, of_ref, ob_ref, acc_ref):
    k = pl.program_id(1)

    @pl.when(jnp.logical_and(pl.program_id(0) == 0, k == 0))
    def _():
        acc_ref[...] = jnp.zeros(acc_ref.shape, F32)

    part = jnp.dot(a_ref[...], w_ref[...].astype(BF16), preferred_element_type=F32)
    acc_ref[...] = jnp.where(k > 0, acc_ref[...], 0.0) + part

    @pl.when(k == pl.num_programs(1) - 1)
    def _():
        for r in range(0, acc_ref.shape[0], KACC_GROUP_ROWS):
            sl = slice(r, r + KACC_GROUP_ROWS)
            out = _layer_norm(DN_ALPHA * h_ref[sl, :] + acc_ref[sl, :], g_ref[...], b_ref[...])
            of_ref[sl, :] = out
            ob_ref[sl, :] = out.astype(BF16)


def matmul_kacc_residual_ln(a, w, w_layer, h, ln_g, ln_b, layer, which, *, tm, tk, single_buffer_rows):
    m, kk = a.shape
    d = w.shape[-1]
    assert m % tm == 0 and kk % tk == 0 and tm % KACC_GROUP_ROWS == 0
    mode = dict(pipeline_mode=pl.Buffered(1)) if single_buffer_rows else {}
    row = pl.BlockSpec((tm, d), lambda i, k: (i, 0), **mode)
    par = pl.BlockSpec((None, None, 1, d), lambda i, k: (layer, which, 0, 0))
    return pl.pallas_call(
        _mm_kacc_ln_kernel,
        out_shape=(jax.ShapeDtypeStruct((m, d), F32), jax.ShapeDtypeStruct((m, d), BF16)),
        grid=(m // tm, kk // tk),
        in_specs=[
            pl.BlockSpec((tm, tk), lambda i, k: (i, k)),
            pl.BlockSpec((None, tk, d), lambda i, k: (w_layer, k, 0)),
            row, par, par,
        ],
        out_specs=(row, row),
        scratch_shapes=[pltpu.VMEM((tm, d), F32)],
        compiler_params=_params(2, VMEM_CAP_BYTES),
        name="matmul_kacc_residual_ln",
    )(a, w, h, ln_g, ln_b)


def _mm_layers_kernel(a_ref, w_ref, o_ref):
    o_ref[...] = jnp.dot(a_ref[...].astype(BF16), w_ref[...].astype(BF16),
                         preferred_element_type=F32).astype(o_ref.dtype)


def matmul_all_layers(a, w, *, tn, out_dtype):
    m, k = a.shape
    layers, _, n = w.shape
    assert n % tn == 0
    return pl.pallas_call(
        _mm_layers_kernel,
        out_shape=jax.ShapeDtypeStruct((layers, m, n), out_dtype),
        grid=(layers, n // tn),
        in_specs=[
            pl.BlockSpec((m, k), lambda l, j: (0, 0)),
            pl.BlockSpec((None, k, tn), lambda l, j: (l, 0, j)),
        ],
        out_specs=pl.BlockSpec((None, m, tn), lambda l, j: (l, 0, j)),
        compiler_params=_params(2, 2 * m * k * 4 + 3 * k * tn * 4 + 3 * m * tn * 4 + (4 << 20)),
        name="matmul_all_layers",
    )(a, w)


def _swiglu_kernel(a_ref, wg_ref, wu_ref, o_ref, wgb_ref, wub_ref):
    @pl.when(pl.program_id(1) == 0)
    def _():
        _cast_rows(wg_ref, wgb_ref)
        _cast_rows(wu_ref, wub_ref)

    a = a_ref[...]
    g = jnp.dot(a, wgb_ref[...], preferred_element_type=F32)
    u = jnp.dot(a, wub_ref[...], preferred_element_type=F32)
    o_ref[...] = (jax.nn.silu(g) * u).astype(o_ref.dtype)


def swiglu_up(a, wg, wu, layer, *, tm, tn):
    m, k = a.shape
    n = wg.shape[-1]
    assert m % tm == 0 and n % tn == 0
    vmem = 2 * tm * k * 2 + 4 * k * tn * 4 + 2 * k * tn * 2 + 2 * tm * tn * 2 + 3 * tm * tn * 4
    wspec = pl.BlockSpec((None, k, tn), lambda j, i: (layer, 0, j))
    return pl.pallas_call(
        _swiglu_kernel,
        out_shape=jax.ShapeDtypeStruct((m, n), BF16),
        grid=(n // tn, m // tm),
        in_specs=[pl.BlockSpec((tm, k), lambda j, i: (i, 0)), wspec, wspec],
        out_specs=pl.BlockSpec((tm, tn), lambda j, i: (i, j)),
        scratch_shapes=[pltpu.VMEM((k, tn), BF16), pltpu.VMEM((k, tn), BF16)],
        compiler_params=_params(2, vmem + (4 << 20)),
        name="swiglu_up",
    )(a, wg, wu)


MIX_ROWS = 2 * GMLP_CHUNK


def _mixer_kernel(hm_ref, prev_ref, lng_ref, lnb_ref, ws_ref, bst_ref, pw_ref, ps_ref, o_ref,
                  wbd_ref, ext_ref, dbl_ref, *, blocks_per_seq):
    i = pl.program_id(0)

    @pl.when(i == 0)
    def _():
        p = lax.broadcasted_iota(jnp.int32, (GMLP_CHUNK, GMLP_CHUNK), 0)
        q = lax.broadcasted_iota(jnp.int32, (GMLP_CHUNK, GMLP_CHUNK), 1)
        allowed = (q // CHUNK) <= (p // CHUNK)
        wbd_ref[...] = jnp.zeros(wbd_ref.shape, BF16)
        for h in range(A_HEADS):
            w = jnp.where(allowed, ws_ref[h], 0.0).astype(BF16)
            wbd_ref[h, 0:GMLP_CHUNK, 0:GMLP_CHUNK] = w
            wbd_ref[h, GMLP_CHUNK:MIX_ROWS, GMLP_CHUNK:MIX_ROWS] = w

    z = jax.nn.gelu(hm_ref[:, 0:2 * A_WIDTH])
    u = z[:, 0:A_WIDTH]
    v = _layer_norm(z[:, A_WIDTH:2 * A_WIDTH], lng_ref[...], lnb_ref[...]).astype(BF16)
    for h in range(A_HEADS):
        c0 = h * A_HEAD_DIM
        sv = jnp.dot(wbd_ref[h], v[:, c0:c0 + A_HEAD_DIM], preferred_element_type=F32)
        sv = sv + bst_ref[:, h:h + 1]
        o_ref[:, c0:c0 + A_HEAD_DIM] = (u[:, c0:c0 + A_HEAD_DIM] * sv).astype(BF16)

    blk = i % blocks_per_seq
    prev = jnp.where(blk == 0, 0.0, prev_ref[...])
    ext_ref[0:POOL_HALO, :] = prev
    ext_ref[POOL_HALO:POOL_HALO + MIX_ROWS, :] = hm_ref[:, 2 * A_WIDTH:MIX_IN]
    t1 = blk * MIX_ROWS + lax.broadcasted_iota(jnp.int32, (MIX_ROWS, 1), 0) + 1
    for g, w in enumerate(POOL_WINDOWS):
        c0 = g * B_GROUP
        cols = slice(c0, c0 + B_GROUP)
        pin = ext_ref[POOL_HALO:POOL_HALO + MIX_ROWS, cols]
        src_ref, src_cols = ext_ref, cols
        k = 1
        while k < w:
            lo = 2 * k - 1
            n = POOL_HALO + MIX_ROWS - lo
            dbl_ref[lo:lo + n, :] = src_ref[lo:lo + n, src_cols] + src_ref[lo - k:lo - k + n, src_cols]
            src_ref, src_cols = dbl_ref, slice(None)
            k *= 2
        s = dbl_ref[POOL_HALO:POOL_HALO + MIX_ROWS, :]
        cnt = jnp.minimum(t1, w).astype(F32)
        pooled = s / cnt - pin
        b = jnp.dot(pooled.astype(BF16), pw_ref[g].astype(BF16), preferred_element_type=F32)
        b = b * ps_ref[:, c0:c0 + B_GROUP]
        o_ref[:, A_WIDTH + c0:A_WIDTH + c0 + B_GROUP] = b.astype(BF16)


def gmlp_pool_mixer(hm, seq_len, ln_g, ln_b, w_s, b_s, pool_w, pool_scale, layer):
    n = hm.shape[0]
    assert seq_len % MIX_ROWS == 0 and max(POOL_WINDOWS) <= POOL_HALO
    blocks_per_seq = seq_len // MIX_ROWS
    halo_per_block = MIX_ROWS // POOL_HALO
    bst = jnp.tile(b_s[layer].T, (MIX_ROWS // GMLP_CHUNK, 1))
    kernel = functools.partial(_mixer_kernel, blocks_per_seq=blocks_per_seq)
    return pl.pallas_call(
        kernel,
        out_shape=jax.ShapeDtypeStruct((n, D), BF16),
        grid=(n // MIX_ROWS,),
        in_specs=[
            pl.BlockSpec((MIX_ROWS, MIX_IN), lambda i: (i, 0)),
            pl.BlockSpec((POOL_HALO, B_WIDTH),
                         lambda i: (jnp.maximum(i * halo_per_block - 1, 0), 2 * A_WIDTH // B_WIDTH)),
            pl.BlockSpec((None, 1, A_WIDTH), lambda i: (layer, 0, 0)),
            pl.BlockSpec((None, 1, A_WIDTH), lambda i: (layer, 0, 0)),
            pl.BlockSpec((None, A_HEADS, GMLP_CHUNK, GMLP_CHUNK), lambda i: (layer, 0, 0, 0)),
            pl.BlockSpec((MIX_ROWS, A_HEADS), lambda i: (0, 0)),
            pl.BlockSpec((None, len(POOL_WINDOWS), B_GROUP, B_GROUP), lambda i: (layer, 0, 0, 0)),
            pl.BlockSpec((None, 1, B_WIDTH), lambda i: (layer, 0, 0)),
        ],
        out_specs=pl.BlockSpec((MIX_ROWS, D), lambda i: (i, 0)),
        scratch_shapes=[
            pltpu.VMEM((A_HEADS, MIX_ROWS, MIX_ROWS), BF16),
            pltpu.VMEM((POOL_HALO + MIX_ROWS, B_WIDTH), F32),
            pltpu.VMEM((POOL_HALO + MIX_ROWS, B_GROUP), F32),
        ],
        compiler_params=_params(1, 40 << 20),
        name="gmlp_pool_mixer",
    )(hm, hm, ln_g.reshape(-1, 1, A_WIDTH), ln_b.reshape(-1, 1, A_WIDTH), w_s, bst, pool_w,
      pool_scale.reshape(-1, 1, B_WIDTH))


def _t5_bucket_np(rel):
    half = REL_BUCKETS // 2
    max_exact = half // 2
    rel = np.asarray(rel, dtype=np.int64)
    n = np.abs(rel)
    steps = half - max_exact
    ratio = REL_MAX_DIST // max_exact
    large = np.zeros_like(n)
    nn = n.astype(object) ** steps
    for j in range(1, steps + 1):
        large = large + (nn >= (max_exact ** steps) * (ratio ** j)).astype(np.int64)
    large = np.minimum(max_exact + large, half - 1)
    return np.where(rel > 0, half, 0) + np.where(n < max_exact, n, large)


def _bias_bucket_tiles():
    qi = np.arange(ATT_TQ)[:, None]
    kj = np.arange(ATT_TQ)[None, :]
    diag = _t5_bucket_np(kj - qi)
    diag = np.where((kj // CHUNK) <= (qi // CHUNK), diag, -1)
    prev = _t5_bucket_np(kj - ATT_TQ - qi)
    return np.stack([diag, prev], 0).astype(np.int32)


FAR_BUCKET = REL_BUCKETS // 2 - 1
LOG2E = math.log2(math.e)


def _bias_kernel(rel_ref, bk_ref, o_ref):
    h = pl.program_id(0)
    bk = bk_ref[...]
    acc = jnp.zeros(bk.shape, F32)
    for b in range(REL_BUCKETS):
        acc = jnp.where(bk == b, rel_ref[b, h], acc)
    o_ref[...] = jnp.where(bk < 0, NEG, (acc - rel_ref[FAR_BUCKET, h]) * LOG2E)


def attention_bias_tiles(rel_table):
    assert ATT_TQ >= REL_MAX_DIST and ATT_TQ % CHUNK == 0
    bk = jnp.asarray(_bias_bucket_tiles())
    return pl.pallas_call(
        _bias_kernel,
        out_shape=jax.ShapeDtypeStruct((C_HEADS, 2, ATT_TQ, ATT_TQ), F32),
        grid=(C_HEADS,),
        in_specs=[
            pl.BlockSpec(memory_space=pltpu.SMEM),
            pl.BlockSpec((2, ATT_TQ, ATT_TQ), lambda h: (0, 0, 0)),
        ],
        out_specs=pl.BlockSpec((None, 2, ATT_TQ, ATT_TQ), lambda h: (h, 0, 0, 0)),
        compiler_params=_params(1, 16 << 20),
        name="attention_bias_tiles",
    )(rel_table, bk)


def _diff_attn_kernel(q_ref, k_ref, v_ref, bias_ref, lam_ref, g_ref, o_ref, vx_ref, *, lam_init, seq_len):
    tq = ATT_TQ
    nt = (((1,), (1,)), ((), ()))
    lp = lam_ref[...]
    lam = (jnp.exp(jnp.sum(lp[0:1] * lp[1:2], axis=1, keepdims=True))
           - jnp.exp(jnp.sum(lp[2:3] * lp[3:4], axis=1, keepdims=True)) + lam_init)
    lane = lax.broadcasted_iota(jnp.int32, (tq, C_V_DIM), 1)
    vx_ref[:, 0:C_V_DIM] = v_ref[...]
    vx_ref[:, C_V_DIM:2 * C_V_DIM] = jnp.ones((seq_len, C_V_DIM), BF16)
    for qi in range(seq_len // tq):
        q = q_ref[qi * tq:(qi + 1) * tq, :]
        zero = jnp.zeros_like(q)
        parts = []
        if qi >= 2:
            parts.append((0, (qi - 1) * tq, None))
        if qi >= 1:
            parts.append(((qi - 1) * tq, qi * tq, 1))
        parts.append((qi * tq, (qi + 1) * tq, 0))
        outs = []
        for half in range(2):
            keep = (lane < C_HEAD_DIM) if half == 0 else (lane >= C_HEAD_DIM)
            qh = jnp.where(keep, q, zero)
            scores = []
            m = None
            for a, b, bi in parts:
                s = lax.dot_general(qh, k_ref[a:b, :], nt, preferred_element_type=F32)
                if bi is not None:
                    s = s + bias_ref[bi]
                scores.append(s)
                sm = jnp.max(s, axis=1, keepdims=True)
                m = sm if m is None else jnp.maximum(m, sm)
            acc = None
            for s, (a, b, _) in zip(scores, parts):
                p = jnp.exp2(s - m).astype(BF16)
                pv = jnp.dot(p, vx_ref[a:b, :], preferred_element_type=F32)
                acc = pv if acc is None else acc + pv
            outs.append(acc[:, 0:C_V_DIM] / acc[:, C_V_DIM:C_V_DIM + 1])
        o = outs[0] - lam * outs[1]
        y = o * lax.rsqrt(jnp.mean(o * o, axis=-1, keepdims=True) + LN_EPS) * g_ref[...]
        o_ref[qi * tq:(qi + 1) * tq, :] = (y * (1.0 - lam_init)).astype(BF16)


def diff_attention(qkv, bias_tiles, lam_params, subln_g, layer, lam_init, batch, seq_len):
    n = qkv.shape[0]
    tq = ATT_TQ
    assert seq_len % tq == 0
    kernel = functools.partial(_diff_attn_kernel, lam_init=lam_init, seq_len=seq_len)
    seq = lambda off: pl.BlockSpec((seq_len, C_V_DIM), lambda b, h: (b, off + h))
    return pl.pallas_call(
        kernel,
        out_shape=jax.ShapeDtypeStruct((n, D), BF16),
        grid=(batch, C_HEADS),
        in_specs=[
            seq(0), seq(C_HEADS), seq(2 * C_HEADS),
            pl.BlockSpec((None, 2, tq, tq), lambda b, h: (h, 0, 0, 0)),
            pl.BlockSpec((None, 4, C_HEAD_DIM), lambda b, h: (layer, 0, 0)),
            pl.BlockSpec((None, 1, C_V_DIM), lambda b, h: (layer, 0, 0)),
        ],
        out_specs=seq(0),
        scratch_shapes=[pltpu.VMEM((seq_len, 2 * C_V_DIM), BF16)],
        compiler_params=_params(2, 40 << 20),
        name="diff_attention",
    )(qkv, qkv, qkv, bias_tiles, lam_params, subln_g.reshape(-1, 1, C_V_DIM))


def _qk_fold_kernel(wq_ref, k_ref, o_ref, *, batch, mem_len):
    wb = wq_ref[...].astype(BF16)
    for b in range(batch):
        qk = lax.dot_general(wb, k_ref[b * mem_len:(b + 1) * mem_len, :], (((1,), (1,)), ((), ())),
                             preferred_element_type=F32)
        o_ref[b] = (qk * (LOG2E * X_HEAD_DIM ** -0.5)).astype(o_ref.dtype)


def _vo_fold_kernel(v_ref, wo_ref, o_ref, *, batch, mem_len):
    wb = wo_ref[...].astype(BF16)
    for b in range(batch):
        o_ref[b] = jnp.dot(v_ref[b * mem_len:(b + 1) * mem_len, :], wb,
                           preferred_element_type=F32).astype(o_ref.dtype)


def cross_attention_folds(kv_all, w_q, w_o, batch, mem_len):
    layers = kv_all.shape[0]
    grid = (layers, X_HEADS)
    qk = pl.pallas_call(
        functools.partial(_qk_fold_kernel, batch=batch, mem_len=mem_len),
        out_shape=jax.ShapeDtypeStruct((layers, batch, D, X_HEADS * mem_len), BF16),
        grid=grid,
        in_specs=[
            pl.BlockSpec((None, D, X_HEAD_DIM), lambda l, h: (l, 0, h)),
            pl.BlockSpec((None, batch * mem_len, X_HEAD_DIM), lambda l, h: (l, 0, h)),
        ],
        out_specs=pl.BlockSpec((None, batch, D, mem_len), lambda l, h: (l, 0, 0, h)),
        compiler_params=_params(2, 32 << 20),
        name="cross_attention_qk_fold",
    )(w_q, kv_all)
    vo = pl.pallas_call(
        functools.partial(_vo_fold_kernel, batch=batch, mem_len=mem_len),
        out_shape=jax.ShapeDtypeStruct((layers, batch, X_HEADS * mem_len, D), BF16),
        grid=grid,
        in_specs=[
            pl.BlockSpec((None, batch * mem_len, X_HEAD_DIM), lambda l, h: (l, 0, X_HEADS + h)),
            pl.BlockSpec((None, X_HEAD_DIM, D), lambda l, h: (l, h, 0)),
        ],
        out_specs=pl.BlockSpec((None, batch, mem_len, D), lambda l, h: (l, 0, h, 0)),
        compiler_params=_params(2, 32 << 20),
        name="cross_attention_vo_fold",
    )(kv_all, w_o)
    return qk, vo


XATTN_GROUPS = 2


def _xattn_ln_kernel(hb_ref, qk_ref, vo_ref, h_ref, g_ref, b_ref, of_ref, ob_ref, *, mem_len):
    rows = hb_ref.shape[0] // XATTN_GROUPS
    for r in range(XATTN_GROUPS):
        sl = slice(r * rows, (r + 1) * rows)
        s = jnp.dot(hb_ref[sl, :], qk_ref[...], preferred_element_type=F32)
        probs = []
        for hd in range(X_HEADS):
            sh = s[:, hd * mem_len:(hd + 1) * mem_len]
            e = jnp.exp2(sh - jnp.max(sh, axis=1, keepdims=True))
            inv = 1.0 / jnp.sum(e, axis=1, keepdims=True)
            probs.append((e * inv).astype(BF16))
        p = jnp.concatenate(probs, axis=1)
        y = jnp.dot(p, vo_ref[...], preferred_element_type=F32)
        out = _layer_norm(DN_ALPHA * h_ref[sl, :] + y, g_ref[...], b_ref[...])
        of_ref[sl, :] = out
        ob_ref[sl, :] = out.astype(BF16)


def cross_attention_residual_ln(hb, h, qk, vo, ln_g, ln_b, layer, which, seq_len, mem_len, *, tm=512):
    n, d = h.shape
    assert seq_len % tm == 0 and tm % XATTN_GROUPS == 0
    blocks_per_seq = seq_len // tm
    row = pl.BlockSpec((tm, d), lambda i: (i, 0))
    par = pl.BlockSpec((None, None, 1, d), lambda i: (layer, which, 0, 0))
    kw = X_HEADS * mem_len
    kernel = functools.partial(_xattn_ln_kernel, mem_len=mem_len)
    return pl.pallas_call(
        kernel,
        out_shape=(jax.ShapeDtypeStruct((n, d), F32), jax.ShapeDtypeStruct((n, d), BF16)),
        grid=(n // tm,),
        in_specs=[
            row,
            pl.BlockSpec((None, None, d, kw), lambda i: (layer, i // blocks_per_seq, 0, 0)),
            pl.BlockSpec((None, None, kw, d), lambda i: (layer, i // blocks_per_seq, 0, 0)),
            row, par, par,
        ],
        out_specs=(row, row),
        compiler_params=_params(1, 52 << 20),
        name="cross_attention_residual_ln",
    )(hb, qk, vo, h, ln_g, ln_b)


ROUTE_T = 256


def _router_kernel(h_ref, wrt_ref, e_ref, w_ref, rank_ref, cnt_ref, base_ref):
    i = pl.program_id(0)

    @pl.when(i == 0)
    def _():
        base_ref[...] = jnp.zeros(base_ref.shape, F32)

    t = ROUTE_T
    nt = (((1,), (1,)), ((), ()))
    hf = h_ref[...]
    wf = wrt_ref[...]
    hh = hf.astype(BF16)
    wh = wf.astype(BF16)
    hl = (hf - hh.astype(F32)).astype(BF16)
    wl = (wf - wh.astype(F32)).astype(BF16)
    logits = (lax.dot_general(wh, hh, nt, preferred_element_type=F32)
              + lax.dot_general(wh, hl, nt, preferred_element_type=F32)
              + lax.dot_general(wl, hh, nt, preferred_element_type=F32))
    eidx = lax.broadcasted_iota(jnp.int32, (N_EXPERTS, t), 0)
    m1 = jnp.max(logits, axis=0, keepdims=True)
    e1 = jnp.min(jnp.where(logits == m1, eidx, N_EXPERTS), axis=0, keepdims=True)
    oh1 = eidx == e1
    rest = jnp.where(oh1, -jnp.inf, logits)
    m2 = jnp.max(rest, axis=0, keepdims=True)
    e2 = jnp.min(jnp.where(rest == m2, eidx, N_EXPERTS), axis=0, keepdims=True)
    oh2 = eidx == e2
    ex = jnp.exp(m2 - m1)
    den = 1.0 + ex
    e_ref[0:1, :] = e1
    e_ref[1:2, :] = e2
    w_ref[0:1, :] = 1.0 / den
    w_ref[1:2, :] = ex / den

    oh = (oh1.astype(F32) + oh2.astype(F32)).astype(BF16)
    r = lax.broadcasted_iota(jnp.int32, (t, t), 0)
    c = lax.broadcasted_iota(jnp.int32, (t, t), 1)
    before = (r < c).astype(BF16)
    ones = jnp.ones((t, t), BF16)
    pos = base_ref[...] + jnp.dot(oh, before, preferred_element_type=F32)
    rank_ref[0:1, :] = jnp.sum(jnp.where(oh1, pos, 0.0), axis=0, keepdims=True).astype(jnp.int32)
    rank_ref[1:2, :] = jnp.sum(jnp.where(oh2, pos, 0.0), axis=0, keepdims=True).astype(jnp.int32)
    total = base_ref[...] + jnp.dot(oh, ones, preferred_element_type=F32)
    base_ref[...] = total
    cnt_ref[...] = total[:, 0:128].astype(jnp.int32)


def moe_router(h, w_router, layer):
    n = h.shape[0]
    t = ROUTE_T
    wrt = jnp.swapaxes(w_router, 1, 2)
    pair = pl.BlockSpec((TOP_K, t), lambda i: (0, i))
    return pl.pallas_call(
        _router_kernel,
        out_shape=(
            jax.ShapeDtypeStruct((TOP_K, n), jnp.int32),
            jax.ShapeDtypeStruct((TOP_K, n), F32),
            jax.ShapeDtypeStruct((TOP_K, n), jnp.int32),
            jax.ShapeDtypeStruct((N_EXPERTS, 128), jnp.int32),
        ),
        grid=(n // t,),
        in_specs=[
            pl.BlockSpec((t, D), lambda i: (i, 0)),
            pl.BlockSpec((None, N_EXPERTS, D), lambda i: (layer, 0, 0)),
        ],
        out_specs=(pair, pair, pair, pl.BlockSpec((N_EXPERTS, 128), lambda i: (0, 0))),
        scratch_shapes=[pltpu.VMEM((N_EXPERTS, t), F32)],
        compiler_params=_params(1, 16 << 20),
        name="moe_router",
    )(h, wrt)


def _row_copy(src_ref, src_row, dst_ref, dst_row, sem):
    return pltpu.make_async_copy(src_ref.at[pl.ds(src_row, 1)], dst_ref.at[pl.ds(dst_row, 1)], sem)


DISPATCH_T = 256
ISSUE_UNROLL = 8


def _dispatch_kernel(dest_ref, pad_ref, nused_ref, h_ref, xb_hbm, zero_ref, sem, zsem, *, n_tokens, n_blocks):
    i = pl.program_id(0)
    t = DISPATCH_T
    rows = EXPERT_BLOCK

    @pl.when(i == 0)
    def _():
        zero_ref[...] = jnp.zeros(zero_ref.shape, F32)

        def zero_row(r, carry):
            _row_copy(zero_ref, 0, xb_hbm, r, zsem).start()
            return carry

        def wait_row(r, carry):
            _row_copy(zero_ref, 0, xb_hbm, 0, zsem).wait()
            return carry

        def block_copy(blk):
            return pltpu.make_async_copy(zero_ref, xb_hbm.at[pl.ds(pl.multiple_of(blk * rows, rows), rows)], zsem)

        def zero_block(blk, carry):
            block_copy(blk).start()
            return carry

        def wait_block(blk, carry):
            block_copy(0).wait()
            return carry

        for e in range(N_EXPERTS):
            lax.fori_loop(pad_ref[e], pad_ref[N_EXPERTS + e], zero_row, 0)
        lax.fori_loop(nused_ref[0], n_blocks, zero_block, 0)
        for e in range(N_EXPERTS):
            lax.fori_loop(pad_ref[e], pad_ref[N_EXPERTS + e], wait_row, 0)
        lax.fori_loop(nused_ref[0], n_blocks, wait_block, 0)

    def start(r, carry):
        tok = i * t + r
        _row_copy(h_ref, r, xb_hbm, dest_ref[tok], sem).start(priority=0)
        _row_copy(h_ref, r, xb_hbm, dest_ref[n_tokens + tok], sem).start(priority=1)
        return carry

    lax.fori_loop(0, t, start, 0, unroll=ISSUE_UNROLL)
    for _ in range(TOP_K):
        pltpu.make_async_copy(h_ref, xb_hbm.at[pl.ds(0, t)], sem).wait()


def moe_dispatch(h, dest, pad_rows, nused, n_blocks):
    n, d = h.shape
    t = DISPATCH_T
    rows = n_blocks * EXPERT_BLOCK
    kernel = functools.partial(_dispatch_kernel, n_tokens=n, n_blocks=n_blocks)
    return pl.pallas_call(
        kernel,
        out_shape=jax.ShapeDtypeStruct((rows, d), F32),
        grid_spec=pltpu.PrefetchScalarGridSpec(
            num_scalar_prefetch=3,
            grid=(n // t,),
            in_specs=[pl.BlockSpec((t, d), lambda i, *_: (i, 0))],
            out_specs=pl.BlockSpec(memory_space=pl.ANY),
            scratch_shapes=[pltpu.VMEM((EXPERT_BLOCK, d), F32), pltpu.SemaphoreType.DMA(()),
                            pltpu.SemaphoreType.DMA(())],
        ),
        compiler_params=_params(1, 16 << 20),
        name="moe_dispatch",
    )(dest, pad_rows, nused, h)


def _moe_up_kernel(be_ref, nused_ref, first_ref, nxte_ref, last_ref, x_ref, wg_hbm, wu_hbm, o_ref,
                   wgb_ref, wub_ref, sg_ref, su_ref, sems, *, layer, tn, n_col_tiles):
    j = pl.program_id(0)
    i = pl.program_id(1)

    def fetch(e, jj):
        cols = pl.ds(pl.multiple_of(jj * tn, 128), tn)
        return (pltpu.make_async_copy(wg_hbm.at[layer, e, :, cols], sg_ref, sems.at[0]),
                pltpu.make_async_copy(wu_hbm.at[layer, e, :, cols], su_ref, sems.at[1]))

    @pl.when(first_ref[i] == 1)
    def _():
        @pl.when(jnp.logical_and(j == 0, i == 0))
        def _():
            for c in fetch(be_ref[0], 0):
                c.start()

        for c in fetch(be_ref[i], j):
            c.wait()
        _cast_rows(sg_ref, wgb_ref)
        _cast_rows(su_ref, wub_ref)
        nj = j + last_ref[i]

        @pl.when(nj < n_col_tiles)
        def _():
            for c in fetch(nxte_ref[i], nj):
                c.start()

    @pl.when(i < nused_ref[0])
    def _():
        x = x_ref[...].astype(BF16)
        g = jnp.dot(x, wgb_ref[...], preferred_element_type=F32)
        u = jnp.dot(x, wub_ref[...], preferred_element_type=F32)
        o_ref[...] = (jax.nn.silu(g) * u).astype(o_ref.dtype)

    @pl.when(i >= nused_ref[0])
    def _():
        o_ref[...] = jnp.zeros(o_ref.shape, o_ref.dtype)


def _group_schedule(block_e):
    nb = block_e.shape[0]
    idx = jnp.arange(nb, dtype=jnp.int32)
    first = jnp.concatenate([jnp.ones((1,), bool), block_e[1:] != block_e[:-1]])
    later_first = jnp.logical_and(first[None, :], idx[None, :] > idx[:, None])
    nxt_idx = jnp.min(jnp.where(later_first, idx[None, :], nb), axis=1)
    last = nxt_idx >= nb
    nxt_idx = jnp.where(last, 0, nxt_idx)
    nxt_e = jnp.sum(jnp.where(idx[None, :] == nxt_idx[:, None], block_e[None, :], 0), axis=1)
    return first.astype(jnp.int32), nxt_e.astype(jnp.int32), last.astype(jnp.int32)


def moe_up(xb, w_gate, w_up, layer, block_e, nused, schedule, *, tn=1408):
    p, k = xb.shape
    rows = EXPERT_BLOCK
    f = w_gate.shape[-1]
    assert f % tn == 0 and tn % 128 == 0
    first, nxt_e, last = schedule
    kernel = functools.partial(_moe_up_kernel, layer=layer, tn=tn, n_col_tiles=f // tn)
    hbm = pl.BlockSpec(memory_space=pl.ANY)
    return pl.pallas_call(
        kernel,
        out_shape=jax.ShapeDtypeStruct((p, f), BF16),
        grid_spec=pltpu.PrefetchScalarGridSpec(
            num_scalar_prefetch=5,
            grid=(f // tn, p // rows),
            in_specs=[pl.BlockSpec((rows, k), lambda j, i, *_: (i, 0)), hbm, hbm],
            out_specs=pl.BlockSpec((rows, tn), lambda j, i, *_: (i, j)),
            scratch_shapes=[pltpu.VMEM((k, tn), BF16), pltpu.VMEM((k, tn), BF16),
                            pltpu.VMEM((k, tn), F32), pltpu.VMEM((k, tn), F32),
                            pltpu.SemaphoreType.DMA((2,))],
        ),
        compiler_params=_params(2, 52 << 20),
        name="moe_up",
    )(block_e, nused, first, nxt_e, last, xb, w_gate, w_up)


def _moe_down_kernel(be_ref, nused_ref, first_ref, nxte_ref, last_ref, x_ref, w_hbm, o_ref,
                     wb_ref, st_ref, sem, *, layer):
    i = pl.program_id(0)

    def fetch(e):
        return pltpu.make_async_copy(w_hbm.at[layer, e], st_ref, sem)

    @pl.when(first_ref[i] == 1)
    def _():
        @pl.when(i == 0)
        def _():
            fetch(be_ref[0]).start()

        fetch(be_ref[i]).wait()
        _cast_rows(st_ref, wb_ref)

        @pl.when(last_ref[i] == 0)
        def _():
            fetch(nxte_ref[i]).start()

    @pl.when(i < nused_ref[0])
    def _():
        o_ref[...] = jnp.dot(x_ref[...], wb_ref[...], preferred_element_type=F32)

    @pl.when(i >= nused_ref[0])
    def _():
        o_ref[...] = jnp.zeros(o_ref.shape, o_ref.dtype)


def moe_down(hb, w_down, layer, block_e, nused, schedule):
    p, k = hb.shape
    rows = EXPERT_BLOCK
    n = w_down.shape[-1]
    first, nxt_e, last = schedule
    kernel = functools.partial(_moe_down_kernel, layer=layer)
    return pl.pallas_call(
        kernel,
        out_shape=jax.ShapeDtypeStruct((p, n), F32),
        grid_spec=pltpu.PrefetchScalarGridSpec(
            num_scalar_prefetch=5,
            grid=(p // rows,),
            in_specs=[pl.BlockSpec((rows, k), lambda i, *_: (i, 0)), pl.BlockSpec(memory_space=pl.ANY)],
            out_specs=pl.BlockSpec((rows, n), lambda i, *_: (i, 0)),
            scratch_shapes=[pltpu.VMEM((k, n), BF16), pltpu.VMEM((k, n), F32), pltpu.SemaphoreType.DMA(())],
        ),
        compiler_params=_params(1, 50 << 20),
        name="moe_down",
    )(block_e, nused, first, nxt_e, last, hb, w_down)


COMBINE_T = 256


def _combine_kernel(dest_ref, y_hbm, wt_ref, h_ref, g_ref, b_ref, of_ref, ob_ref,
                    buf_ref, sems, *, n_tokens):
    i = pl.program_id(0)
    t = COMBINE_T

    def issue(block, slot):
        def body(r, carry):
            tok = block * t + r
            _row_copy(y_hbm, dest_ref[tok], buf_ref.at[slot, 0], r, sems.at[slot]).start(priority=0)
            _row_copy(y_hbm, dest_ref[n_tokens + tok], buf_ref.at[slot, 1], r, sems.at[slot]).start(priority=1)
            return carry

        lax.fori_loop(0, t, body, 0, unroll=ISSUE_UNROLL)

    slot = i % 2

    @pl.when(i == 0)
    def _():
        issue(0, 0)

    @pl.when(i + 1 < pl.num_programs(0))
    def _():
        issue(i + 1, 1 - slot)

    for k in range(TOP_K):
        pltpu.make_async_copy(y_hbm.at[pl.ds(0, t)], buf_ref.at[slot, k], sems.at[slot]).wait()
    ff = buf_ref[slot, 0] * wt_ref[:, 0:1] + buf_ref[slot, 1] * wt_ref[:, 1:2]
    y = _layer_norm(DN_ALPHA * h_ref[...] + ff, g_ref[...], b_ref[...])
    of_ref[...] = y
    ob_ref[...] = y.astype(BF16)


def moe_combine_ln(yb, dest, wts_t, h, ln_g, ln_b, layer, which):
    n, d = h.shape
    t = COMBINE_T
    row = pl.BlockSpec((t, d), lambda i, ds: (i, 0))
    par = pl.BlockSpec((None, None, 1, d), lambda i, ds: (layer, which, 0, 0))
    kernel = functools.partial(_combine_kernel, n_tokens=n)
    return pl.pallas_call(
        kernel,
        out_shape=(jax.ShapeDtypeStruct((n, d), F32), jax.ShapeDtypeStruct((n, d), BF16)),
        grid_spec=pltpu.PrefetchScalarGridSpec(
            num_scalar_prefetch=1,
            grid=(n // t,),
            in_specs=[
                pl.BlockSpec(memory_space=pl.ANY),
                pl.BlockSpec((t, TOP_K), lambda i, ds: (i, 0)),
                row, par, par,
            ],
            out_specs=(row, row),
            scratch_shapes=[pltpu.VMEM((2, TOP_K, t, d), F32), pltpu.SemaphoreType.DMA((2,))],
        ),
        compiler_params=_params(1, 32 << 20),
        name="moe_combine_ln",
    )(dest, yb, wts_t, h, ln_g, ln_b)


def moe_layer(h, hb, w_router, w_gate, w_up, w_down, ln_g, ln_b, layer, moe_idx):
    n = h.shape[0]
    m = n * TOP_K
    n_blocks = (m + N_EXPERTS * (EXPERT_BLOCK - 1) + EXPERT_BLOCK - 1) // EXPERT_BLOCK
    e, wts, rank, cnt = moe_router(h, w_router, moe_idx)
    counts = cnt[:, 0]
    padded = (counts + EXPERT_BLOCK - 1) // EXPERT_BLOCK * EXPERT_BLOCK
    pad_ends = jnp.cumsum(padded)
    pad_starts = pad_ends - padded
    expert_ids = jnp.arange(N_EXPERTS, dtype=jnp.int32)[:, None, None]
    dest = jnp.sum(jnp.where(e[None] == expert_ids, pad_starts[:, None, None], 0), axis=0) + rank
    block_start = jnp.arange(n_blocks, dtype=jnp.int32) * EXPERT_BLOCK
    block_e = jnp.minimum(jnp.sum(block_start[:, None] >= pad_ends[None, :], axis=1), N_EXPERTS - 1).astype(jnp.int32)
    nused = (pad_ends[-1:] // EXPERT_BLOCK).astype(jnp.int32)

    dest = dest.reshape(-1)
    pad_rows = jnp.concatenate([pad_starts + counts, pad_ends]).astype(jnp.int32)
    schedule = _group_schedule(block_e)
    xb = moe_dispatch(h, dest, pad_rows, nused, n_blocks)
    hid = moe_up(xb, w_gate, w_up, moe_idx, block_e, nused, schedule)
    yb = moe_down(hid, w_down, moe_idx, block_e, nused, schedule)
    return moe_combine_ln(yb, dest, wts.T, h, ln_g, ln_b, layer, 2)


def kernel(x, mem, rel_table, mix_w_in, gmlp_ln_g, gmlp_ln_b, gmlp_w_s, gmlp_b_s, pool_w, pool_scale, mix_w_out, diff_w_qkv, diff_lam_q1, diff_lam_k1, diff_lam_q2, diff_lam_k2, diff_subln_g, diff_w_o, xa_w_q, xa_w_kv, xa_w_o, ffn_w_gate, ffn_w_up, ffn_w_down, moe_w_router, moe_w_gate, moe_w_up, moe_w_down, ln_g, ln_b):
    batch, seq_len, d = x.shape
    mem_len = mem.shape[1]
    n = batch * seq_len
    h = x.reshape(n, d)
    hb = h
    memf = mem.reshape(batch * mem_len, d)
    ln_g4 = ln_g.reshape(DEPTH, 3, 1, d)
    ln_b4 = ln_b.reshape(DEPTH, 3, 1, d)
    lam_params = jnp.stack([diff_lam_q1, diff_lam_k1, diff_lam_q2, diff_lam_k2], axis=1)
    bias_tiles = attention_bias_tiles(rel_table)
    kv_all = matmul_all_layers(memf, xa_w_kv, tn=1024, out_dtype=BF16)
    xa_qk, xa_vo = cross_attention_folds(kv_all, xa_w_q, xa_w_o, batch, mem_len)

    for layer in range(DEPTH):
        i = layer // 2
        if layer % 2 == 0:
            hm = matmul(hb, mix_w_in, i, tm=512, tn=MIX_IN // 2, out_dtype=F32)
            mixed = gmlp_pool_mixer(hm, seq_len, gmlp_ln_g, gmlp_ln_b, gmlp_w_s, gmlp_b_s, pool_w,
                                    pool_scale, i)
            h, hb = matmul_residual_ln(mixed, mix_w_out, i, h, ln_g4, ln_b4, layer, 0)
        else:
            lam_init = 0.8 - 0.6 * math.exp(-0.3 * layer)
            qkv = matmul(hb, diff_w_qkv, i, tm=1024, tn=1024, out_dtype=BF16,
                         scaled_cols=D, scale=LOG2E * C_HEAD_DIM ** -0.5)
            att = diff_attention(qkv, bias_tiles, lam_params, diff_subln_g, i, lam_init, batch, seq_len)
            h, hb = matmul_residual_ln(att, diff_w_o, i, h, ln_g4, ln_b4, layer, 0)

        h, hb = cross_attention_residual_ln(hb, h, xa_qk, xa_vo, ln_g4, ln_b4, layer, 1, seq_len, mem_len)

        if layer % 2 == 0:
            hid = swiglu_up(hb, ffn_w_gate, ffn_w_up, i, tm=1024, tn=512)
            h, hb = matmul_kacc_residual_ln(hid, ffn_w_down, i, h, ln_g4, ln_b4, layer, 2,
                                            tm=1024, tk=512, single_buffer_rows=True)
        else:
            h, hb = moe_layer(h, hb, moe_w_router, moe_w_gate, moe_w_up, moe_w_down, ln_g4, ln_b4,
                              layer, i)
    return h.reshape(batch, seq_len, d)
```

```python
import functools
import math

import numpy as np
import jax
import jax.numpy as jnp
from jax import lax
from jax.experimental import pallas as pl
from jax.experimental.pallas import tpu as pltpu

F32 = jnp.float32
BF16 = jnp.bfloat16

D = 2048
DEPTH = 4
CHUNK = 64
GMLP_CHUNK = 128
A_WIDTH = D // 2
A_HEADS = 8
A_HEAD_DIM = A_WIDTH // A_HEADS
B_WIDTH = D // 2
POOL_WINDOWS = (2, 4, 8, 16)
B_GROUP = B_WIDTH // len(POOL_WINDOWS)
MIX_IN = 2 * A_WIDTH + B_WIDTH
C_HEADS = 16
C_HEAD_DIM = D // (2 * C_HEADS)
C_V_DIM = 2 * C_HEAD_DIM
REL_BUCKETS = 32
REL_MAX_DIST = 128
X_HEADS = 4
X_HEAD_DIM = D // X_HEADS
N_EXPERTS = 8
TOP_K = 2
EXPERT_BLOCK = 256
DN_ALPHA = (2 * DEPTH) ** 0.25
LN_EPS = 1e-5
NEG = -1e30

VMEM_CAP_BYTES = 56 * 1024 * 1024
CAST_ROWS = 256
ATT_TQ = 256
POOL_HALO = 16


def _params(n_axes, vmem_bytes):
    return pltpu.CompilerParams(
        dimension_semantics=("arbitrary",) * n_axes,
        vmem_limit_bytes=int(min(max(vmem_bytes, 16 * 1024 * 1024), VMEM_CAP_BYTES)),
    )


def _cast_rows(w_ref, wb_ref):
    k = w_ref.shape[0]
    rows = CAST_ROWS if k % CAST_ROWS == 0 else k

    def body(c, carry):
        r = pl.multiple_of(c * rows, rows)
        wb_ref[pl.ds(r, rows), :] = w_ref[pl.ds(r, rows), :].astype(BF16)
        return carry

    lax.fori_loop(0, k // rows, body, 0)


def _layer_norm(x, g, b):
    mu = jnp.mean(x, axis=-1, keepdims=True)
    xc = x - mu
    var = jnp.mean(xc * xc, axis=-1, keepdims=True)
    return xc * lax.rsqrt(var + LN_EPS) * g + b


def _mm_kernel(a_ref, w_ref, o_ref, wb_ref, *, scaled_tiles, scale):
    j = pl.program_id(0)

    @pl.when(pl.program_id(1) == 0)
    def _():
        _cast_rows(w_ref, wb_ref)

    a = a_ref[...].astype(BF16)
    y = jnp.dot(a, wb_ref[...], preferred_element_type=F32)
    if scaled_tiles:
        y = y * jnp.where(j < scaled_tiles, scale, 1.0)
    o_ref[...] = y.astype(o_ref.dtype)


def matmul(a, w, layer, *, tm, tn, out_dtype, scaled_cols=0, scale=1.0):
    m, k = a.shape
    n = w.shape[-1]
    assert m % tm == 0 and n % tn == 0 and w.shape[-2] == k and scaled_cols % tn == 0
    vmem = 2 * tm * k * a.dtype.itemsize + 2 * k * tn * 4 + k * tn * 2 + 2 * tm * tn * 4 + tm * tn * 4
    kernel = functools.partial(_mm_kernel, scaled_tiles=scaled_cols // tn, scale=scale)
    return pl.pallas_call(
        kernel,
        out_shape=jax.ShapeDtypeStruct((m, n), out_dtype),
        grid=(n // tn, m // tm),
        in_specs=[
            pl.BlockSpec((tm, k), lambda j, i: (i, 0)),
            pl.BlockSpec((None, k, tn), lambda j, i: (layer, 0, j)),
        ],
        out_specs=pl.BlockSpec((tm, tn), lambda j, i: (i, j)),
        scratch_shapes=[pltpu.VMEM((k, tn), BF16)],
        compiler_params=_params(2, vmem + (4 << 20)),
        name="matmul",
    )(a, w)


MMLN_GROUPS = 2


def _mm_ln_kernel(a_ref, w_ref, h_ref, g_ref, b_ref, of_ref, ob_ref, wb_ref):
    @pl.when(pl.program_id(0) == 0)
    def _():
        _cast_rows(w_ref, wb_ref)

    rows = a_ref.shape[0] // MMLN_GROUPS
    for r in range(MMLN_GROUPS):
        sl = slice(r * rows, (r + 1) * rows)
        y = jnp.dot(a_ref[sl, :], wb_ref[...], preferred_element_type=F32)
        out = _layer_norm(DN_ALPHA * h_ref[sl, :] + y, g_ref[...], b_ref[...])
        of_ref[sl, :] = out
        ob_ref[sl, :] = out.astype(BF16)


def matmul_residual_ln(a, w, w_layer, h, ln_g, ln_b, layer, which, *, tm=512):
    m, k = a.shape
    d = w.shape[-1]
    assert m % tm == 0 and h.shape == (m, d)
    row = pl.BlockSpec((tm, d), lambda i: (i, 0))
    par = pl.BlockSpec((None, None, 1, d), lambda i: (layer, which, 0, 0))
    vmem = k * d * 4 + k * d * 2 + 2 * tm * k * 2 + 4 * tm * d * 4 + 2 * tm * d * 2 + 3 * tm * d * 4
    return pl.pallas_call(
        _mm_ln_kernel,
        out_shape=(jax.ShapeDtypeStruct((m, d), F32), jax.ShapeDtypeStruct((m, d), BF16)),
        grid=(m // tm,),
        in_specs=[
            pl.BlockSpec((tm, k), lambda i: (i, 0)),
            pl.BlockSpec((None, k, d), lambda i: (w_layer, 0, 0), pipeline_mode=pl.Buffered(1)),
            row, par, par,
        ],
        out_specs=(row, row),
        scratch_shapes=[pltpu.VMEM((k, d), BF16)],
        compiler_params=_params(1, vmem + (4 << 20)),
        name="matmul_residual_ln",
    )(a, w, h, ln_g, ln_b)


KACC_GROUP_ROWS = 256


def _mm_kacc_ln_kernel(a_ref, w_ref, h_ref, g_ref, b_ref, of_ref, ob_ref, acc_ref):
    k = pl.program_id(1)

    @pl.when(jnp.logical_and(pl.program_id(0) == 0, k == 0))
    def _():
        acc_ref[...] = jnp.zeros(acc_ref.shape, F32)

    part = jnp.dot(a_ref[...], w_ref[...].astype(BF16), preferred_element_type=F32)
    acc_ref[...] = jnp.where(k > 0, acc_ref[...], 0.0) + part

    @pl.when(k == pl.num_programs(1) - 1)
    def _():
        for r in range(0, acc_ref.shape[0], KACC_GROUP_ROWS):
            sl = slice(r, r + KACC_GROUP_ROWS)
            out = _layer_norm(DN_ALPHA * h_ref[sl, :] + acc_ref[sl, :], g_ref[...], b_ref[...])
            of_ref[sl, :] = out
            ob_ref[sl, :] = out.astype(BF16)


def matmul_kacc_residual_ln(a, w, w_layer, h, ln_g, ln_b, layer, which, *, tm, tk, single_buffer_rows):
    m, kk = a.shape
    d = w.shape[-1]
    assert m % tm == 0 and kk % tk == 0 and tm % KACC_GROUP_ROWS == 0
    mode = dict(pipeline_mode=pl.Buffered(1)) if single_buffer_rows else {}
    row = pl.BlockSpec((tm, d), lambda i, k: (i, 0))
    out_row = pl.BlockSpec((tm, d), lambda i, k: (i, 0), **mode)
    par = pl.BlockSpec((None, None, 1, d), lambda i, k: (layer, which, 0, 0))
    return pl.pallas_call(
        _mm_kacc_ln_kernel,
        out_shape=(jax.ShapeDtypeStruct((m, d), F32), jax.ShapeDtypeStruct((m, d), BF16)),
        grid=(m // tm, kk // tk),
        in_specs=[
            pl.BlockSpec((tm, tk), lambda i, k: (i, k)),
            pl.BlockSpec((None, tk, d), lambda i, k: (w_layer, k, 0)),
            row, par, par,
        ],
        out_specs=(out_row, out_row),
        scratch_shapes=[pltpu.VMEM((tm, d), F32)],
        compiler_params=_params(2, VMEM_CAP_BYTES),
        name="matmul_kacc_residual_ln",
    )(a, w, h, ln_g, ln_b)


def _mm_layers_kernel(a_ref, w_ref, o_ref):
    o_ref[...] = jnp.dot(a_ref[...].astype(BF16), w_ref[...].astype(BF16),
                         preferred_element_type=F32).astype(o_ref.dtype)


def matmul_all_layers(a, w, *, tn, out_dtype):
    m, k = a.shape
    layers, _, n = w.shape
    assert n % tn == 0
    return pl.pallas_call(
        _mm_layers_kernel,
        out_shape=jax.ShapeDtypeStruct((layers, m, n), out_dtype),
        grid=(layers, n // tn),
        in_specs=[
            pl.BlockSpec((m, k), lambda l, j: (0, 0)),
            pl.BlockSpec((None, k, tn), lambda l, j: (l, 0, j)),
        ],
        out_specs=pl.BlockSpec((None, m, tn), lambda l, j: (l, 0, j)),
        compiler_params=_params(2, 2 * m * k * 4 + 3 * k * tn * 4 + 3 * m * tn * 4 + (4 << 20)),
        name="matmul_all_layers",
    )(a, w)


def _swiglu_kernel(a_ref, wg_ref, wu_ref, o_ref, wgb_ref, wub_ref):
    @pl.when(pl.program_id(1) == 0)
    def _():
        _cast_rows(wg_ref, wgb_ref)
        _cast_rows(wu_ref, wub_ref)

    a = a_ref[...]
    g = jnp.dot(a, wgb_ref[...], preferred_element_type=F32)
    u = jnp.dot(a, wub_ref[...], preferred_element_type=F32)
    o_ref[...] = (jax.nn.silu(g) * u).astype(o_ref.dtype)


def swiglu_up(a, wg, wu, layer, *, tm, tn):
    m, k = a.shape
    n = wg.shape[-1]
    assert m % tm == 0 and n % tn == 0
    vmem = 2 * tm * k * 2 + 4 * k * tn * 4 + 2 * k * tn * 2 + 2 * tm * tn * 2 + 3 * tm * tn * 4
    wspec = pl.BlockSpec((None, k, tn), lambda j, i: (layer, 0, j))
    return pl.pallas_call(
        _swiglu_kernel,
        out_shape=jax.ShapeDtypeStruct((m, n), BF16),
        grid=(n // tn, m // tm),
        in_specs=[pl.BlockSpec((tm, k), lambda j, i: (i, 0)), wspec, wspec],
        out_specs=pl.BlockSpec((tm, tn), lambda j, i: (i, j)),
        scratch_shapes=[pltpu.VMEM((k, tn), BF16), pltpu.VMEM((k, tn), BF16)],
        compiler_params=_params(2, vmem + (4 << 20)),
        name="swiglu_up",
    )(a, wg, wu)


MIX_ROWS = 2 * GMLP_CHUNK


def _mixer_kernel(hm_ref, prev_ref, lng_ref, lnb_ref, ws_ref, bst_ref, pw_ref, ps_ref, o_ref,
                  wbd_ref, ext_ref, dbl_ref, *, blocks_per_seq):
    i = pl.program_id(0)

    @pl.when(i == 0)
    def _():
        p = lax.broadcasted_iota(jnp.int32, (GMLP_CHUNK, GMLP_CHUNK), 0)
        q = lax.broadcasted_iota(jnp.int32, (GMLP_CHUNK, GMLP_CHUNK), 1)
        allowed = (q // CHUNK) <= (p // CHUNK)
        wbd_ref[...] = jnp.zeros(wbd_ref.shape, BF16)
        for h in range(A_HEADS):
            w = jnp.where(allowed, ws_ref[h], 0.0).astype(BF16)
            wbd_ref[h, 0:GMLP_CHUNK, 0:GMLP_CHUNK] = w
            wbd_ref[h, GMLP_CHUNK:MIX_ROWS, GMLP_CHUNK:MIX_ROWS] = w

    z = jax.nn.gelu(hm_ref[:, 0:2 * A_WIDTH])
    u = z[:, 0:A_WIDTH]
    v = _layer_norm(z[:, A_WIDTH:2 * A_WIDTH], lng_ref[...], lnb_ref[...]).astype(BF16)
    for h in range(A_HEADS):
        c0 = h * A_HEAD_DIM
        sv = jnp.dot(wbd_ref[h], v[:, c0:c0 + A_HEAD_DIM], preferred_element_type=F32)
        sv = sv + bst_ref[:, h:h + 1]
        o_ref[:, c0:c0 + A_HEAD_DIM] = (u[:, c0:c0 + A_HEAD_DIM] * sv).astype(BF16)

    blk = i % blocks_per_seq
    prev = jnp.where(blk == 0, 0.0, prev_ref[...])
    ext_ref[0:POOL_HALO, :] = prev
    ext_ref[POOL_HALO:POOL_HALO + MIX_ROWS, :] = hm_ref[:, 2 * A_WIDTH:MIX_IN]
    t1 = blk * MIX_ROWS + lax.broadcasted_iota(jnp.int32, (MIX_ROWS, 1), 0) + 1
    for g, w in enumerate(POOL_WINDOWS):
        c0 = g * B_GROUP
        cols = slice(c0, c0 + B_GROUP)
        pin = ext_ref[POOL_HALO:POOL_HALO + MIX_ROWS, cols]
        src_ref, src_cols = ext_ref, cols
        k = 1
        while k < w:
            lo = 2 * k - 1
            n = POOL_HALO + MIX_ROWS - lo
            dbl_ref[lo:lo + n, :] = src_ref[lo:lo + n, src_cols] + src_ref[lo - k:lo - k + n, src_cols]
            src_ref, src_cols = dbl_ref, slice(None)
            k *= 2
        s = dbl_ref[POOL_HALO:POOL_HALO + MIX_ROWS, :]
        cnt = jnp.minimum(t1, w).astype(F32)
        pooled = s / cnt - pin
        b = jnp.dot(pooled.astype(BF16), pw_ref[g].astype(BF16), preferred_element_type=F32)
        b = b * ps_ref[:, c0:c0 + B_GROUP]
        o_ref[:, A_WIDTH + c0:A_WIDTH + c0 + B_GROUP] = b.astype(BF16)


def gmlp_pool_mixer(hm, seq_len, ln_g, ln_b, w_s, b_s, pool_w, pool_scale, layer):
    n = hm.shape[0]
    assert seq_len % MIX_ROWS == 0 and max(POOL_WINDOWS) <= POOL_HALO
    blocks_per_seq = seq_len // MIX_ROWS
    halo_per_block = MIX_ROWS // POOL_HALO
    bst = jnp.tile(b_s[layer].T, (MIX_ROWS // GMLP_CHUNK, 1))
    kernel = functools.partial(_mixer_kernel, blocks_per_seq=blocks_per_seq)
    return pl.pallas_call(
        kernel,
        out_shape=jax.ShapeDtypeStruct((n, D), BF16),
        grid=(n // MIX_ROWS,),
        in_specs=[
            pl.BlockSpec((MIX_ROWS, MIX_IN), lambda i: (i, 0)),
            pl.BlockSpec((POOL_HALO, B_WIDTH),
                         lambda i: (jnp.maximum(i * halo_per_block - 1, 0), 2 * A_WIDTH // B_WIDTH)),
            pl.BlockSpec((None, 1, A_WIDTH), lambda i: (layer, 0, 0)),
            pl.BlockSpec((None, 1, A_WIDTH), lambda i: (layer, 0, 0)),
            pl.BlockSpec((None, A_HEADS, GMLP_CHUNK, GMLP_CHUNK), lambda i: (layer, 0, 0, 0)),
            pl.BlockSpec((MIX_ROWS, A_HEADS), lambda i: (0, 0)),
            pl.BlockSpec((None, len(POOL_WINDOWS), B_GROUP, B_GROUP), lambda i: (layer, 0, 0, 0)),
            pl.BlockSpec((None, 1, B_WIDTH), lambda i: (layer, 0, 0)),
        ],
        out_specs=pl.BlockSpec((MIX_ROWS, D), lambda i: (i, 0)),
        scratch_shapes=[
            pltpu.VMEM((A_HEADS, MIX_ROWS, MIX_ROWS), BF16),
            pltpu.VMEM((POOL_HALO + MIX_ROWS, B_WIDTH), F32),
            pltpu.VMEM((POOL_HALO + MIX_ROWS, B_GROUP), F32),
        ],
        compiler_params=_params(1, 40 << 20),
        name="gmlp_pool_mixer",
    )(hm, hm, ln_g.reshape(-1, 1, A_WIDTH), ln_b.reshape(-1, 1, A_WIDTH), w_s, bst, pool_w,
      pool_scale.reshape(-1, 1, B_WIDTH))


def _t5_bucket_np(rel):
    half = REL_BUCKETS // 2
    max_exact = half // 2
    rel = np.asarray(rel, dtype=np.int64)
    n = np.abs(rel)
    steps = half - max_exact
    ratio = REL_MAX_DIST // max_exact
    large = np.zeros_like(n)
    nn = n.astype(object) ** steps
    for j in range(1, steps + 1):
        large = large + (nn >= (max_exact ** steps) * (ratio ** j)).astype(np.int64)
    large = np.minimum(max_exact + large, half - 1)
    return np.where(rel > 0, half, 0) + np.where(n < max_exact, n, large)


def _bias_bucket_tiles():
    qi = np.arange(ATT_TQ)[:, None]
    kj = np.arange(ATT_TQ)[None, :]
    diag = _t5_bucket_np(kj - qi)
    diag = np.where((kj // CHUNK) <= (qi // CHUNK), diag, -1)
    prev = _t5_bucket_np(kj - ATT_TQ - qi)
    return np.stack([diag, prev], 0).astype(np.int32)


FAR_BUCKET = REL_BUCKETS // 2 - 1
LOG2E = math.log2(math.e)


def _bias_kernel(rel_ref, bk_ref, o_ref):
    h = pl.program_id(0)
    bk = bk_ref[...]
    acc = jnp.zeros(bk.shape, F32)
    for b in range(REL_BUCKETS):
        acc = jnp.where(bk == b, rel_ref[b, h], acc)
    o_ref[...] = jnp.where(bk < 0, NEG, (acc - rel_ref[FAR_BUCKET, h]) * LOG2E)


def attention_bias_tiles(rel_table):
    assert ATT_TQ >= REL_MAX_DIST and ATT_TQ % CHUNK == 0
    bk = jnp.asarray(_bias_bucket_tiles())
    return pl.pallas_call(
        _bias_kernel,
        out_shape=jax.ShapeDtypeStruct((C_HEADS, 2, ATT_TQ, ATT_TQ), F32),
        grid=(C_HEADS,),
        in_specs=[
            pl.BlockSpec(memory_space=pltpu.SMEM),
            pl.BlockSpec((2, ATT_TQ, ATT_TQ), lambda h: (0, 0, 0)),
        ],
        out_specs=pl.BlockSpec((None, 2, ATT_TQ, ATT_TQ), lambda h: (h, 0, 0, 0)),
        compiler_params=_params(1, 16 << 20),
        name="attention_bias_tiles",
    )(rel_table, bk)


ATT_HEADS_PER_STEP = 2


def _diff_attn_kernel(q_ref, k_ref, v_ref, bias_ref, lam_ref, g_ref, o_ref, vx_ref, *, lam_init, seq_len):
    tq = ATT_TQ
    nt = (((1,), (1,)), ((), ()))
    lp = lam_ref[...]
    lam = (jnp.exp(jnp.sum(lp[0:1] * lp[1:2], axis=1, keepdims=True))
           - jnp.exp(jnp.sum(lp[2:3] * lp[3:4], axis=1, keepdims=True)) + lam_init)
    lane = lax.broadcasted_iota(jnp.int32, (tq, C_V_DIM), 1)
    for hh in range(ATT_HEADS_PER_STEP):
        hc = slice(hh * C_V_DIM, (hh + 1) * C_V_DIM)
        vx_ref[hh, :, 0:C_V_DIM] = v_ref[:, hc]
        vx_ref[hh, :, C_V_DIM:2 * C_V_DIM] = jnp.ones((seq_len, C_V_DIM), BF16)
        for qi in range(seq_len // tq):
            q = q_ref[qi * tq:(qi + 1) * tq, hc]
            zero = jnp.zeros_like(q)
            parts = []
            if qi >= 2:
                parts.append((0, (qi - 1) * tq, None))
            if qi >= 1:
                parts.append(((qi - 1) * tq, qi * tq, 1))
            parts.append((qi * tq, (qi + 1) * tq, 0))
            outs = []
            for half in range(2):
                keep = (lane < C_HEAD_DIM) if half == 0 else (lane >= C_HEAD_DIM)
                qh = jnp.where(keep, q, zero)
                scores = []
                m = None
                for a, b, bi in parts:
                    s = lax.dot_general(qh, k_ref[a:b, hc], nt, preferred_element_type=F32)
                    if bi is not None:
                        s = s + bias_ref[hh, bi]
                    scores.append(s)
                    sm = jnp.max(s, axis=1, keepdims=True)
                    m = sm if m is None else jnp.maximum(m, sm)
                acc = None
                for s, (a, b, _) in zip(scores, parts):
                    p = jnp.exp2(s - m).astype(BF16)
                    pv = jnp.dot(p, vx_ref[hh, a:b, :], preferred_element_type=F32)
                    acc = pv if acc is None else acc + pv
                outs.append(acc[:, 0:C_V_DIM] / acc[:, C_V_DIM:C_V_DIM + 1])
            o = outs[0] - lam * outs[1]
            y = o * lax.rsqrt(jnp.mean(o * o, axis=-1, keepdims=True) + LN_EPS) * g_ref[...]
            o_ref[qi * tq:(qi + 1) * tq, hc] = (y * (1.0 - lam_init)).astype(BF16)


def diff_attention(qkv, bias_tiles, lam_params, subln_g, layer, lam_init, batch, seq_len):
    n = qkv.shape[0]
    tq = ATT_TQ
    hps = ATT_HEADS_PER_STEP
    assert seq_len % tq == 0 and C_HEADS % hps == 0
    groups = C_HEADS // hps
    kernel = functools.partial(_diff_attn_kernel, lam_init=lam_init, seq_len=seq_len)
    seq = lambda off: pl.BlockSpec((seq_len, hps * C_V_DIM), lambda b, h: (b, off + h))
    return pl.pallas_call(
        kernel,
        out_shape=jax.ShapeDtypeStruct((n, D), BF16),
        grid=(batch, groups),
        in_specs=[
            seq(0), seq(groups), seq(2 * groups),
            pl.BlockSpec((hps, 2, tq, tq), lambda b, h: (h, 0, 0, 0)),
            pl.BlockSpec((None, 4, C_HEAD_DIM), lambda b, h: (layer, 0, 0)),
            pl.BlockSpec((None, 1, C_V_DIM), lambda b, h: (layer, 0, 0)),
        ],
        out_specs=seq(0),
        scratch_shapes=[pltpu.VMEM((hps, seq_len, 2 * C_V_DIM), BF16)],
        compiler_params=_params(2, 48 << 20),
        name="diff_attention",
    )(qkv, qkv, qkv, bias_tiles, lam_params, subln_g.reshape(-1, 1, C_V_DIM))


def _qk_fold_kernel(wq_ref, k_ref, o_ref, *, batch, mem_len):
    wb = wq_ref[...].astype(BF16)
    for b in range(batch):
        qk = lax.dot_general(wb, k_ref[b * mem_len:(b + 1) * mem_len, :], (((1,), (1,)), ((), ())),
                             preferred_element_type=F32)
        o_ref[b] = (qk * (LOG2E * X_HEAD_DIM ** -0.5)).astype(o_ref.dtype)


def _vo_fold_kernel(v_ref, wo_ref, o_ref, *, batch, mem_len):
    wb = wo_ref[...].astype(BF16)
    for b in range(batch):
        o_ref[b] = jnp.dot(v_ref[b * mem_len:(b + 1) * mem_len, :], wb,
                           preferred_element_type=F32).astype(o_ref.dtype)


def cross_attention_folds(kv_all, w_q, w_o, batch, mem_len):
    layers = kv_all.shape[0]
    grid = (layers, X_HEADS)
    qk = pl.pallas_call(
        functools.partial(_qk_fold_kernel, batch=batch, mem_len=mem_len),
        out_shape=jax.ShapeDtypeStruct((layers, batch, D, X_HEADS * mem_len), BF16),
        grid=grid,
        in_specs=[
            pl.BlockSpec((None, D, X_HEAD_DIM), lambda l, h: (l, 0, h)),
            pl.BlockSpec((None, batch * mem_len, X_HEAD_DIM), lambda l, h: (l, 0, h)),
        ],
        out_specs=pl.BlockSpec((None, batch, D, mem_len), lambda l, h: (l, 0, 0, h)),
        compiler_params=_params(2, 32 << 20),
        name="cross_attention_qk_fold",
    )(w_q, kv_all)
    vo = pl.pallas_call(
        functools.partial(_vo_fold_kernel, batch=batch, mem_len=mem_len),
        out_shape=jax.ShapeDtypeStruct((layers, batch, X_HEADS * mem_len, D), BF16),
        grid=grid,
        in_specs=[
            pl.BlockSpec((None, batch * mem_len, X_HEAD_DIM), lambda l, h: (l, 0, X_HEADS + h)),
            pl.BlockSpec((None, X_HEAD_DIM, D), lambda l, h: (l, h, 0)),
        ],
        out_specs=pl.BlockSpec((None, batch, mem_len, D), lambda l, h: (l, 0, h, 0)),
        compiler_params=_params(2, 32 << 20),
        name="cross_attention_vo_fold",
    )(kv_all, w_o)
    return qk, vo


XATTN_GROUPS = 2


def _xattn_ln_kernel(hb_ref, qk_ref, vo_ref, h_ref, g_ref, b_ref, of_ref, ob_ref, *, mem_len):
    rows = hb_ref.shape[0] // XATTN_GROUPS
    for r in range(XATTN_GROUPS):
        sl = slice(r * rows, (r + 1) * rows)
        s = jnp.dot(hb_ref[sl, :], qk_ref[...], preferred_element_type=F32)
        probs = []
        for hd in range(X_HEADS):
            sh = s[:, hd * mem_len:(hd + 1) * mem_len]
            e = jnp.exp2(sh - jnp.max(sh, axis=1, keepdims=True))
            inv = 1.0 / jnp.sum(e, axis=1, keepdims=True)
            probs.append((e * inv).astype(BF16))
        p = jnp.concatenate(probs, axis=1)
        y = jnp.dot(p, vo_ref[...], preferred_element_type=F32)
        out = _layer_norm(DN_ALPHA * h_ref[sl, :] + y, g_ref[...], b_ref[...])
        of_ref[sl, :] = out
        ob_ref[sl, :] = out.astype(BF16)


def cross_attention_residual_ln(hb, h, qk, vo, ln_g, ln_b, layer, which, seq_len, mem_len, *, tm=512):
    n, d = h.shape
    assert seq_len % tm == 0 and tm % XATTN_GROUPS == 0
    blocks_per_seq = seq_len // tm
    row = pl.BlockSpec((tm, d), lambda i: (i, 0))
    par = pl.BlockSpec((None, None, 1, d), lambda i: (layer, which, 0, 0))
    kw = X_HEADS * mem_len
    kernel = functools.partial(_xattn_ln_kernel, mem_len=mem_len)
    return pl.pallas_call(
        kernel,
        out_shape=(jax.ShapeDtypeStruct((n, d), F32), jax.ShapeDtypeStruct((n, d), BF16)),
        grid=(n // tm,),
        in_specs=[
            row,
            pl.BlockSpec((None, None, d, kw), lambda i: (layer, i // blocks_per_seq, 0, 0)),
            pl.BlockSpec((None, None, kw, d), lambda i: (layer, i // blocks_per_seq, 0, 0)),
            row, par, par,
        ],
        out_specs=(row, row),
        compiler_params=_params(1, 52 << 20),
        name="cross_attention_residual_ln",
    )(hb, qk, vo, h, ln_g, ln_b)


ROUTE_T = 256


def _router_kernel(h_ref, wrt_ref, e_ref, w_ref, rank_ref, cnt_ref, base_ref):
    i = pl.program_id(0)

    @pl.when(i == 0)
    def _():
        base_ref[...] = jnp.zeros(base_ref.shape, F32)

    t = ROUTE_T
    nt = (((1,), (1,)), ((), ()))
    hf = h_ref[...]
    wf = wrt_ref[...]
    hh = hf.astype(BF16)
    wh = wf.astype(BF16)
    hl = (hf - hh.astype(F32)).astype(BF16)
    wl = (wf - wh.astype(F32)).astype(BF16)
    logits = (lax.dot_general(wh, hh, nt, preferred_element_type=F32)
              + lax.dot_general(wh, hl, nt, preferred_element_type=F32)
              + lax.dot_general(wl, hh, nt, preferred_element_type=F32))
    eidx = lax.broadcasted_iota(jnp.int32, (N_EXPERTS, t), 0)
    m1 = jnp.max(logits, axis=0, keepdims=True)
    e1 = jnp.min(jnp.where(logits == m1, eidx, N_EXPERTS), axis=0, keepdims=True)
    oh1 = eidx == e1
    rest = jnp.where(oh1, -jnp.inf, logits)
    m2 = jnp.max(rest, axis=0, keepdims=True)
    e2 = jnp.min(jnp.where(rest == m2, eidx, N_EXPERTS), axis=0, keepdims=True)
    oh2 = eidx == e2
    ex = jnp.exp(m2 - m1)
    den = 1.0 + ex
    e_ref[0:1, :] = e1
    e_ref[1:2, :] = e2
    w_ref[0:1, :] = 1.0 / den
    w_ref[1:2, :] = ex / den

    oh = (oh1.astype(F32) + oh2.astype(F32)).astype(BF16)
    r = lax.broadcasted_iota(jnp.int32, (t, t), 0)
    c = lax.broadcasted_iota(jnp.int32, (t, t), 1)
    before = (r < c).astype(BF16)
    ones = jnp.ones((t, t), BF16)
    pos = base_ref[...] + jnp.dot(oh, before, preferred_element_type=F32)
    rank_ref[0:1, :] = jnp.sum(jnp.where(oh1, pos, 0.0), axis=0, keepdims=True).astype(jnp.int32)
    rank_ref[1:2, :] = jnp.sum(jnp.where(oh2, pos, 0.0), axis=0, keepdims=True).astype(jnp.int32)
    total = base_ref[...] + jnp.dot(oh, ones, preferred_element_type=F32)
    base_ref[...] = total
    cnt_ref[...] = total[:, 0:128].astype(jnp.int32)


def moe_router(h, w_router, layer):
    n = h.shape[0]
    t = ROUTE_T
    wrt = jnp.swapaxes(w_router, 1, 2)
    pair = pl.BlockSpec((TOP_K, t), lambda i: (0, i))
    return pl.pallas_call(
        _router_kernel,
        out_shape=(
            jax.ShapeDtypeStruct((TOP_K, n), jnp.int32),
            jax.ShapeDtypeStruct((TOP_K, n), F32),
            jax.ShapeDtypeStruct((TOP_K, n), jnp.int32),
            jax.ShapeDtypeStruct((N_EXPERTS, 128), jnp.int32),
        ),
        grid=(n // t,),
        in_specs=[
            pl.BlockSpec((t, D), lambda i: (i, 0)),
            pl.BlockSpec((None, N_EXPERTS, D), lambda i: (layer, 0, 0)),
        ],
        out_specs=(pair, pair, pair, pl.BlockSpec((N_EXPERTS, 128), lambda i: (0, 0))),
        scratch_shapes=[pltpu.VMEM((N_EXPERTS, t), F32)],
        compiler_params=_params(1, 16 << 20),
        name="moe_router",
    )(h, wrt)


def _row_copy(src_ref, src_row, dst_ref, dst_row, sem):
    return pltpu.make_async_copy(src_ref.at[pl.ds(src_row, 1)], dst_ref.at[pl.ds(dst_row, 1)], sem)


DISPATCH_T = 256
ISSUE_UNROLL = 8


def _dispatch_kernel(dest_ref, pad_ref, nused_ref, h_ref, xb_hbm, zero_ref, sem, zsem, *, n_tokens, n_blocks):
    i = pl.program_id(0)
    t = DISPATCH_T
    rows = EXPERT_BLOCK

    @pl.when(i == 0)
    def _():
        zero_ref[...] = jnp.zeros(zero_ref.shape, F32)

        def zero_row(r, carry):
            _row_copy(zero_ref, 0, xb_hbm, r, zsem).start()
            return carry

        def wait_row(r, carry):
            _row_copy(zero_ref, 0, xb_hbm, 0, zsem).wait()
            return carry

        def block_copy(blk):
            return pltpu.make_async_copy(zero_ref, xb_hbm.at[pl.ds(pl.multiple_of(blk * rows, rows), rows)], zsem)

        def zero_block(blk, carry):
            block_copy(blk).start()
            return carry

        def wait_block(blk, carry):
            block_copy(0).wait()
            return carry

        for e in range(N_EXPERTS):
            lax.fori_loop(pad_ref[e], pad_ref[N_EXPERTS + e], zero_row, 0)
        lax.fori_loop(nused_ref[0], n_blocks, zero_block, 0)
        for e in range(N_EXPERTS):
            lax.fori_loop(pad_ref[e], pad_ref[N_EXPERTS + e], wait_row, 0)
        lax.fori_loop(nused_ref[0], n_blocks, wait_block, 0)

    def start(r, carry):
        tok = i * t + r
        _row_copy(h_ref, r, xb_hbm, dest_ref[tok], sem).start(priority=0)
        _row_copy(h_ref, r, xb_hbm, dest_ref[n_tokens + tok], sem).start(priority=1)
        return carry

    lax.fori_loop(0, t, start, 0, unroll=ISSUE_UNROLL)
    for _ in range(TOP_K):
        pltpu.make_async_copy(h_ref, xb_hbm.at[pl.ds(0, t)], sem).wait()


def moe_dispatch(h, dest, pad_rows, nused, n_blocks):
    n, d = h.shape
    t = DISPATCH_T
    rows = n_blocks * EXPERT_BLOCK
    kernel = functools.partial(_dispatch_kernel, n_tokens=n, n_blocks=n_blocks)
    return pl.pallas_call(
        kernel,
        out_shape=jax.ShapeDtypeStruct((rows, d), F32),
        grid_spec=pltpu.PrefetchScalarGridSpec(
            num_scalar_prefetch=3,
            grid=(n // t,),
            in_specs=[pl.BlockSpec((t, d), lambda i, *_: (i, 0))],
            out_specs=pl.BlockSpec(memory_space=pl.ANY),
            scratch_shapes=[pltpu.VMEM((EXPERT_BLOCK, d), F32), pltpu.SemaphoreType.DMA(()),
                            pltpu.SemaphoreType.DMA(())],
        ),
        compiler_params=_params(1, 16 << 20),
        name="moe_dispatch",
    )(dest, pad_rows, nused, h)


def _moe_up_kernel(be_ref, nused_ref, first_ref, nxte_ref, last_ref, x_ref, wg_hbm, wu_hbm, o_ref,
                   wgb_ref, wub_ref, sg_ref, su_ref, sems, *, layer, tn, n_col_tiles):
    j = pl.program_id(0)
    i = pl.program_id(1)

    def fetch(e, jj):
        cols = pl.ds(pl.multiple_of(jj * tn, 128), tn)
        return (pltpu.make_async_copy(wg_hbm.at[layer, e, :, cols], sg_ref, sems.at[0]),
                pltpu.make_async_copy(wu_hbm.at[layer, e, :, cols], su_ref, sems.at[1]))

    @pl.when(first_ref[i] == 1)
    def _():
        @pl.when(jnp.logical_and(j == 0, i == 0))
        def _():
            for c in fetch(be_ref[0], 0):
                c.start()

        for c in fetch(be_ref[i], j):
            c.wait()
        _cast_rows(sg_ref, wgb_ref)
        _cast_rows(su_ref, wub_ref)
        nj = j + last_ref[i]

        @pl.when(nj < n_col_tiles)
        def _():
            for c in fetch(nxte_ref[i], nj):
                c.start()

    @pl.when(i < nused_ref[0])
    def _():
        x = x_ref[...].astype(BF16)
        g = jnp.dot(x, wgb_ref[...], preferred_element_type=F32)
        u = jnp.dot(x, wub_ref[...], preferred_element_type=F32)
        o_ref[...] = (jax.nn.silu(g) * u).astype(o_ref.dtype)

    @pl.when(i >= nused_ref[0])
    def _():
        o_ref[...] = jnp.zeros(o_ref.shape, o_ref.dtype)


def _group_schedule(block_e):
    nb = block_e.shape[0]
    idx = jnp.arange(nb, dtype=jnp.int32)
    first = jnp.concatenate([jnp.ones((1,), bool), block_e[1:] != block_e[:-1]])
    later_first = jnp.logical_and(first[None, :], idx[None, :] > idx[:, None])
    nxt_idx = jnp.min(jnp.where(later_first, idx[None, :], nb), axis=1)
    last = nxt_idx >= nb
    nxt_idx = jnp.where(last, 0, nxt_idx)
    nxt_e = jnp.sum(jnp.where(idx[None, :] == nxt_idx[:, None], block_e[None, :], 0), axis=1)
    return first.astype(jnp.int32), nxt_e.astype(jnp.int32), last.astype(jnp.int32)


def moe_up(xb, w_gate, w_up, layer, block_e, nused, schedule, *, tn=1408):
    p, k = xb.shape
    rows = EXPERT_BLOCK
    f = w_gate.shape[-1]
    assert f % tn == 0 and tn % 128 == 0
    first, nxt_e, last = schedule
    kernel = functools.partial(_moe_up_kernel, layer=layer, tn=tn, n_col_tiles=f // tn)
    hbm = pl.BlockSpec(memory_space=pl.ANY)
    return pl.pallas_call(
        kernel,
        out_shape=jax.ShapeDtypeStruct((p, f), BF16),
        grid_spec=pltpu.PrefetchScalarGridSpec(
            num_scalar_prefetch=5,
            grid=(f // tn, p // rows),
            in_specs=[pl.BlockSpec((rows, k), lambda j, i, *_: (i, 0)), hbm, hbm],
            out_specs=pl.BlockSpec((rows, tn), lambda j, i, *_: (i, j)),
            scratch_shapes=[pltpu.VMEM((k, tn), BF16), pltpu.VMEM((k, tn), BF16),
                            pltpu.VMEM((k, tn), F32), pltpu.VMEM((k, tn), F32),
                            pltpu.SemaphoreType.DMA((2,))],
        ),
        compiler_params=_params(2, 52 << 20),
        name="moe_up",
    )(block_e, nused, first, nxt_e, last, xb, w_gate, w_up)


def _moe_down_kernel(be_ref, nused_ref, first_ref, nxte_ref, last_ref, x_ref, w_hbm, o_ref,
                     wb_ref, st_ref, sem, *, layer):
    i = pl.program_id(0)

    def fetch(e):
        return pltpu.make_async_copy(w_hbm.at[layer, e], st_ref, sem)

    @pl.when(first_ref[i] == 1)
    def _():
        @pl.when(i == 0)
        def _():
            fetch(be_ref[0]).start()

        fetch(be_ref[i]).wait()
        _cast_rows(st_ref, wb_ref)

        @pl.when(last_ref[i] == 0)
        def _():
            fetch(nxte_ref[i]).start()

    @pl.when(i < nused_ref[0])
    def _():
        o_ref[...] = jnp.dot(x_ref[...], wb_ref[...], preferred_element_type=F32)

    @pl.when(i >= nused_ref[0])
    def _():
        o_ref[...] = jnp.zeros(o_ref.shape, o_ref.dtype)


def moe_down(hb, w_down, layer, block_e, nused, schedule):
    p, k = hb.shape
    rows = EXPERT_BLOCK
    n = w_down.shape[-1]
    first, nxt_e, last = schedule
    kernel = functools.partial(_moe_down_kernel, layer=layer)
    return pl.pallas_call(
        kernel,
        out_shape=jax.ShapeDtypeStruct((p, n), F32),
        grid_spec=pltpu.PrefetchScalarGridSpec(
            num_scalar_prefetch=5,
            grid=(p // rows,),
            in_specs=[pl.BlockSpec((rows, k), lambda i, *_: (i, 0)), pl.BlockSpec(memory_space=pl.ANY)],
            out_specs=pl.BlockSpec((rows, n), lambda i, *_: (i, 0)),
            scratch_shapes=[pltpu.VMEM((k, n), BF16), pltpu.VMEM((k, n), F32), pltpu.SemaphoreType.DMA(())],
        ),
        compiler_params=_params(1, 50 << 20),
        name="moe_down",
    )(block_e, nused, first, nxt_e, last, hb, w_down)


COMBINE_T = 256


def _combine_kernel(dest_ref, y_hbm, wt_ref, h_ref, g_ref, b_ref, of_ref, ob_ref,
                    buf_ref, sems, *, n_tokens):
    i = pl.program_id(0)
    t = COMBINE_T

    def issue(block, slot):
        def body(r, carry):
            tok = block * t + r
            _row_copy(y_hbm, dest_ref[tok], buf_ref.at[slot, 0], r, sems.at[slot]).start(priority=0)
            _row_copy(y_hbm, dest_ref[n_tokens + tok], buf_ref.at[slot, 1], r, sems.at[slot]).start(priority=1)
            return carry

        lax.fori_loop(0, t, body, 0, unroll=ISSUE_UNROLL)

    slot = i % 2

    @pl.when(i == 0)
    def _():
        issue(0, 0)

    @pl.when(i + 1 < pl.num_programs(0))
    def _():
        issue(i + 1, 1 - slot)

    for k in range(TOP_K):
        pltpu.make_async_copy(y_hbm.at[pl.ds(0, t)], buf_ref.at[slot, k], sems.at[slot]).wait()
    ff = buf_ref[slot, 0] * wt_ref[:, 0:1] + buf_ref[slot, 1] * wt_ref[:, 1:2]
    y = _layer_norm(DN_ALPHA * h_ref[...] + ff, g_ref[...], b_ref[...])
    of_ref[...] = y
    ob_ref[...] = y.astype(BF16)


def moe_combine_ln(yb, dest, wts_t, h, ln_g, ln_b, layer, which):
    n, d = h.shape
    t = COMBINE_T
    row = pl.BlockSpec((t, d), lambda i, ds: (i, 0))
    par = pl.BlockSpec((None, None, 1, d), lambda i, ds: (layer, which, 0, 0))
    kernel = functools.partial(_combine_kernel, n_tokens=n)
    return pl.pallas_call(
        kernel,
        out_shape=(jax.ShapeDtypeStruct((n, d), F32), jax.ShapeDtypeStruct((n, d), BF16)),
        grid_spec=pltpu.PrefetchScalarGridSpec(
            num_scalar_prefetch=1,
            grid=(n // t,),
            in_specs=[
                pl.BlockSpec(memory_space=pl.ANY),
                pl.BlockSpec((t, TOP_K), lambda i, ds: (i, 0)),
                row, par, par,
            ],
            out_specs=(row, row),
            scratch_shapes=[pltpu.VMEM((2, TOP_K, t, d), F32), pltpu.SemaphoreType.DMA((2,))],
        ),
        compiler_params=_params(1, 32 << 20),
        name="moe_combine_ln",
    )(dest, yb, wts_t, h, ln_g, ln_b)


def moe_layer(h, hb, w_router, w_gate, w_up, w_down, ln_g, ln_b, layer, moe_idx):
    n = h.shape[0]
    m = n * TOP_K
    n_blocks = (m + N_EXPERTS * (EXPERT_BLOCK - 1) + EXPERT_BLOCK - 1) // EXPERT_BLOCK
    e, wts, rank, cnt = moe_router(h, w_router, moe_idx)
    counts = cnt[:, 0]
    padded = (counts + EXPERT_BLOCK - 1) // EXPERT_BLOCK * EXPERT_BLOCK
    pad_ends = jnp.cumsum(padded)
    pad_starts = pad_ends - padded
    expert_ids = jnp.arange(N_EXPERTS, dtype=jnp.int32)[:, None, None]
    dest = jnp.sum(jnp.where(e[None] == expert_ids, pad_starts[:, None, None], 0), axis=0) + rank
    block_start = jnp.arange(n_blocks, dtype=jnp.int32) * EXPERT_BLOCK
    block_e = jnp.minimum(jnp.sum(block_start[:, None] >= pad_ends[None, :], axis=1), N_EXPERTS - 1).astype(jnp.int32)
    nused = (pad_ends[-1:] // EXPERT_BLOCK).astype(jnp.int32)

    dest = dest.reshape(-1)
    pad_rows = jnp.concatenate([pad_starts + counts, pad_ends]).astype(jnp.int32)
    schedule = _group_schedule(block_e)
    xb = moe_dispatch(h, dest, pad_rows, nused, n_blocks)
    hid = moe_up(xb, w_gate, w_up, moe_idx, block_e, nused, schedule)
    yb = moe_down(hid, w_down, moe_idx, block_e, nused, schedule)
    return moe_combine_ln(yb, dest, wts.T, h, ln_g, ln_b, layer, 2)


def kernel(x, mem, rel_table, mix_w_in, gmlp_ln_g, gmlp_ln_b, gmlp_w_s, gmlp_b_s, pool_w, pool_scale, mix_w_out, diff_w_qkv, diff_lam_q1, diff_lam_k1, diff_lam_q2, diff_lam_k2, diff_subln_g, diff_w_o, xa_w_q, xa_w_kv, xa_w_o, ffn_w_gate, ffn_w_up, ffn_w_down, moe_w_router, moe_w_gate, moe_w_up, moe_w_down, ln_g, ln_b):
    batch, seq_len, d = x.shape
    mem_len = mem.shape[1]
    n = batch * seq_len
    h = x.reshape(n, d)
    hb = h
    memf = mem.reshape(batch * mem_len, d)
    ln_g4 = ln_g.reshape(DEPTH, 3, 1, d)
    ln_b4 = ln_b.reshape(DEPTH, 3, 1, d)
    lam_params = jnp.stack([diff_lam_q1, diff_lam_k1, diff_lam_q2, diff_lam_k2], axis=1)
    bias_tiles = attention_bias_tiles(rel_table)
    kv_all = matmul_all_layers(memf, xa_w_kv, tn=1024, out_dtype=BF16)
    xa_qk, xa_vo = cross_attention_folds(kv_all, xa_w_q, xa_w_o, batch, mem_len)

    for layer in range(DEPTH):
        i = layer // 2
        if layer % 2 == 0:
            hm = matmul(hb, mix_w_in, i, tm=512, tn=MIX_IN // 2, out_dtype=F32)
            mixed = gmlp_pool_mixer(hm, seq_len, gmlp_ln_g, gmlp_ln_b, gmlp_w_s, gmlp_b_s, pool_w,
                                    pool_scale, i)
            h, hb = matmul_residual_ln(mixed, mix_w_out, i, h, ln_g4, ln_b4, layer, 0)
        else:
            lam_init = 0.8 - 0.6 * math.exp(-0.3 * layer)
            qkv = matmul(hb, diff_w_qkv, i, tm=1024, tn=1024, out_dtype=BF16,
                         scaled_cols=D, scale=LOG2E * C_HEAD_DIM ** -0.5)
            att = diff_attention(qkv, bias_tiles, lam_params, diff_subln_g, i, lam_init, batch, seq_len)
            h, hb = matmul_residual_ln(att, diff_w_o, i, h, ln_g4, ln_b4, layer, 0)

        h, hb = cross_attention_residual_ln(hb, h, xa_qk, xa_vo, ln_g4, ln_b4, layer, 1, seq_len, mem_len)

        if layer % 2 == 0:
            hid = swiglu_up(hb, ffn_w_gate, ffn_w_up, i, tm=1024, tn=512)
            h, hb = matmul_kacc_residual_ln(hid, ffn_w_down, i, h, ln_g4, ln_b4, layer, 2,
                                            tm=1024, tk=512, single_buffer_rows=True)
        else:
            h, hb = moe_layer(h, hb, moe_w_router, moe_w_gate, moe_w_up, moe_w_down, ln_g4, ln_b4,
                              layer, i)
    return h.reshape(batch, seq_len, d)
```

```python
import functools
import math

import numpy as np
import jax
import jax.numpy as jnp
from jax import lax
from jax.experimental import pallas as pl
from jax.experimental.pallas import tpu as pltpu

F32 = jnp.float32
BF16 = jnp.bfloat16

D = 2048
DEPTH = 4
CHUNK = 64
GMLP_CHUNK = 128
A_WIDTH = D // 2
A_HEADS = 8
A_HEAD_DIM = A_WIDTH // A_HEADS
B_WIDTH = D // 2
POOL_WINDOWS = (2, 4, 8, 16)
B_GROUP = B_WIDTH // len(POOL_WINDOWS)
MIX_IN = 2 * A_WIDTH + B_WIDTH
C_HEADS = 16
C_HEAD_DIM = D // (2 * C_HEADS)
C_V_DIM = 2 * C_HEAD_DIM
REL_BUCKETS = 32
REL_MAX_DIST = 128
X_HEADS = 4
X_HEAD_DIM = D // X_HEADS
N_EXPERTS = 8
TOP_K = 2
EXPERT_BLOCK = 256
DN_ALPHA = (2 * DEPTH) ** 0.25
LN_EPS = 1e-5
NEG = -1e30

VMEM_CAP_BYTES = 56 * 1024 * 1024
CAST_ROWS = 256
ATT_TQ = 256
POOL_HALO = 16


def _params(n_axes, vmem_bytes):
    return pltpu.CompilerParams(
        dimension_semantics=("arbitrary",) * n_axes,
        vmem_limit_bytes=int(min(max(vmem_bytes, 16 * 1024 * 1024), VMEM_CAP_BYTES)),
    )


def _cast_rows(w_ref, wb_ref):
    k = w_ref.shape[0]
    rows = CAST_ROWS if k % CAST_ROWS == 0 else k

    def body(c, carry):
        r = pl.multiple_of(c * rows, rows)
        wb_ref[pl.ds(r, rows), :] = w_ref[pl.ds(r, rows), :].astype(BF16)
        return carry

    lax.fori_loop(0, k // rows, body, 0)


def _layer_norm(x, g, b):
    mu = jnp.mean(x, axis=-1, keepdims=True)
    xc = x - mu
    var = jnp.mean(xc * xc, axis=-1, keepdims=True)
    return xc * lax.rsqrt(var + LN_EPS) * g + b


def _mm_kernel(a_ref, w_ref, o_ref, wb_ref, *, scaled_tiles, scale):
    j = pl.program_id(0)

    @pl.when(pl.program_id(1) == 0)
    def _():
        _cast_rows(w_ref, wb_ref)

    a = a_ref[...].astype(BF16)
    y = jnp.dot(a, wb_ref[...], preferred_element_type=F32)
    if scaled_tiles:
        y = y * jnp.where(j < scaled_tiles, scale, 1.0)
    o_ref[...] = y.astype(o_ref.dtype)


def matmul(a, w, layer, *, tm, tn, out_dtype, scaled_cols=0, scale=1.0):
    m, k = a.shape
    n = w.shape[-1]
    assert m % tm == 0 and n % tn == 0 and w.shape[-2] == k and scaled_cols % tn == 0
    vmem = 2 * tm * k * a.dtype.itemsize + 2 * k * tn * 4 + k * tn * 2 + 2 * tm * tn * 4 + tm * tn * 4
    kernel = functools.partial(_mm_kernel, scaled_tiles=scaled_cols // tn, scale=scale)
    return pl.pallas_call(
        kernel,
        out_shape=jax.ShapeDtypeStruct((m, n), out_dtype),
        grid=(n // tn, m // tm),
        in_specs=[
            pl.BlockSpec((tm, k), lambda j, i: (i, 0)),
            pl.BlockSpec((None, k, tn), lambda j, i: (layer, 0, j)),
        ],
        out_specs=pl.BlockSpec((tm, tn), lambda j, i: (i, j)),
        scratch_shapes=[pltpu.VMEM((k, tn), BF16)],
        compiler_params=_params(2, vmem + (4 << 20)),
        name="matmul",
    )(a, w)


MMLN_GROUPS = 2


def _mm_ln_kernel(a_ref, w_ref, h_ref, g_ref, b_ref, of_ref, ob_ref, wb_ref):
    @pl.when(pl.program_id(0) == 0)
    def _():
        _cast_rows(w_ref, wb_ref)

    rows = a_ref.shape[0] // MMLN_GROUPS
    for r in range(MMLN_GROUPS):
        sl = slice(r * rows, (r + 1) * rows)
        y = jnp.dot(a_ref[sl, :], wb_ref[...], preferred_element_type=F32)
        out = _layer_norm(DN_ALPHA * h_ref[sl, :] + y, g_ref[...], b_ref[...])
        of_ref[sl, :] = out
        ob_ref[sl, :] = out.astype(BF16)


def matmul_residual_ln(a, w, w_layer, h, ln_g, ln_b, layer, which, *, tm=512):
    m, k = a.shape
    d = w.shape[-1]
    assert m % tm == 0 and h.shape == (m, d)
    row = pl.BlockSpec((tm, d), lambda i: (i, 0))
    par = pl.BlockSpec((None, None, 1, d), lambda i: (layer, which, 0, 0))
    vmem = k * d * 4 + k * d * 2 + 2 * tm * k * 2 + 4 * tm * d * 4 + 2 * tm * d * 2 + 3 * tm * d * 4
    return pl.pallas_call(
        _mm_ln_kernel,
        out_shape=(jax.ShapeDtypeStruct((m, d), F32), jax.ShapeDtypeStruct((m, d), BF16)),
        grid=(m // tm,),
        in_specs=[
            pl.BlockSpec((tm, k), lambda i: (i, 0)),
            pl.BlockSpec((None, k, d), lambda i: (w_layer, 0, 0), pipeline_mode=pl.Buffered(1)),
            row, par, par,
        ],
        out_specs=(row, row),
        scratch_shapes=[pltpu.VMEM((k, d), BF16)],
        compiler_params=_params(1, vmem + (4 << 20)),
        name="matmul_residual_ln",
    )(a, w, h, ln_g, ln_b)


KACC_GROUP_ROWS = 256


def _mm_kacc_ln_kernel(a_ref, w_ref, h_ref, g_ref, b_ref, of_ref, ob_ref, acc_ref):
    k = pl.program_id(1)

    @pl.when(jnp.logical_and(pl.program_id(0) == 0, k == 0))
    def _():
        acc_ref[...] = jnp.zeros(acc_ref.shape, F32)

    part = jnp.dot(a_ref[...], w_ref[...].astype(BF16), preferred_element_type=F32)
    acc_ref[...] = jnp.where(k > 0, acc_ref[...], 0.0) + part

    @pl.when(k == pl.num_programs(1) - 1)
    def _():
        for r in range(0, acc_ref.shape[0], KACC_GROUP_ROWS):
            sl = slice(r, r + KACC_GROUP_ROWS)
            out = _layer_norm(DN_ALPHA * h_ref[sl, :] + acc_ref[sl, :], g_ref[...], b_ref[...])
            of_ref[sl, :] = out
            ob_ref[sl, :] = out.astype(BF16)


def matmul_kacc_residual_ln(a, w, w_layer, h, ln_g, ln_b, layer, which, *, tm, tk, single_buffer_rows):
    m, kk = a.shape
    d = w.shape[-1]
    assert m % tm == 0 and kk % tk == 0 and tm % KACC_GROUP_ROWS == 0
    mode = dict(pipeline_mode=pl.Buffered(1)) if single_buffer_rows else {}
    row = pl.BlockSpec((tm, d), lambda i, k: (i, 0))
    out_row = pl.BlockSpec((tm, d), lambda i, k: (i, 0), **mode)
    par = pl.BlockSpec((None, None, 1, d), lambda i, k: (layer, which, 0, 0))
    return pl.pallas_call(
        _mm_kacc_ln_kernel,
        out_shape=(jax.ShapeDtypeStruct((m, d), F32), jax.ShapeDtypeStruct((m, d), BF16)),
        grid=(m // tm, kk // tk),
        in_specs=[
            pl.BlockSpec((tm, tk), lambda i, k: (i, k)),
            pl.BlockSpec((None, tk, d), lambda i, k: (w_layer, k, 0)),
            row, par, par,
        ],
        out_specs=(out_row, out_row),
        scratch_shapes=[pltpu.VMEM((tm, d), F32)],
        compiler_params=_params(2, VMEM_CAP_BYTES),
        name="matmul_kacc_residual_ln",
    )(a, w, h, ln_g, ln_b)


def _mm_layers_kernel(a_ref, w_ref, o_ref):
    o_ref[...] = jnp.dot(a_ref[...].astype(BF16), w_ref[...].astype(BF16),
                         preferred_element_type=F32).astype(o_ref.dtype)


def matmul_all_layers(a, w, *, tn, out_dtype):
    m, k = a.shape
    layers, _, n = w.shape
    assert n % tn == 0
    return pl.pallas_call(
        _mm_layers_kernel,
        out_shape=jax.ShapeDtypeStruct((layers, m, n), out_dtype),
        grid=(layers, n // tn),
        in_specs=[
            pl.BlockSpec((m, k), lambda l, j: (0, 0)),
            pl.BlockSpec((None, k, tn), lambda l, j: (l, 0, j)),
        ],
        out_specs=pl.BlockSpec((None, m, tn), lambda l, j: (l, 0, j)),
        compiler_params=_params(2, 2 * m * k * 4 + 3 * k * tn * 4 + 3 * m * tn * 4 + (4 << 20)),
        name="matmul_all_layers",
    )(a, w)


def _swiglu_kernel(a_ref, wg_ref, wu_ref, o_ref, wgb_ref, wub_ref):
    @pl.when(pl.program_id(1) == 0)
    def _():
        _cast_rows(wg_ref, wgb_ref)
        _cast_rows(wu_ref, wub_ref)

    a = a_ref[...]
    g = jnp.dot(a, wgb_ref[...], preferred_element_type=F32)
    u = jnp.dot(a, wub_ref[...], preferred_element_type=F32)
    o_ref[...] = (jax.nn.silu(g) * u).astype(o_ref.dtype)


def swiglu_up(a, wg, wu, layer, *, tm, tn):
    m, k = a.shape
    n = wg.shape[-1]
    assert m % tm == 0 and n % tn == 0
    vmem = 2 * tm * k * 2 + 4 * k * tn * 4 + 2 * k * tn * 2 + 2 * tm * tn * 2 + 3 * tm * tn * 4
    wspec = pl.BlockSpec((None, k, tn), lambda j, i: (layer, 0, j))
    return pl.pallas_call(
        _swiglu_kernel,
        out_shape=jax.ShapeDtypeStruct((m, n), BF16),
        grid=(n // tn, m // tm),
        in_specs=[pl.BlockSpec((tm, k), lambda j, i: (i, 0)), wspec, wspec],
        out_specs=pl.BlockSpec((tm, tn), lambda j, i: (i, j)),
        scratch_shapes=[pltpu.VMEM((k, tn), BF16), pltpu.VMEM((k, tn), BF16)],
        compiler_params=_params(2, vmem + (4 << 20)),
        name="swiglu_up",
    )(a, wg, wu)


MIX_ROWS = 2 * GMLP_CHUNK


def _mixer_kernel(hm_ref, prev_ref, lng_ref, lnb_ref, ws_ref, bst_ref, pw_ref, ps_ref, o_ref,
                  wbd_ref, ext_ref, dbl_ref, *, blocks_per_seq):
    i = pl.program_id(0)

    @pl.when(i == 0)
    def _():
        p = lax.broadcasted_iota(jnp.int32, (GMLP_CHUNK, GMLP_CHUNK), 0)
        q = lax.broadcasted_iota(jnp.int32, (GMLP_CHUNK, GMLP_CHUNK), 1)
        allowed = (q // CHUNK) <= (p // CHUNK)
        wbd_ref[...] = jnp.zeros(wbd_ref.shape, BF16)
        for h in range(A_HEADS):
            w = jnp.where(allowed, ws_ref[h], 0.0).astype(BF16)
            wbd_ref[h, 0:GMLP_CHUNK, 0:GMLP_CHUNK] = w
            wbd_ref[h, GMLP_CHUNK:MIX_ROWS, GMLP_CHUNK:MIX_ROWS] = w

    z = jax.nn.gelu(hm_ref[:, 0:2 * A_WIDTH])
    u = z[:, 0:A_WIDTH]
    v = _layer_norm(z[:, A_WIDTH:2 * A_WIDTH], lng_ref[...], lnb_ref[...]).astype(BF16)
    for h in range(A_HEADS):
        c0 = h * A_HEAD_DIM
        sv = jnp.dot(wbd_ref[h], v[:, c0:c0 + A_HEAD_DIM], preferred_element_type=F32)
        sv = sv + bst_ref[:, h:h + 1]
        o_ref[:, c0:c0 + A_HEAD_DIM] = (u[:, c0:c0 + A_HEAD_DIM] * sv).astype(BF16)

    blk = i % blocks_per_seq
    prev = jnp.where(blk == 0, 0.0, prev_ref[...])
    ext_ref[0:POOL_HALO, :] = prev
    ext_ref[POOL_HALO:POOL_HALO + MIX_ROWS, :] = hm_ref[:, 2 * A_WIDTH:MIX_IN]
    t1 = blk * MIX_ROWS + lax.broadcasted_iota(jnp.int32, (MIX_ROWS, 1), 0) + 1
    for g, w in enumerate(POOL_WINDOWS):
        c0 = g * B_GROUP
        cols = slice(c0, c0 + B_GROUP)
        pin = ext_ref[POOL_HALO:POOL_HALO + MIX_ROWS, cols]
        src_ref, src_cols = ext_ref, cols
        k = 1
        while k < w:
            lo = 2 * k - 1
            n = POOL_HALO + MIX_ROWS - lo
            dbl_ref[lo:lo + n, :] = src_ref[lo:lo + n, src_cols] + src_ref[lo - k:lo - k + n, src_cols]
            src_ref, src_cols = dbl_ref, slice(None)
            k *= 2
        s = dbl_ref[POOL_HALO:POOL_HALO + MIX_ROWS, :]
        cnt = jnp.minimum(t1, w).astype(F32)
        pooled = s / cnt - pin
        b = jnp.dot(pooled.astype(BF16), pw_ref[g].astype(BF16), preferred_element_type=F32)
        b = b * ps_ref[:, c0:c0 + B_GROUP]
        o_ref[:, A_WIDTH + c0:A_WIDTH + c0 + B_GROUP] = b.astype(BF16)


def gmlp_pool_mixer(hm, seq_len, ln_g, ln_b, w_s, b_s, pool_w, pool_scale, layer):
    n = hm.shape[0]
    assert seq_len % MIX_ROWS == 0 and max(POOL_WINDOWS) <= POOL_HALO
    blocks_per_seq = seq_len // MIX_ROWS
    halo_per_block = MIX_ROWS // POOL_HALO
    bst = jnp.tile(b_s[layer].T, (MIX_ROWS // GMLP_CHUNK, 1))
    kernel = functools.partial(_mixer_kernel, blocks_per_seq=blocks_per_seq)
    return pl.pallas_call(
        kernel,
        out_shape=jax.ShapeDtypeStruct((n, D), BF16),
        grid=(n // MIX_ROWS,),
        in_specs=[
            pl.BlockSpec((MIX_ROWS, MIX_IN), lambda i: (i, 0)),
            pl.BlockSpec((POOL_HALO, B_WIDTH),
                         lambda i: (jnp.maximum(i * halo_per_block - 1, 0), 2 * A_WIDTH // B_WIDTH)),
            pl.BlockSpec((None, 1, A_WIDTH), lambda i: (layer, 0, 0)),
            pl.BlockSpec((None, 1, A_WIDTH), lambda i: (layer, 0, 0)),
            pl.BlockSpec((None, A_HEADS, GMLP_CHUNK, GMLP_CHUNK), lambda i: (layer, 0, 0, 0)),
            pl.BlockSpec((MIX_ROWS, A_HEADS), lambda i: (0, 0)),
            pl.BlockSpec((None, len(POOL_WINDOWS), B_GROUP, B_GROUP), lambda i: (layer, 0, 0, 0)),
            pl.BlockSpec((None, 1, B_WIDTH), lambda i: (layer, 0, 0)),
        ],
        out_specs=pl.BlockSpec((MIX_ROWS, D), lambda i: (i, 0)),
        scratch_shapes=[
            pltpu.VMEM((A_HEADS, MIX_ROWS, MIX_ROWS), BF16),
            pltpu.VMEM((POOL_HALO + MIX_ROWS, B_WIDTH), F32),
            pltpu.VMEM((POOL_HALO + MIX_ROWS, B_GROUP), F32),
        ],
        compiler_params=_params(1, 40 << 20),
        name="gmlp_pool_mixer",
    )(hm, hm, ln_g.reshape(-1, 1, A_WIDTH), ln_b.reshape(-1, 1, A_WIDTH), w_s, bst, pool_w,
      pool_scale.reshape(-1, 1, B_WIDTH))


def _t5_bucket_np(rel):
    half = REL_BUCKETS // 2
    max_exact = half // 2
    rel = np.asarray(rel, dtype=np.int64)
    n = np.abs(rel)
    steps = half - max_exact
    ratio = REL_MAX_DIST // max_exact
    large = np.zeros_like(n)
    nn = n.astype(object) ** steps
    for j in range(1, steps + 1):
        large = large + (nn >= (max_exact ** steps) * (ratio ** j)).astype(np.int64)
    large = np.minimum(max_exact + large, half - 1)
    return np.where(rel > 0, half, 0) + np.where(n < max_exact, n, large)


def _bias_bucket_tiles():
    qi = np.arange(ATT_TQ)[:, None]
    kj = np.arange(ATT_TQ)[None, :]
    diag = _t5_bucket_np(kj - qi)
    diag = np.where((kj // CHUNK) <= (qi // CHUNK), diag, -1)
    prev = _t5_bucket_np(kj - ATT_TQ - qi)
    return np.stack([diag, prev], 0).astype(np.int32)


FAR_BUCKET = REL_BUCKETS // 2 - 1
LOG2E = math.log2(math.e)


def _bias_kernel(rel_ref, bk_ref, o_ref):
    h = pl.program_id(0)
    bk = bk_ref[...]
    acc = jnp.zeros(bk.shape, F32)
    for b in range(REL_BUCKETS):
        acc = jnp.where(bk == b, rel_ref[b, h], acc)
    o_ref[...] = jnp.where(bk < 0, NEG, (acc - rel_ref[FAR_BUCKET, h]) * LOG2E)


def attention_bias_tiles(rel_table):
    assert ATT_TQ >= REL_MAX_DIST and ATT_TQ % CHUNK == 0
    bk = jnp.asarray(_bias_bucket_tiles())
    return pl.pallas_call(
        _bias_kernel,
        out_shape=jax.ShapeDtypeStruct((C_HEADS, 2, ATT_TQ, ATT_TQ), F32),
        grid=(C_HEADS,),
        in_specs=[
            pl.BlockSpec(memory_space=pltpu.SMEM),
            pl.BlockSpec((2, ATT_TQ, ATT_TQ), lambda h: (0, 0, 0)),
        ],
        out_specs=pl.BlockSpec((None, 2, ATT_TQ, ATT_TQ), lambda h: (h, 0, 0, 0)),
        compiler_params=_params(1, 16 << 20),
        name="attention_bias_tiles",
    )(rel_table, bk)


ATT_HEADS_PER_STEP = 2


def _diff_attn_kernel(q_ref, k_ref, v_ref, bias_ref, lam_ref, g_ref, o_ref, vx_ref, *, lam_init, seq_len):
    tq = ATT_TQ
    nt = (((1,), (1,)), ((), ()))
    lp = lam_ref[...]
    lam = (jnp.exp(jnp.sum(lp[0:1] * lp[1:2], axis=1, keepdims=True))
           - jnp.exp(jnp.sum(lp[2:3] * lp[3:4], axis=1, keepdims=True)) + lam_init)
    lane = lax.broadcasted_iota(jnp.int32, (tq, C_V_DIM), 1)
    for hh in range(ATT_HEADS_PER_STEP):
        hc = slice(hh * C_V_DIM, (hh + 1) * C_V_DIM)
        vx_ref[hh, :, 0:C_V_DIM] = v_ref[:, hc]
        vx_ref[hh, :, C_V_DIM:2 * C_V_DIM] = jnp.ones((seq_len, C_V_DIM), BF16)
        for qi in range(seq_len // tq):
            q = q_ref[qi * tq:(qi + 1) * tq, hc]
            zero = jnp.zeros_like(q)
            parts = []
            if qi >= 2:
                parts.append((0, (qi - 1) * tq, None))
            if qi >= 1:
                parts.append(((qi - 1) * tq, qi * tq, 1))
            parts.append((qi * tq, (qi + 1) * tq, 0))
            outs = []
            for half in range(2):
                keep = (lane < C_HEAD_DIM) if half == 0 else (lane >= C_HEAD_DIM)
                qh = jnp.where(keep, q, zero)
                scores = []
                m = None
                for a, b, bi in parts:
                    s = lax.dot_general(qh, k_ref[a:b, hc], nt, preferred_element_type=F32)
                    if bi is not None:
                        s = s + bias_ref[hh, bi]
                    scores.append(s)
                    sm = jnp.max(s, axis=1, keepdims=True)
                    m = sm if m is None else jnp.maximum(m, sm)
                acc = None
                for s, (a, b, _) in zip(scores, parts):
                    p = jnp.exp2(s - m).astype(BF16)
                    pv = jnp.dot(p, vx_ref[hh, a:b, :], preferred_element_type=F32)
                    acc = pv if acc is None else acc + pv
                outs.append(acc[:, 0:C_V_DIM] / acc[:, C_V_DIM:C_V_DIM + 1])
            o = outs[0] - lam * outs[1]
            y = o * lax.rsqrt(jnp.mean(o * o, axis=-1, keepdims=True) + LN_EPS) * g_ref[...]
            o_ref[qi * tq:(qi + 1) * tq, hc] = (y * (1.0 - lam_init)).astype(BF16)


def diff_attention(qkv, bias_tiles, lam_params, subln_g, layer, lam_init, batch, seq_len):
    n = qkv.shape[0]
    tq = ATT_TQ
    hps = ATT_HEADS_PER_STEP
    assert seq_len % tq == 0 and C_HEADS % hps == 0
    groups = C_HEADS // hps
    kernel = functools.partial(_diff_attn_kernel, lam_init=lam_init, seq_len=seq_len)
    seq = lambda off: pl.BlockSpec((seq_len, hps * C_V_DIM), lambda b, h: (b, off + h))
    return pl.pallas_call(
        kernel,
        out_shape=jax.ShapeDtypeStruct((n, D), BF16),
        grid=(batch, groups),
        in_specs=[
            seq(0), seq(groups), seq(2 * groups),
            pl.BlockSpec((hps, 2, tq, tq), lambda b, h: (h, 0, 0, 0)),
            pl.BlockSpec((None, 4, C_HEAD_DIM), lambda b, h: (layer, 0, 0)),
            pl.BlockSpec((None, 1, C_V_DIM), lambda b, h: (layer, 0, 0)),
        ],
        out_specs=seq(0),
        scratch_shapes=[pltpu.VMEM((hps, seq_len, 2 * C_V_DIM), BF16)],
        compiler_params=_params(2, 48 << 20),
        name="diff_attention",
    )(qkv, qkv, qkv, bias_tiles, lam_params, subln_g.reshape(-1, 1, C_V_DIM))


def _qk_fold_kernel(wq_ref, k_ref, o_ref, *, batch, mem_len):
    wb = wq_ref[...].astype(BF16)
    for b in range(batch):
        qk = lax.dot_general(wb, k_ref[b * mem_len:(b + 1) * mem_len, :], (((1,), (1,)), ((), ())),
                             preferred_element_type=F32)
        o_ref[b] = (qk * (LOG2E * X_HEAD_DIM ** -0.5)).astype(o_ref.dtype)


def _vo_fold_kernel(v_ref, wo_ref, o_ref, *, batch, mem_len):
    wb = wo_ref[...].astype(BF16)
    for b in range(batch):
        o_ref[b] = jnp.dot(v_ref[b * mem_len:(b + 1) * mem_len, :], wb,
                           preferred_element_type=F32).astype(o_ref.dtype)


def cross_attention_folds(kv_all, w_q, w_o, batch, mem_len):
    layers = kv_all.shape[0]
    grid = (layers, X_HEADS)
    qk = pl.pallas_call(
        functools.partial(_qk_fold_kernel, batch=batch, mem_len=mem_len),
        out_shape=jax.ShapeDtypeStruct((layers, batch, D, X_HEADS * mem_len), BF16),
        grid=grid,
        in_specs=[
            pl.BlockSpec((None, D, X_HEAD_DIM), lambda l, h: (l, 0, h)),
            pl.BlockSpec((None, batch * mem_len, X_HEAD_DIM), lambda l, h: (l, 0, h)),
        ],
        out_specs=pl.BlockSpec((None, batch, D, mem_len), lambda l, h: (l, 0, 0, h)),
        compiler_params=_params(2, 32 << 20),
        name="cross_attention_qk_fold",
    )(w_q, kv_all)
    vo = pl.pallas_call(
        functools.partial(_vo_fold_kernel, batch=batch, mem_len=mem_len),
        out_shape=jax.ShapeDtypeStruct((layers, batch, X_HEADS * mem_len, D), BF16),
        grid=grid,
        in_specs=[
            pl.BlockSpec((None, batch * mem_len, X_HEAD_DIM), lambda l, h: (l, 0, X_HEADS + h)),
            pl.BlockSpec((None, X_HEAD_DIM, D), lambda l, h: (l, h, 0)),
        ],
        out_specs=pl.BlockSpec((None, batch, mem_len, D), lambda l, h: (l, 0, h, 0)),
        compiler_params=_params(2, 32 << 20),
        name="cross_attention_vo_fold",
    )(kv_all, w_o)
    return qk, vo


XATTN_GROUPS = 2


def _xattn_ln_kernel(hb_ref, qk_ref, vo_ref, h_ref, g_ref, b_ref, of_ref, ob_ref, *, mem_len):
    rows = hb_ref.shape[0] // XATTN_GROUPS
    for r in range(XATTN_GROUPS):
        sl = slice(r * rows, (r + 1) * rows)
        s = jnp.dot(hb_ref[sl, :], qk_ref[...], preferred_element_type=F32)
        probs = []
        for hd in range(X_HEADS):
            sh = s[:, hd * mem_len:(hd + 1) * mem_len]
            e = jnp.exp2(sh - jnp.max(sh, axis=1, keepdims=True))
            inv = 1.0 / jnp.sum(e, axis=1, keepdims=True)
            probs.append((e * inv).astype(BF16))
        p = jnp.concatenate(probs, axis=1)
        y = jnp.dot(p, vo_ref[...], preferred_element_type=F32)
        out = _layer_norm(DN_ALPHA * h_ref[sl, :] + y, g_ref[...], b_ref[...])
        of_ref[sl, :] = out
        ob_ref[sl, :] = out.astype(BF16)


def cross_attention_residual_ln(hb, h, qk, vo, ln_g, ln_b, layer, which, seq_len, mem_len, *, tm=512):
    n, d = h.shape
    assert seq_len % tm == 0 and tm % XATTN_GROUPS == 0
    blocks_per_seq = seq_len // tm
    row = pl.BlockSpec((tm, d), lambda i: (i, 0))
    par = pl.BlockSpec((None, None, 1, d), lambda i: (layer, which, 0, 0))
    kw = X_HEADS * mem_len
    kernel = functools.partial(_xattn_ln_kernel, mem_len=mem_len)
    return pl.pallas_call(
        kernel,
        out_shape=(jax.ShapeDtypeStruct((n, d), F32), jax.ShapeDtypeStruct((n, d), BF16)),
        grid=(n // tm,),
        in_specs=[
            row,
            pl.BlockSpec((None, None, d, kw), lambda i: (layer, i // blocks_per_seq, 0, 0)),
            pl.BlockSpec((None, None, kw, d), lambda i: (layer, i // blocks_per_seq, 0, 0)),
            row, par, par,
        ],
        out_specs=(row, row),
        compiler_params=_params(1, 52 << 20),
        name="cross_attention_residual_ln",
    )(hb, qk, vo, h, ln_g, ln_b)


ROUTE_T = 256


def _router_kernel(h_ref, wrt_ref, e_ref, w_ref, rank_ref, cnt_ref, base_ref):
    i = pl.program_id(0)

    @pl.when(i == 0)
    def _():
        base_ref[...] = jnp.zeros(base_ref.shape, F32)

    t = ROUTE_T
    nt = (((1,), (1,)), ((), ()))
    hf = h_ref[...]
    wf = wrt_ref[...]
    hh = hf.astype(BF16)
    wh = wf.astype(BF16)
    hl = (hf - hh.astype(F32)).astype(BF16)
    wl = (wf - wh.astype(F32)).astype(BF16)
    logits = (lax.dot_general(wh, hh, nt, preferred_element_type=F32)
              + lax.dot_general(wh, hl, nt, preferred_element_type=F32)
              + lax.dot_general(wl, hh, nt, preferred_element_type=F32))
    eidx = lax.broadcasted_iota(jnp.int32, (N_EXPERTS, t), 0)
    m1 = jnp.max(logits, axis=0, keepdims=True)
    e1 = jnp.min(jnp.where(logits == m1, eidx, N_EXPERTS), axis=0, keepdims=True)
    oh1 = eidx == e1
    rest = jnp.where(oh1, -jnp.inf, logits)
    m2 = jnp.max(rest, axis=0, keepdims=True)
    e2 = jnp.min(jnp.where(rest == m2, eidx, N_EXPERTS), axis=0, keepdims=True)
    oh2 = eidx == e2
    ex = jnp.exp(m2 - m1)
    den = 1.0 + ex
    e_ref[0:1, :] = e1
    e_ref[1:2, :] = e2
    w_ref[0:1, :] = 1.0 / den
    w_ref[1:2, :] = ex / den

    oh = (oh1.astype(F32) + oh2.astype(F32)).astype(BF16)
    r = lax.broadcasted_iota(jnp.int32, (t, t), 0)
    c = lax.broadcasted_iota(jnp.int32, (t, t), 1)
    before = (r < c).astype(BF16)
    ones = jnp.ones((t, t), BF16)
    pos = base_ref[...] + jnp.dot(oh, before, preferred_element_type=F32)
    rank_ref[0:1, :] = jnp.sum(jnp.where(oh1, pos, 0.0), axis=0, keepdims=True).astype(jnp.int32)
    rank_ref[1:2, :] = jnp.sum(jnp.where(oh2, pos, 0.0), axis=0, keepdims=True).astype(jnp.int32)
    total = base_ref[...] + jnp.dot(oh, ones, preferred_element_type=F32)
    base_ref[...] = total
    cnt_ref[...] = total[:, 0:128].astype(jnp.int32)


def moe_router(h, w_router, layer):
    n = h.shape[0]
    t = ROUTE_T
    wrt = jnp.swapaxes(w_router, 1, 2)
    pair = pl.BlockSpec((TOP_K, t), lambda i: (0, i))
    return pl.pallas_call(
        _router_kernel,
        out_shape=(
            jax.ShapeDtypeStruct((TOP_K, n), jnp.int32),
            jax.ShapeDtypeStruct((TOP_K, n), F32),
            jax.ShapeDtypeStruct((TOP_K, n), jnp.int32),
            jax.ShapeDtypeStruct((N_EXPERTS, 128), jnp.int32),
        ),
        grid=(n // t,),
        in_specs=[
            pl.BlockSpec((t, D), lambda i: (i, 0)),
            pl.BlockSpec((None, N_EXPERTS, D), lambda i: (layer, 0, 0)),
        ],
        out_specs=(pair, pair, pair, pl.BlockSpec((N_EXPERTS, 128), lambda i: (0, 0))),
        scratch_shapes=[pltpu.VMEM((N_EXPERTS, t), F32)],
        compiler_params=_params(1, 16 << 20),
        name="moe_router",
    )(h, wrt)


def _row_copy(src_ref, src_row, dst_ref, dst_row, sem):
    return pltpu.make_async_copy(src_ref.at[pl.ds(src_row, 1)], dst_ref.at[pl.ds(dst_row, 1)], sem)


PREFETCH_PRIORITY = 1
DISPATCH_T = 512
ISSUE_UNROLL = 8


def _dispatch_kernel(dest_ref, pad_ref, nused_ref, h_ref, xb_hbm, zero_ref, sem, zsem, *, n_tokens, n_blocks):
    i = pl.program_id(0)
    t = DISPATCH_T
    rows = EXPERT_BLOCK

    @pl.when(i == 0)
    def _():
        zero_ref[...] = jnp.zeros(zero_ref.shape, F32)

        def zero_row(r, carry):
            _row_copy(zero_ref, 0, xb_hbm, r, zsem).start()
            return carry

        def wait_row(r, carry):
            _row_copy(zero_ref, 0, xb_hbm, 0, zsem).wait()
            return carry

        def block_copy(blk):
            return pltpu.make_async_copy(zero_ref, xb_hbm.at[pl.ds(pl.multiple_of(blk * rows, rows), rows)], zsem)

        def zero_block(blk, carry):
            block_copy(blk).start()
            return carry

        def wait_block(blk, carry):
            block_copy(0).wait()
            return carry

        for e in range(N_EXPERTS):
            lax.fori_loop(pad_ref[e], pad_ref[N_EXPERTS + e], zero_row, 0)
        lax.fori_loop(nused_ref[0], n_blocks, zero_block, 0)
        for e in range(N_EXPERTS):
            lax.fori_loop(pad_ref[e], pad_ref[N_EXPERTS + e], wait_row, 0)
        lax.fori_loop(nused_ref[0], n_blocks, wait_block, 0)

    def start(r, carry):
        tok = i * t + r
        _row_copy(h_ref, r, xb_hbm, dest_ref[tok], sem).start(priority=0)
        _row_copy(h_ref, r, xb_hbm, dest_ref[n_tokens + tok], sem).start(priority=1)
        return carry

    lax.fori_loop(0, t, start, 0, unroll=ISSUE_UNROLL)
    for _ in range(TOP_K):
        pltpu.make_async_copy(h_ref, xb_hbm.at[pl.ds(0, t)], sem).wait()


def moe_dispatch(h, dest, pad_rows, nused, n_blocks):
    n, d = h.shape
    t = DISPATCH_T
    rows = n_blocks * EXPERT_BLOCK
    kernel = functools.partial(_dispatch_kernel, n_tokens=n, n_blocks=n_blocks)
    return pl.pallas_call(
        kernel,
        out_shape=jax.ShapeDtypeStruct((rows, d), F32),
        grid_spec=pltpu.PrefetchScalarGridSpec(
            num_scalar_prefetch=3,
            grid=(n // t,),
            in_specs=[pl.BlockSpec((t, d), lambda i, *_: (i, 0))],
            out_specs=pl.BlockSpec(memory_space=pl.ANY),
            scratch_shapes=[pltpu.VMEM((EXPERT_BLOCK, d), F32), pltpu.SemaphoreType.DMA(()),
                            pltpu.SemaphoreType.DMA(())],
        ),
        compiler_params=_params(1, 24 << 20),
        name="moe_dispatch",
    )(dest, pad_rows, nused, h)


def _moe_up_kernel(be_ref, nused_ref, first_ref, nxte_ref, last_ref, x_ref, wg_hbm, wu_hbm, o_ref,
                   wgb_ref, wub_ref, sg_ref, su_ref, sems, *, layer, tn, n_col_tiles):
    j = pl.program_id(0)
    i = pl.program_id(1)

    def fetch(e, jj):
        cols = pl.ds(pl.multiple_of(jj * tn, 128), tn)
        return (pltpu.make_async_copy(wg_hbm.at[layer, e, :, cols], sg_ref, sems.at[0]),
                pltpu.make_async_copy(wu_hbm.at[layer, e, :, cols], su_ref, sems.at[1]))

    @pl.when(first_ref[i] == 1)
    def _():
        @pl.when(jnp.logical_and(j == 0, i == 0))
        def _():
            for c in fetch(be_ref[0], 0):
                c.start()

        for c in fetch(be_ref[i], j):
            c.wait()
        _cast_rows(sg_ref, wgb_ref)
        _cast_rows(su_ref, wub_ref)
        nj = j + last_ref[i]

        @pl.when(nj < n_col_tiles)
        def _():
            for c in fetch(nxte_ref[i], nj):
                c.start(priority=PREFETCH_PRIORITY)

    @pl.when(i < nused_ref[0])
    def _():
        x = x_ref[...].astype(BF16)
        g = jnp.dot(x, wgb_ref[...], preferred_element_type=F32)
        u = jnp.dot(x, wub_ref[...], preferred_element_type=F32)
        o_ref[...] = (jax.nn.silu(g) * u).astype(o_ref.dtype)

    @pl.when(i >= nused_ref[0])
    def _():
        o_ref[...] = jnp.zeros(o_ref.shape, o_ref.dtype)


def _group_schedule(block_e):
    nb = block_e.shape[0]
    idx = jnp.arange(nb, dtype=jnp.int32)
    first = jnp.concatenate([jnp.ones((1,), bool), block_e[1:] != block_e[:-1]])
    later_first = jnp.logical_and(first[None, :], idx[None, :] > idx[:, None])
    nxt_idx = jnp.min(jnp.where(later_first, idx[None, :], nb), axis=1)
    last = nxt_idx >= nb
    nxt_idx = jnp.where(last, 0, nxt_idx)
    nxt_e = jnp.sum(jnp.where(idx[None, :] == nxt_idx[:, None], block_e[None, :], 0), axis=1)
    return first.astype(jnp.int32), nxt_e.astype(jnp.int32), last.astype(jnp.int32)


def moe_up(xb, w_gate, w_up, layer, block_e, nused, schedule, *, tn=1408):
    p, k = xb.shape
    rows = EXPERT_BLOCK
    f = w_gate.shape[-1]
    assert f % tn == 0 and tn % 128 == 0
    first, nxt_e, last = schedule
    kernel = functools.partial(_moe_up_kernel, layer=layer, tn=tn, n_col_tiles=f // tn)
    hbm = pl.BlockSpec(memory_space=pl.ANY)
    return pl.pallas_call(
        kernel,
        out_shape=jax.ShapeDtypeStruct((p, f), BF16),
        grid_spec=pltpu.PrefetchScalarGridSpec(
            num_scalar_prefetch=5,
            grid=(f // tn, p // rows),
            in_specs=[pl.BlockSpec((rows, k), lambda j, i, *_: (i, 0)), hbm, hbm],
            out_specs=pl.BlockSpec((rows, tn), lambda j, i, *_: (i, j)),
            scratch_shapes=[pltpu.VMEM((k, tn), BF16), pltpu.VMEM((k, tn), BF16),
                            pltpu.VMEM((k, tn), F32), pltpu.VMEM((k, tn), F32),
                            pltpu.SemaphoreType.DMA((2,))],
        ),
        compiler_params=_params(2, 52 << 20),
        name="moe_up",
    )(block_e, nused, first, nxt_e, last, xb, w_gate, w_up)


def _moe_down_kernel(be_ref, nused_ref, first_ref, nxte_ref, last_ref, x_ref, w_hbm, o_ref,
                     wb_ref, st_ref, sem, *, layer):
    i = pl.program_id(0)

    def fetch(e):
        return pltpu.make_async_copy(w_hbm.at[layer, e], st_ref, sem)

    @pl.when(first_ref[i] == 1)
    def _():
        @pl.when(i == 0)
        def _():
            fetch(be_ref[0]).start()

        fetch(be_ref[i]).wait()
        _cast_rows(st_ref, wb_ref)

        @pl.when(last_ref[i] == 0)
        def _():
            fetch(nxte_ref[i]).start(priority=PREFETCH_PRIORITY)

    @pl.when(i < nused_ref[0])
    def _():
        o_ref[...] = jnp.dot(x_ref[...], wb_ref[...], preferred_element_type=F32)

    @pl.when(i >= nused_ref[0])
    def _():
        o_ref[...] = jnp.zeros(o_ref.shape, o_ref.dtype)


def moe_down(hb, w_down, layer, block_e, nused, schedule):
    p, k = hb.shape
    rows = EXPERT_BLOCK
    n = w_down.shape[-1]
    first, nxt_e, last = schedule
    kernel = functools.partial(_moe_down_kernel, layer=layer)
    return pl.pallas_call(
        kernel,
        out_shape=jax.ShapeDtypeStruct((p, n), F32),
        grid_spec=pltpu.PrefetchScalarGridSpec(
            num_scalar_prefetch=5,
            grid=(p // rows,),
            in_specs=[pl.BlockSpec((rows, k), lambda i, *_: (i, 0)), pl.BlockSpec(memory_space=pl.ANY)],
            out_specs=pl.BlockSpec((rows, n), lambda i, *_: (i, 0)),
            scratch_shapes=[pltpu.VMEM((k, n), BF16), pltpu.VMEM((k, n), F32), pltpu.SemaphoreType.DMA(())],
        ),
        compiler_params=_params(1, 50 << 20),
        name="moe_down",
    )(block_e, nused, first, nxt_e, last, hb, w_down)


COMBINE_T = 512


def _combine_kernel(dest_ref, y_hbm, wt_ref, h_ref, g_ref, b_ref, of_ref, ob_ref,
                    buf_ref, sems, *, n_tokens):
    i = pl.program_id(0)
    t = COMBINE_T

    def issue(block, slot):
        def body(r, carry):
            tok = block * t + r
            _row_copy(y_hbm, dest_ref[tok], buf_ref.at[slot, 0], r, sems.at[slot]).start(priority=0)
            _row_copy(y_hbm, dest_ref[n_tokens + tok], buf_ref.at[slot, 1], r, sems.at[slot]).start(priority=1)
            return carry

        lax.fori_loop(0, t, body, 0, unroll=ISSUE_UNROLL)

    slot = i % 2

    @pl.when(i == 0)
    def _():
        issue(0, 0)

    @pl.when(i + 1 < pl.num_programs(0))
    def _():
        issue(i + 1, 1 - slot)

    for k in range(TOP_K):
        pltpu.make_async_copy(y_hbm.at[pl.ds(0, t)], buf_ref.at[slot, k], sems.at[slot]).wait()
    ff = buf_ref[slot, 0] * wt_ref[:, 0:1] + buf_ref[slot, 1] * wt_ref[:, 1:2]
    y = _layer_norm(DN_ALPHA * h_ref[...] + ff, g_ref[...], b_ref[...])
    of_ref[...] = y
    ob_ref[...] = y.astype(BF16)


def moe_combine_ln(yb, dest, wts_t, h, ln_g, ln_b, layer, which):
    n, d = h.shape
    t = COMBINE_T
    row = pl.BlockSpec((t, d), lambda i, ds: (i, 0))
    par = pl.BlockSpec((None, None, 1, d), lambda i, ds: (layer, which, 0, 0))
    kernel = functools.partial(_combine_kernel, n_tokens=n)
    return pl.pallas_call(
        kernel,
        out_shape=(jax.ShapeDtypeStruct((n, d), F32), jax.ShapeDtypeStruct((n, d), BF16)),
        grid_spec=pltpu.PrefetchScalarGridSpec(
            num_scalar_prefetch=1,
            grid=(n // t,),
            in_specs=[
                pl.BlockSpec(memory_space=pl.ANY),
                pl.BlockSpec((t, TOP_K), lambda i, ds: (i, 0)),
                row, par, par,
            ],
            out_specs=(row, row),
            scratch_shapes=[pltpu.VMEM((2, TOP_K, t, d), F32), pltpu.SemaphoreType.DMA((2,))],
        ),
        compiler_params=_params(1, 48 << 20),
        name="moe_combine_ln",
    )(dest, yb, wts_t, h, ln_g, ln_b)


def moe_layer(h, hb, w_router, w_gate, w_up, w_down, ln_g, ln_b, layer, moe_idx):
    n = h.shape[0]
    m = n * TOP_K
    n_blocks = (m + N_EXPERTS * (EXPERT_BLOCK - 1) + EXPERT_BLOCK - 1) // EXPERT_BLOCK
    e, wts, rank, cnt = moe_router(h, w_router, moe_idx)
    counts = cnt[:, 0]
    padded = (counts + EXPERT_BLOCK - 1) // EXPERT_BLOCK * EXPERT_BLOCK
    pad_ends = jnp.cumsum(padded)
    pad_starts = pad_ends - padded
    expert_ids = jnp.arange(N_EXPERTS, dtype=jnp.int32)[:, None, None]
    dest = jnp.sum(jnp.where(e[None] == expert_ids, pad_starts[:, None, None], 0), axis=0) + rank
    block_start = jnp.arange(n_blocks, dtype=jnp.int32) * EXPERT_BLOCK
    block_e = jnp.minimum(jnp.sum(block_start[:, None] >= pad_ends[None, :], axis=1), N_EXPERTS - 1).astype(jnp.int32)
    nused = (pad_ends[-1:] // EXPERT_BLOCK).astype(jnp.int32)

    dest = dest.reshape(-1)
    pad_rows = jnp.concatenate([pad_starts + counts, pad_ends]).astype(jnp.int32)
    schedule = _group_schedule(block_e)
    xb = moe_dispatch(h, dest, pad_rows, nused, n_blocks)
    hid = moe_up(xb, w_gate, w_up, moe_idx, block_e, nused, schedule)
    yb = moe_down(hid, w_down, moe_idx, block_e, nused, schedule)
    return moe_combine_ln(yb, dest, wts.T, h, ln_g, ln_b, layer, 2)


def kernel(x, mem, rel_table, mix_w_in, gmlp_ln_g, gmlp_ln_b, gmlp_w_s, gmlp_b_s, pool_w, pool_scale, mix_w_out, diff_w_qkv, diff_lam_q1, diff_lam_k1, diff_lam_q2, diff_lam_k2, diff_subln_g, diff_w_o, xa_w_q, xa_w_kv, xa_w_o, ffn_w_gate, ffn_w_up, ffn_w_down, moe_w_router, moe_w_gate, moe_w_up, moe_w_down, ln_g, ln_b):
    batch, seq_len, d = x.shape
    mem_len = mem.shape[1]
    n = batch * seq_len
    h = x.reshape(n, d)
    hb = h
    memf = mem.reshape(batch * mem_len, d)
    ln_g4 = ln_g.reshape(DEPTH, 3, 1, d)
    ln_b4 = ln_b.reshape(DEPTH, 3, 1, d)
    lam_params = jnp.stack([diff_lam_q1, diff_lam_k1, diff_lam_q2, diff_lam_k2], axis=1)
    bias_tiles = attention_bias_tiles(rel_table)
    kv_all = matmul_all_layers(memf, xa_w_kv, tn=1024, out_dtype=BF16)
    xa_qk, xa_vo = cross_attention_folds(kv_all, xa_w_q, xa_w_o, batch, mem_len)

    for layer in range(DEPTH):
        i = layer // 2
        if layer % 2 == 0:
            hm = matmul(hb, mix_w_in, i, tm=512, tn=MIX_IN // 2, out_dtype=F32)
            mixed = gmlp_pool_mixer(hm, seq_len, gmlp_ln_g, gmlp_ln_b, gmlp_w_s, gmlp_b_s, pool_w,
                                    pool_scale, i)
            h, hb = matmul_residual_ln(mixed, mix_w_out, i, h, ln_g4, ln_b4, layer, 0)
        else:
            lam_init = 0.8 - 0.6 * math.exp(-0.3 * layer)
            qkv = matmul(hb, diff_w_qkv, i, tm=1024, tn=1024, out_dtype=BF16,
                         scaled_cols=D, scale=LOG2E * C_HEAD_DIM ** -0.5)
            att = diff_attention(qkv, bias_tiles, lam_params, diff_subln_g, i, lam_init, batch, seq_len)
            h, hb = matmul_residual_ln(att, diff_w_o, i, h, ln_g4, ln_b4, layer, 0)

        h, hb = cross_attention_residual_ln(hb, h, xa_qk, xa_vo, ln_g4, ln_b4, layer, 1, seq_len, mem_len)

        if layer % 2 == 0:
            hid = swiglu_up(hb, ffn_w_gate, ffn_w_up, i, tm=1024, tn=512)
            h, hb = matmul_kacc_residual_ln(hid, ffn_w_down, i, h, ln_g4, ln_b4, layer, 2,
                                            tm=1024, tk=512, single_buffer_rows=True)
        else:
            h, hb = moe_layer(h, hb, moe_w_router, moe_w_gate, moe_w_up, moe_w_down, ln_g4, ln_b4,
                              layer, i)
    return h.reshape(batch, seq_len, d)
```

```python
import functools
import math

import numpy as np
import jax
import jax.numpy as jnp
from jax import lax
from jax.experimental import pallas as pl
from jax.experimental.pallas import tpu as pltpu

F32 = jnp.float32
BF16 = jnp.bfloat16

D = 2048
DEPTH = 4
CHUNK = 64
GMLP_CHUNK = 128
A_WIDTH = D // 2
A_HEADS = 8
A_HEAD_DIM = A_WIDTH // A_HEADS
B_WIDTH = D // 2
POOL_WINDOWS = (2, 4, 8, 16)
B_GROUP = B_WIDTH // len(POOL_WINDOWS)
MIX_IN = 2 * A_WIDTH + B_WIDTH
C_HEADS = 16
C_HEAD_DIM = D // (2 * C_HEADS)
C_V_DIM = 2 * C_HEAD_DIM
REL_BUCKETS = 32
REL_MAX_DIST = 128
X_HEADS = 4
X_HEAD_DIM = D // X_HEADS
N_EXPERTS = 8
TOP_K = 2
EXPERT_BLOCK = 256
DN_ALPHA = (2 * DEPTH) ** 0.25
LN_EPS = 1e-5
NEG = -1e30

VMEM_CAP_BYTES = 56 * 1024 * 1024
CAST_ROWS = 256
ATT_TQ = 256
POOL_HALO = 16


def _params(n_axes, vmem_bytes):
    return pltpu.CompilerParams(
        dimension_semantics=("arbitrary",) * n_axes,
        vmem_limit_bytes=int(min(max(vmem_bytes, 16 * 1024 * 1024), VMEM_CAP_BYTES)),
    )


def _cast_rows(w_ref, wb_ref):
    k = w_ref.shape[0]
    rows = CAST_ROWS if k % CAST_ROWS == 0 else k

    def body(c, carry):
        r = pl.multiple_of(c * rows, rows)
        wb_ref[pl.ds(r, rows), :] = w_ref[pl.ds(r, rows), :].astype(BF16)
        return carry

    lax.fori_loop(0, k // rows, body, 0)


def _layer_norm(x, g, b):
    mu = jnp.mean(x, axis=-1, keepdims=True)
    xc = x - mu
    var = jnp.mean(xc * xc, axis=-1, keepdims=True)
    return xc * lax.rsqrt(var + LN_EPS) * g + b


def _mm_kernel(a_ref, w_ref, o_ref, wb_ref, *, scaled_tiles, scale):
    j = pl.program_id(0)

    @pl.when(pl.program_id(1) == 0)
    def _():
        _cast_rows(w_ref, wb_ref)

    a = a_ref[...].astype(BF16)
    y = jnp.dot(a, wb_ref[...], preferred_element_type=F32)
    if scaled_tiles:
        y = y * jnp.where(j < scaled_tiles, scale, 1.0)
    o_ref[...] = y.astype(o_ref.dtype)


def matmul(a, w, layer, *, tm, tn, out_dtype, scaled_cols=0, scale=1.0):
    m, k = a.shape
    n = w.shape[-1]
    assert m % tm == 0 and n % tn == 0 and w.shape[-2] == k and scaled_cols % tn == 0
    vmem = 2 * tm * k * a.dtype.itemsize + 2 * k * tn * 4 + k * tn * 2 + 2 * tm * tn * 4 + tm * tn * 4
    kernel = functools.partial(_mm_kernel, scaled_tiles=scaled_cols // tn, scale=scale)
    return pl.pallas_call(
        kernel,
        out_shape=jax.ShapeDtypeStruct((m, n), out_dtype),
        grid=(n // tn, m // tm),
        in_specs=[
            pl.BlockSpec((tm, k), lambda j, i: (i, 0)),
            pl.BlockSpec((None, k, tn), lambda j, i: (layer, 0, j)),
        ],
        out_specs=pl.BlockSpec((tm, tn), lambda j, i: (i, j)),
        scratch_shapes=[pltpu.VMEM((k, tn), BF16)],
        compiler_params=_params(2, vmem + (4 << 20)),
        name="matmul",
    )(a, w)


MMLN_GROUPS = 2


def _mm_ln_kernel(a_ref, w_ref, h_ref, g_ref, b_ref, of_ref, ob_ref, wb_ref):
    @pl.when(pl.program_id(0) == 0)
    def _():
        _cast_rows(w_ref, wb_ref)

    rows = a_ref.shape[0] // MMLN_GROUPS
    for r in range(MMLN_GROUPS):
        sl = slice(r * rows, (r + 1) * rows)
        y = jnp.dot(a_ref[sl, :], wb_ref[...], preferred_element_type=F32)
        out = _layer_norm(DN_ALPHA * h_ref[sl, :] + y, g_ref[...], b_ref[...])
        of_ref[sl, :] = out
        ob_ref[sl, :] = out.astype(BF16)


def matmul_residual_ln(a, w, w_layer, h, ln_g, ln_b, layer, which, *, tm=512):
    m, k = a.shape
    d = w.shape[-1]
    assert m % tm == 0 and h.shape == (m, d)
    row = pl.BlockSpec((tm, d), lambda i: (i, 0))
    par = pl.BlockSpec((None, None, 1, d), lambda i: (layer, which, 0, 0))
    vmem = k * d * 4 + k * d * 2 + 2 * tm * k * 2 + 4 * tm * d * 4 + 2 * tm * d * 2 + 3 * tm * d * 4
    return pl.pallas_call(
        _mm_ln_kernel,
        out_shape=(jax.ShapeDtypeStruct((m, d), F32), jax.ShapeDtypeStruct((m, d), BF16)),
        grid=(m // tm,),
        in_specs=[
            pl.BlockSpec((tm, k), lambda i: (i, 0)),
            pl.BlockSpec((None, k, d), lambda i: (w_layer, 0, 0), pipeline_mode=pl.Buffered(1)),
            row, par, par,
        ],
        out_specs=(row, row),
        scratch_shapes=[pltpu.VMEM((k, d), BF16)],
        compiler_params=_params(1, vmem + (4 << 20)),
        name="matmul_residual_ln",
    )(a, w, h, ln_g, ln_b)


KACC_GROUP_ROWS = 256


def _mm_kacc_ln_kernel(a_ref, w_ref, h_ref, g_ref, b_ref, of_ref, ob_ref, acc_ref):
    k = pl.program_id(1)

    @pl.when(jnp.logical_and(pl.program_id(0) == 0, k == 0))
    def _():
        acc_ref[...] = jnp.zeros(acc_ref.shape, F32)

    part = jnp.dot(a_ref[...], w_ref[...].astype(BF16), preferred_element_type=F32)
    acc_ref[...] = jnp.where(k > 0, acc_ref[...], 0.0) + part

    @pl.when(k == pl.num_programs(1) - 1)
    def _():
        for r in range(0, acc_ref.shape[0], KACC_GROUP_ROWS):
            sl = slice(r, r + KACC_GROUP_ROWS)
            out = _layer_norm(DN_ALPHA * h_ref[sl, :] + acc_ref[sl, :], g_ref[...], b_ref[...])
            of_ref[sl, :] = out
            ob_ref[sl, :] = out.astype(BF16)


def matmul_kacc_residual_ln(a, w, w_layer, h, ln_g, ln_b, layer, which, *, tm, tk, single_buffer_rows):
    m, kk = a.shape
    d = w.shape[-1]
    assert m % tm == 0 and kk % tk == 0 and tm % KACC_GROUP_ROWS == 0
    mode = dict(pipeline_mode=pl.Buffered(1)) if single_buffer_rows else {}
    row = pl.BlockSpec((tm, d), lambda i, k: (i, 0))
    out_row = pl.BlockSpec((tm, d), lambda i, k: (i, 0), **mode)
    par = pl.BlockSpec((None, None, 1, d), lambda i, k: (layer, which, 0, 0))
    return pl.pallas_call(
        _mm_kacc_ln_kernel,
        out_shape=(jax.ShapeDtypeStruct((m, d), F32), jax.ShapeDtypeStruct((m, d), BF16)),
        grid=(m // tm, kk // tk),
        in_specs=[
            pl.BlockSpec((tm, tk), lambda i, k: (i, k)),
            pl.BlockSpec((None, tk, d), lambda i, k: (w_layer, k, 0)),
            row, par, par,
        ],
        out_specs=(out_row, out_row),
        scratch_shapes=[pltpu.VMEM((tm, d), F32)],
        compiler_params=_params(2, VMEM_CAP_BYTES),
        name="matmul_kacc_residual_ln",
    )(a, w, h, ln_g, ln_b)


def _mm_layers_kernel(a_ref, w_ref, o_ref):
    o_ref[...] = jnp.dot(a_ref[...].astype(BF16), w_ref[...].astype(BF16),
                         preferred_element_type=F32).astype(o_ref.dtype)


def matmul_all_layers(a, w, *, tn, out_dtype):
    m, k = a.shape
    layers, _, n = w.shape
    assert n % tn == 0
    return pl.pallas_call(
        _mm_layers_kernel,
        out_shape=jax.ShapeDtypeStruct((layers, m, n), out_dtype),
        grid=(layers, n // tn),
        in_specs=[
            pl.BlockSpec((m, k), lambda l, j: (0, 0)),
            pl.BlockSpec((None, k, tn), lambda l, j: (l, 0, j)),
        ],
        out_specs=pl.BlockSpec((None, m, tn), lambda l, j: (l, 0, j)),
        compiler_params=_params(2, 2 * m * k * 4 + 3 * k * tn * 4 + 3 * m * tn * 4 + (4 << 20)),
        name="matmul_all_layers",
    )(a, w)


def _swiglu_kernel(a_ref, wg_ref, wu_ref, o_ref, wgb_ref, wub_ref):
    @pl.when(pl.program_id(1) == 0)
    def _():
        _cast_rows(wg_ref, wgb_ref)
        _cast_rows(wu_ref, wub_ref)

    a = a_ref[...]
    g = jnp.dot(a, wgb_ref[...], preferred_element_type=F32)
    u = jnp.dot(a, wub_ref[...], preferred_element_type=F32)
    o_ref[...] = (jax.nn.silu(g) * u).astype(o_ref.dtype)


def swiglu_up(a, wg, wu, layer, *, tm, tn):
    m, k = a.shape
    n = wg.shape[-1]
    assert m % tm == 0 and n % tn == 0
    vmem = 2 * tm * k * 2 + 4 * k * tn * 4 + 2 * k * tn * 2 + 2 * tm * tn * 2 + 3 * tm * tn * 4
    wspec = pl.BlockSpec((None, k, tn), lambda j, i: (layer, 0, j))
    return pl.pallas_call(
        _swiglu_kernel,
        out_shape=jax.ShapeDtypeStruct((m, n), BF16),
        grid=(n // tn, m // tm),
        in_specs=[pl.BlockSpec((tm, k), lambda j, i: (i, 0)), wspec, wspec],
        out_specs=pl.BlockSpec((tm, tn), lambda j, i: (i, j)),
        scratch_shapes=[pltpu.VMEM((k, tn), BF16), pltpu.VMEM((k, tn), BF16)],
        compiler_params=_params(2, vmem + (4 << 20)),
        name="swiglu_up",
    )(a, wg, wu)


MIX_ROWS = 2 * GMLP_CHUNK


def _mixer_kernel(hm_ref, prev_ref, lng_ref, lnb_ref, ws_ref, bst_ref, pw_ref, ps_ref, o_ref,
                  wbd_ref, ext_ref, dbl_ref, *, blocks_per_seq):
    i = pl.program_id(0)

    @pl.when(i == 0)
    def _():
        p = lax.broadcasted_iota(jnp.int32, (GMLP_CHUNK, GMLP_CHUNK), 0)
        q = lax.broadcasted_iota(jnp.int32, (GMLP_CHUNK, GMLP_CHUNK), 1)
        allowed = (q // CHUNK) <= (p // CHUNK)
        wbd_ref[...] = jnp.zeros(wbd_ref.shape, BF16)
        for h in range(A_HEADS):
            w = jnp.where(allowed, ws_ref[h], 0.0).astype(BF16)
            wbd_ref[h, 0:GMLP_CHUNK, 0:GMLP_CHUNK] = w
            wbd_ref[h, GMLP_CHUNK:MIX_ROWS, GMLP_CHUNK:MIX_ROWS] = w

    z = jax.nn.gelu(hm_ref[:, 0:2 * A_WIDTH])
    u = z[:, 0:A_WIDTH]
    v = _layer_norm(z[:, A_WIDTH:2 * A_WIDTH], lng_ref[...], lnb_ref[...]).astype(BF16)
    for h in range(A_HEADS):
        c0 = h * A_HEAD_DIM
        sv = jnp.dot(wbd_ref[h], v[:, c0:c0 + A_HEAD_DIM], preferred_element_type=F32)
        sv = sv + bst_ref[:, h:h + 1]
        o_ref[:, c0:c0 + A_HEAD_DIM] = (u[:, c0:c0 + A_HEAD_DIM] * sv).astype(BF16)

    blk = i % blocks_per_seq
    prev = jnp.where(blk == 0, 0.0, prev_ref[...])
    ext_ref[0:POOL_HALO, :] = prev
    ext_ref[POOL_HALO:POOL_HALO + MIX_ROWS, :] = hm_ref[:, 2 * A_WIDTH:MIX_IN]
    t1 = blk * MIX_ROWS + lax.broadcasted_iota(jnp.int32, (MIX_ROWS, 1), 0) + 1
    for g, w in enumerate(POOL_WINDOWS):
        c0 = g * B_GROUP
        cols = slice(c0, c0 + B_GROUP)
        pin = ext_ref[POOL_HALO:POOL_HALO + MIX_ROWS, cols]
        src_ref, src_cols = ext_ref, cols
        k = 1
        while k < w:
            lo = 2 * k - 1
            n = POOL_HALO + MIX_ROWS - lo
            dbl_ref[lo:lo + n, :] = src_ref[lo:lo + n, src_cols] + src_ref[lo - k:lo - k + n, src_cols]
            src_ref, src_cols = dbl_ref, slice(None)
            k *= 2
        s = dbl_ref[POOL_HALO:POOL_HALO + MIX_ROWS, :]
        cnt = jnp.minimum(t1, w).astype(F32)
        pooled = s / cnt - pin
        b = jnp.dot(pooled.astype(BF16), pw_ref[g].astype(BF16), preferred_element_type=F32)
        b = b * ps_ref[:, c0:c0 + B_GROUP]
        o_ref[:, A_WIDTH + c0:A_WIDTH + c0 + B_GROUP] = b.astype(BF16)


def gmlp_pool_mixer(hm, seq_len, ln_g, ln_b, w_s, b_s, pool_w, pool_scale, layer):
    n = hm.shape[0]
    assert seq_len % MIX_ROWS == 0 and max(POOL_WINDOWS) <= POOL_HALO
    blocks_per_seq = seq_len // MIX_ROWS
    halo_per_block = MIX_ROWS // POOL_HALO
    bst = jnp.tile(b_s[layer].T, (MIX_ROWS // GMLP_CHUNK, 1))
    kernel = functools.partial(_mixer_kernel, blocks_per_seq=blocks_per_seq)
    return pl.pallas_call(
        kernel,
        out_shape=jax.ShapeDtypeStruct((n, D), BF16),
        grid=(n // MIX_ROWS,),
        in_specs=[
            pl.BlockSpec((MIX_ROWS, MIX_IN), lambda i: (i, 0)),
            pl.BlockSpec((POOL_HALO, B_WIDTH),
                         lambda i: (jnp.maximum(i * halo_per_block - 1, 0), 2 * A_WIDTH // B_WIDTH)),
            pl.BlockSpec((None, 1, A_WIDTH), lambda i: (layer, 0, 0)),
            pl.BlockSpec((None, 1, A_WIDTH), lambda i: (layer, 0, 0)),
            pl.BlockSpec((None, A_HEADS, GMLP_CHUNK, GMLP_CHUNK), lambda i: (layer, 0, 0, 0)),
            pl.BlockSpec((MIX_ROWS, A_HEADS), lambda i: (0, 0)),
            pl.BlockSpec((None, len(POOL_WINDOWS), B_GROUP, B_GROUP), lambda i: (layer, 0, 0, 0)),
            pl.BlockSpec((None, 1, B_WIDTH), lambda i: (layer, 0, 0)),
        ],
        out_specs=pl.BlockSpec((MIX_ROWS, D), lambda i: (i, 0)),
        scratch_shapes=[
            pltpu.VMEM((A_HEADS, MIX_ROWS, MIX_ROWS), BF16),
            pltpu.VMEM((POOL_HALO + MIX_ROWS, B_WIDTH), F32),
            pltpu.VMEM((POOL_HALO + MIX_ROWS, B_GROUP), F32),
        ],
        compiler_params=_params(1, 40 << 20),
        name="gmlp_pool_mixer",
    )(hm, hm, ln_g.reshape(-1, 1, A_WIDTH), ln_b.reshape(-1, 1, A_WIDTH), w_s, bst, pool_w,
      pool_scale.reshape(-1, 1, B_WIDTH))


def _t5_bucket_np(rel):
    half = REL_BUCKETS // 2
    max_exact = half // 2
    rel = np.asarray(rel, dtype=np.int64)
    n = np.abs(rel)
    steps = half - max_exact
    ratio = REL_MAX_DIST // max_exact
    large = np.zeros_like(n)
    nn = n.astype(object) ** steps
    for j in range(1, steps + 1):
        large = large + (nn >= (max_exact ** steps) * (ratio ** j)).astype(np.int64)
    large = np.minimum(max_exact + large, half - 1)
    return np.where(rel > 0, half, 0) + np.where(n < max_exact, n, large)


def _bias_bucket_tiles():
    qi = np.arange(ATT_TQ)[:, None]
    kj = np.arange(ATT_TQ)[None, :]
    diag = _t5_bucket_np(kj - qi)
    diag = np.where((kj // CHUNK) <= (qi // CHUNK), diag, -1)
    prev = _t5_bucket_np(kj - ATT_TQ - qi)
    return np.stack([diag, prev], 0).astype(np.int32)


FAR_BUCKET = REL_BUCKETS // 2 - 1
LOG2E = math.log2(math.e)


def _bias_kernel(rel_ref, bk_ref, o_ref):
    h = pl.program_id(0)
    bk = bk_ref[...]
    acc = jnp.zeros(bk.shape, F32)
    for b in range(REL_BUCKETS):
        acc = jnp.where(bk == b, rel_ref[b, h], acc)
    o_ref[...] = jnp.where(bk < 0, NEG, (acc - rel_ref[FAR_BUCKET, h]) * LOG2E)


def attention_bias_tiles(rel_table):
    assert ATT_TQ >= REL_MAX_DIST and ATT_TQ % CHUNK == 0
    bk = jnp.asarray(_bias_bucket_tiles())
    return pl.pallas_call(
        _bias_kernel,
        out_shape=jax.ShapeDtypeStruct((C_HEADS, 2, ATT_TQ, ATT_TQ), F32),
        grid=(C_HEADS,),
        in_specs=[
            pl.BlockSpec(memory_space=pltpu.SMEM),
            pl.BlockSpec((2, ATT_TQ, ATT_TQ), lambda h: (0, 0, 0)),
        ],
        out_specs=pl.BlockSpec((None, 2, ATT_TQ, ATT_TQ), lambda h: (h, 0, 0, 0)),
        compiler_params=_params(1, 16 << 20),
        name="attention_bias_tiles",
    )(rel_table, bk)


ATT_HEADS_PER_STEP = 2


def _diff_attn_kernel(q_ref, k_ref, v_ref, bias_ref, lam_ref, g_ref, o_ref, vx_ref, *, lam_init, seq_len):
    tq = ATT_TQ
    nt = (((1,), (1,)), ((), ()))
    lp = lam_ref[...]
    lam = (jnp.exp(jnp.sum(lp[0:1] * lp[1:2], axis=1, keepdims=True))
           - jnp.exp(jnp.sum(lp[2:3] * lp[3:4], axis=1, keepdims=True)) + lam_init)
    lane = lax.broadcasted_iota(jnp.int32, (tq, C_V_DIM), 1)
    for hh in range(ATT_HEADS_PER_STEP):
        hc = slice(hh * C_V_DIM, (hh + 1) * C_V_DIM)
        vx_ref[hh, :, 0:C_V_DIM] = v_ref[:, hc]
        vx_ref[hh, :, C_V_DIM:2 * C_V_DIM] = jnp.ones((seq_len, C_V_DIM), BF16)
        for qi in range(seq_len // tq):
            q = q_ref[qi * tq:(qi + 1) * tq, hc]
            zero = jnp.zeros_like(q)
            parts = []
            if qi >= 2:
                parts.append((0, (qi - 1) * tq, None))
            if qi >= 1:
                parts.append(((qi - 1) * tq, qi * tq, 1))
            parts.append((qi * tq, (qi + 1) * tq, 0))
            outs = []
            for half in range(2):
                keep = (lane < C_HEAD_DIM) if half == 0 else (lane >= C_HEAD_DIM)
                qh = jnp.where(keep, q, zero)
                scores = []
                m = None
                for a, b, bi in parts:
                    s = lax.dot_general(qh, k_ref[a:b, hc], nt, preferred_element_type=F32)
                    if bi is not None:
                        s = s + bias_ref[hh, bi]
                    scores.append(s)
                    sm = jnp.max(s, axis=1, keepdims=True)
                    m = sm if m is None else jnp.maximum(m, sm)
                acc = None
                for s, (a, b, _) in zip(scores, parts):
                    p = jnp.exp2(s - m).astype(BF16)
                    pv = jnp.dot(p, vx_ref[hh, a:b, :], preferred_element_type=F32)
                    acc = pv if acc is None else acc + pv
                outs.append(acc[:, 0:C_V_DIM] / acc[:, C_V_DIM:C_V_DIM + 1])
            o = outs[0] - lam * outs[1]
            y = o * lax.rsqrt(jnp.mean(o * o, axis=-1, keepdims=True) + LN_EPS) * g_ref[...]
            o_ref[qi * tq:(qi + 1) * tq, hc] = (y * (1.0 - lam_init)).astype(BF16)


def diff_attention(qkv, bias_tiles, lam_params, subln_g, layer, lam_init, batch, seq_len):
    n = qkv.shape[0]
    tq = ATT_TQ
    hps = ATT_HEADS_PER_STEP
    assert seq_len % tq == 0 and C_HEADS % hps == 0
    groups = C_HEADS // hps
    kernel = functools.partial(_diff_attn_kernel, lam_init=lam_init, seq_len=seq_len)
    seq = lambda off: pl.BlockSpec((seq_len, hps * C_V_DIM), lambda b, h: (b, off + h))
    return pl.pallas_call(
        kernel,
        out_shape=jax.ShapeDtypeStruct((n, D), BF16),
        grid=(batch, groups),
        in_specs=[
            seq(0), seq(groups), seq(2 * groups),
            pl.BlockSpec((hps, 2, tq, tq), lambda b, h: (h, 0, 0, 0)),
            pl.BlockSpec((None, 4, C_HEAD_DIM), lambda b, h: (layer, 0, 0)),
            pl.BlockSpec((None, 1, C_V_DIM), lambda b, h: (layer, 0, 0)),
        ],
        out_specs=seq(0),
        scratch_shapes=[pltpu.VMEM((hps, seq_len, 2 * C_V_DIM), BF16)],
        compiler_params=_params(2, 48 << 20),
        name="diff_attention",
    )(qkv, qkv, qkv, bias_tiles, lam_params, subln_g.reshape(-1, 1, C_V_DIM))


def _qk_fold_kernel(wq_ref, k_ref, o_ref, *, batch, mem_len):
    wb = wq_ref[...].astype(BF16)
    for b in range(batch):
        qk = lax.dot_general(wb, k_ref[b * mem_len:(b + 1) * mem_len, :], (((1,), (1,)), ((), ())),
                             preferred_element_type=F32)
        o_ref[b] = (qk * (LOG2E * X_HEAD_DIM ** -0.5)).astype(o_ref.dtype)


def _vo_fold_kernel(v_ref, wo_ref, o_ref, *, batch, mem_len):
    wb = wo_ref[...].astype(BF16)
    for b in range(batch):
        o_ref[b] = jnp.dot(v_ref[b * mem_len:(b + 1) * mem_len, :], wb,
                           preferred_element_type=F32).astype(o_ref.dtype)


def cross_attention_folds(kv_all, w_q, w_o, batch, mem_len):
    layers = kv_all.shape[0]
    grid = (layers, X_HEADS)
    qk = pl.pallas_call(
        functools.partial(_qk_fold_kernel, batch=batch, mem_len=mem_len),
        out_shape=jax.ShapeDtypeStruct((layers, batch, D, X_HEADS * mem_len), BF16),
        grid=grid,
        in_specs=[
            pl.BlockSpec((None, D, X_HEAD_DIM), lambda l, h: (l, 0, h)),
            pl.BlockSpec((None, batch * mem_len, X_HEAD_DIM), lambda l, h: (l, 0, h)),
        ],
        out_specs=pl.BlockSpec((None, batch, D, mem_len), lambda l, h: (l, 0, 0, h)),
        compiler_params=_params(2, 32 << 20),
        name="cross_attention_qk_fold",
    )(w_q, kv_all)
    vo = pl.pallas_call(
        functools.partial(_vo_fold_kernel, batch=batch, mem_len=mem_len),
        out_shape=jax.ShapeDtypeStruct((layers, batch, X_HEADS * mem_len, D), BF16),
        grid=grid,
        in_specs=[
            pl.BlockSpec((None, batch * mem_len, X_HEAD_DIM), lambda l, h: (l, 0, X_HEADS + h)),
            pl.BlockSpec((None, X_HEAD_DIM, D), lambda l, h: (l, h, 0)),
        ],
        out_specs=pl.BlockSpec((None, batch, mem_len, D), lambda l, h: (l, 0, h, 0)),
        compiler_params=_params(2, 32 << 20),
        name="cross_attention_vo_fold",
    )(kv_all, w_o)
    return qk, vo


XATTN_GROUPS = 2


def _xattn_ln_kernel(hb_ref, qk_ref, vo_ref, h_ref, g_ref, b_ref, of_ref, ob_ref, *, mem_len):
    rows = hb_ref.shape[0] // XATTN_GROUPS
    for r in range(XATTN_GROUPS):
        sl = slice(r * rows, (r + 1) * rows)
        s = jnp.dot(hb_ref[sl, :], qk_ref[...], preferred_element_type=F32)
        probs = []
        for hd in range(X_HEADS):
            sh = s[:, hd * mem_len:(hd + 1) * mem_len]
            e = jnp.exp2(sh - jnp.max(sh, axis=1, keepdims=True))
            inv = 1.0 / jnp.sum(e, axis=1, keepdims=True)
            probs.append((e * inv).astype(BF16))
        p = jnp.concatenate(probs, axis=1)
        y = jnp.dot(p, vo_ref[...], preferred_element_type=F32)
        out = _layer_norm(DN_ALPHA * h_ref[sl, :] + y, g_ref[...], b_ref[...])
        of_ref[sl, :] = out
        ob_ref[sl, :] = out.astype(BF16)


def cross_attention_residual_ln(hb, h, qk, vo, ln_g, ln_b, layer, which, seq_len, mem_len, *, tm=512):
    n, d = h.shape
    assert seq_len % tm == 0 and tm % XATTN_GROUPS == 0
    blocks_per_seq = seq_len // tm
    row = pl.BlockSpec((tm, d), lambda i: (i, 0))
    par = pl.BlockSpec((None, None, 1, d), lambda i: (layer, which, 0, 0))
    kw = X_HEADS * mem_len
    kernel = functools.partial(_xattn_ln_kernel, mem_len=mem_len)
    return pl.pallas_call(
        kernel,
        out_shape=(jax.ShapeDtypeStruct((n, d), F32), jax.ShapeDtypeStruct((n, d), BF16)),
        grid=(n // tm,),
        in_specs=[
            row,
            pl.BlockSpec((None, None, d, kw), lambda i: (layer, i // blocks_per_seq, 0, 0)),
            pl.BlockSpec((None, None, kw, d), lambda i: (layer, i // blocks_per_seq, 0, 0)),
            row, par, par,
        ],
        out_specs=(row, row),
        compiler_params=_params(1, 52 << 20),
        name="cross_attention_residual_ln",
    )(hb, qk, vo, h, ln_g, ln_b)


ROUTE_T = 256


def _router_kernel(h_ref, wrt_ref, e_ref, w_ref, rank_ref, cnt_ref, base_ref):
    i = pl.program_id(0)

    @pl.when(i == 0)
    def _():
        base_ref[...] = jnp.zeros(base_ref.shape, F32)

    t = ROUTE_T
    nt = (((1,), (1,)), ((), ()))
    hf = h_ref[...]
    wf = wrt_ref[...]
    hh = hf.astype(BF16)
    wh = wf.astype(BF16)
    hl = (hf - hh.astype(F32)).astype(BF16)
    wl = (wf - wh.astype(F32)).astype(BF16)
    logits = (lax.dot_general(wh, hh, nt, preferred_element_type=F32)
              + lax.dot_general(wh, hl, nt, preferred_element_type=F32)
              + lax.dot_general(wl, hh, nt, preferred_element_type=F32))
    eidx = lax.broadcasted_iota(jnp.int32, (N_EXPERTS, t), 0)
    m1 = jnp.max(logits, axis=0, keepdims=True)
    e1 = jnp.min(jnp.where(logits == m1, eidx, N_EXPERTS), axis=0, keepdims=True)
    oh1 = eidx == e1
    rest = jnp.where(oh1, -jnp.inf, logits)
    m2 = jnp.max(rest, axis=0, keepdims=True)
    e2 = jnp.min(jnp.where(rest == m2, eidx, N_EXPERTS), axis=0, keepdims=True)
    oh2 = eidx == e2
    ex = jnp.exp(m2 - m1)
    den = 1.0 + ex
    e_ref[0:1, :] = e1
    e_ref[1:2, :] = e2
    w_ref[0:1, :] = 1.0 / den
    w_ref[1:2, :] = ex / den

    oh = (oh1.astype(F32) + oh2.astype(F32)).astype(BF16)
    r = lax.broadcasted_iota(jnp.int32, (t, t), 0)
    c = lax.broadcasted_iota(jnp.int32, (t, t), 1)
    before = (r < c).astype(BF16)
    ones = jnp.ones((t, t), BF16)
    pos = base_ref[...] + jnp.dot(oh, before, preferred_element_type=F32)
    rank_ref[0:1, :] = jnp.sum(jnp.where(oh1, pos, 0.0), axis=0, keepdims=True).astype(jnp.int32)
    rank_ref[1:2, :] = jnp.sum(jnp.where(oh2, pos, 0.0), axis=0, keepdims=True).astype(jnp.int32)
    total = base_ref[...] + jnp.dot(oh, ones, preferred_element_type=F32)
    base_ref[...] = total
    cnt_ref[...] = total[:, 0:128].astype(jnp.int32)


def moe_router(h, w_router, layer):
    n = h.shape[0]
    t = ROUTE_T
    wrt = jnp.swapaxes(w_router, 1, 2)
    pair = pl.BlockSpec((TOP_K, t), lambda i: (0, i))
    return pl.pallas_call(
        _router_kernel,
        out_shape=(
            jax.ShapeDtypeStruct((TOP_K, n), jnp.int32),
            jax.ShapeDtypeStruct((TOP_K, n), F32),
            jax.ShapeDtypeStruct((TOP_K, n), jnp.int32),
            jax.ShapeDtypeStruct((N_EXPERTS, 128), jnp.int32),
        ),
        grid=(n // t,),
        in_specs=[
            pl.BlockSpec((t, D), lambda i: (i, 0)),
            pl.BlockSpec((None, N_EXPERTS, D), lambda i: (layer, 0, 0)),
        ],
        out_specs=(pair, pair, pair, pl.BlockSpec((N_EXPERTS, 128), lambda i: (0, 0))),
        scratch_shapes=[pltpu.VMEM((N_EXPERTS, t), F32)],
        compiler_params=_params(1, 16 << 20),
        name="moe_router",
    )(h, wrt)


def _row_copy(src_ref, src_row, dst_ref, dst_row, sem):
    return pltpu.make_async_copy(src_ref.at[pl.ds(src_row, 1)], dst_ref.at[pl.ds(dst_row, 1)], sem)


PREFETCH_PRIORITY = 1
DISPATCH_T = 512
ISSUE_UNROLL = 8


def _dispatch_kernel(dest_ref, pad_ref, nused_ref, h_ref, xb_hbm, zero_ref, sem, zsem, *, n_tokens, n_blocks):
    i = pl.program_id(0)
    t = DISPATCH_T
    rows = EXPERT_BLOCK

    @pl.when(i == 0)
    def _():
        zero_ref[...] = jnp.zeros(zero_ref.shape, F32)

        def zero_row(r, carry):
            _row_copy(zero_ref, 0, xb_hbm, r, zsem).start()
            return carry

        def wait_row(r, carry):
            _row_copy(zero_ref, 0, xb_hbm, 0, zsem).wait()
            return carry

        def block_copy(blk):
            return pltpu.make_async_copy(zero_ref, xb_hbm.at[pl.ds(pl.multiple_of(blk * rows, rows), rows)], zsem)

        def zero_block(blk, carry):
            block_copy(blk).start()
            return carry

        def wait_block(blk, carry):
            block_copy(0).wait()
            return carry

        for e in range(N_EXPERTS):
            lax.fori_loop(pad_ref[e], pad_ref[N_EXPERTS + e], zero_row, 0)
        lax.fori_loop(nused_ref[0], n_blocks, zero_block, 0)
        for e in range(N_EXPERTS):
            lax.fori_loop(pad_ref[e], pad_ref[N_EXPERTS + e], wait_row, 0)
        lax.fori_loop(nused_ref[0], n_blocks, wait_block, 0)

    for r in range(t):
        tok = i * t + r
        _row_copy(h_ref, r, xb_hbm, dest_ref[tok], sem).start(priority=0)
        _row_copy(h_ref, r, xb_hbm, dest_ref[n_tokens + tok], sem).start(priority=1)
    for _ in range(TOP_K):
        pltpu.make_async_copy(h_ref, xb_hbm.at[pl.ds(0, t)], sem).wait()


def moe_dispatch(h, dest, pad_rows, nused, n_blocks):
    n, d = h.shape
    t = DISPATCH_T
    rows = n_blocks * EXPERT_BLOCK
    kernel = functools.partial(_dispatch_kernel, n_tokens=n, n_blocks=n_blocks)
    return pl.pallas_call(
        kernel,
        out_shape=jax.ShapeDtypeStruct((rows, d), F32),
        grid_spec=pltpu.PrefetchScalarGridSpec(
            num_scalar_prefetch=3,
            grid=(n // t,),
            in_specs=[pl.BlockSpec((t, d), lambda i, *_: (i, 0))],
            out_specs=pl.BlockSpec(memory_space=pl.ANY),
            scratch_shapes=[pltpu.VMEM((EXPERT_BLOCK, d), F32), pltpu.SemaphoreType.DMA(()),
                            pltpu.SemaphoreType.DMA(())],
        ),
        compiler_params=_params(1, 24 << 20),
        name="moe_dispatch",
    )(dest, pad_rows, nused, h)


def _moe_up_kernel(be_ref, nused_ref, first_ref, nxte_ref, last_ref, x_ref, wg_hbm, wu_hbm, o_ref,
                   wgb_ref, wub_ref, sg_ref, su_ref, sems, *, layer, tn, n_col_tiles):
    j = pl.program_id(0)
    i = pl.program_id(1)

    def fetch(e, jj):
        cols = pl.ds(pl.multiple_of(jj * tn, 128), tn)
        return (pltpu.make_async_copy(wg_hbm.at[layer, e, :, cols], sg_ref, sems.at[0]),
                pltpu.make_async_copy(wu_hbm.at[layer, e, :, cols], su_ref, sems.at[1]))

    @pl.when(first_ref[i] == 1)
    def _():
        @pl.when(jnp.logical_and(j == 0, i == 0))
        def _():
            for c in fetch(be_ref[0], 0):
                c.start()

        for c in fetch(be_ref[i], j):
            c.wait()
        _cast_rows(sg_ref, wgb_ref)
        _cast_rows(su_ref, wub_ref)
        nj = j + last_ref[i]

        @pl.when(nj < n_col_tiles)
        def _():
            for c in fetch(nxte_ref[i], nj):
                c.start(priority=PREFETCH_PRIORITY)

    @pl.when(i < nused_ref[0])
    def _():
        x = x_ref[...].astype(BF16)
        g = jnp.dot(x, wgb_ref[...], preferred_element_type=F32)
        u = jnp.dot(x, wub_ref[...], preferred_element_type=F32)
        o_ref[...] = (jax.nn.silu(g) * u).astype(o_ref.dtype)

    @pl.when(i >= nused_ref[0])
    def _():
        o_ref[...] = jnp.zeros(o_ref.shape, o_ref.dtype)


def _group_schedule(block_e):
    nb = block_e.shape[0]
    idx = jnp.arange(nb, dtype=jnp.int32)
    first = jnp.concatenate([jnp.ones((1,), bool), block_e[1:] != block_e[:-1]])
    later_first = jnp.logical_and(first[None, :], idx[None, :] > idx[:, None])
    nxt_idx = jnp.min(jnp.where(later_first, idx[None, :], nb), axis=1)
    last = nxt_idx >= nb
    nxt_idx = jnp.where(last, 0, nxt_idx)
    nxt_e = jnp.sum(jnp.where(idx[None, :] == nxt_idx[:, None], block_e[None, :], 0), axis=1)
    return first.astype(jnp.int32), nxt_e.astype(jnp.int32), last.astype(jnp.int32)


def moe_up(xb, w_gate, w_up, layer, block_e, nused, schedule, *, tn=1408):
    p, k = xb.shape
    rows = EXPERT_BLOCK
    f = w_gate.shape[-1]
    assert f % tn == 0 and tn % 128 == 0
    first, nxt_e, last = schedule
    kernel = functools.partial(_moe_up_kernel, layer=layer, tn=tn, n_col_tiles=f // tn)
    hbm = pl.BlockSpec(memory_space=pl.ANY)
    return pl.pallas_call(
        kernel,
        out_shape=jax.ShapeDtypeStruct((p, f), BF16),
        grid_spec=pltpu.PrefetchScalarGridSpec(
            num_scalar_prefetch=5,
            grid=(f // tn, p // rows),
            in_specs=[pl.BlockSpec((rows, k), lambda j, i, *_: (i, 0)), hbm, hbm],
            out_specs=pl.BlockSpec((rows, tn), lambda j, i, *_: (i, j)),
            scratch_shapes=[pltpu.VMEM((k, tn), BF16), pltpu.VMEM((k, tn), BF16),
                            pltpu.VMEM((k, tn), F32), pltpu.VMEM((k, tn), F32),
                            pltpu.SemaphoreType.DMA((2,))],
        ),
        compiler_params=_params(2, 52 << 20),
        name="moe_up",
    )(block_e, nused, first, nxt_e, last, xb, w_gate, w_up)


def _moe_down_kernel(be_ref, nused_ref, first_ref, nxte_ref, last_ref, x_ref, w_hbm, o_ref,
                     wb_ref, st_ref, sem, *, layer):
    i = pl.program_id(0)

    def fetch(e):
        return pltpu.make_async_copy(w_hbm.at[layer, e], st_ref, sem)

    @pl.when(first_ref[i] == 1)
    def _():
        @pl.when(i == 0)
        def _():
            fetch(be_ref[0]).start()

        fetch(be_ref[i]).wait()
        _cast_rows(st_ref, wb_ref)

        @pl.when(last_ref[i] == 0)
        def _():
            fetch(nxte_ref[i]).start(priority=PREFETCH_PRIORITY)

    @pl.when(i < nused_ref[0])
    def _():
        o_ref[...] = jnp.dot(x_ref[...], wb_ref[...], preferred_element_type=F32)

    @pl.when(i >= nused_ref[0])
    def _():
        o_ref[...] = jnp.zeros(o_ref.shape, o_ref.dtype)


def moe_down(hb, w_down, layer, block_e, nused, schedule):
    p, k = hb.shape
    rows = EXPERT_BLOCK
    n = w_down.shape[-1]
    first, nxt_e, last = schedule
    kernel = functools.partial(_moe_down_kernel, layer=layer)
    return pl.pallas_call(
        kernel,
        out_shape=jax.ShapeDtypeStruct((p, n), F32),
        grid_spec=pltpu.PrefetchScalarGridSpec(
            num_scalar_prefetch=5,
            grid=(p // rows,),
            in_specs=[pl.BlockSpec((rows, k), lambda i, *_: (i, 0)), pl.BlockSpec(memory_space=pl.ANY)],
            out_specs=pl.BlockSpec((rows, n), lambda i, *_: (i, 0)),
            scratch_shapes=[pltpu.VMEM((k, n), BF16), pltpu.VMEM((k, n), F32), pltpu.SemaphoreType.DMA(())],
        ),
        compiler_params=_params(1, 50 << 20),
        name="moe_down",
    )(block_e, nused, first, nxt_e, last, hb, w_down)


COMBINE_T = 256


COMBINE_GROUP = 32


def _combine_kernel(dest_ref, y_hbm, wt_ref, h_ref, g_ref, b_ref, of_ref, ob_ref,
                    buf_ref, sems, *, n_tokens):
    i = pl.program_id(0)
    nb = pl.num_programs(0)
    t = COMBINE_T
    slot = i % 2

    def issue(block, to_slot, r):
        tok = block * t + r
        _row_copy(y_hbm, dest_ref[tok], buf_ref.at[to_slot, 0], r, sems.at[to_slot]).start(priority=0)
        _row_copy(y_hbm, dest_ref[n_tokens + tok], buf_ref.at[to_slot, 1], r, sems.at[to_slot]).start(priority=1)

    def drain(from_slot):
        for k in range(TOP_K):
            pltpu.make_async_copy(y_hbm.at[pl.ds(0, t)], buf_ref.at[from_slot, k], sems.at[from_slot]).wait()

    @pl.when(i == 0)
    def _():
        def body(r, carry):
            issue(0, 0, r)
            return carry

        lax.fori_loop(0, t, body, 0, unroll=ISSUE_UNROLL)

    drain(slot)
    nxt = jnp.minimum(i + 1, nb - 1)
    for r0 in range(0, t, COMBINE_GROUP):
        sl = slice(r0, r0 + COMBINE_GROUP)
        ff = buf_ref[slot, 0, sl, :] * wt_ref[sl, 0:1] + buf_ref[slot, 1, sl, :] * wt_ref[sl, 1:2]
        y = _layer_norm(DN_ALPHA * h_ref[sl, :] + ff, g_ref[...], b_ref[...])
        of_ref[sl, :] = y
        ob_ref[sl, :] = y.astype(BF16)
        for r in range(r0, r0 + COMBINE_GROUP):
            issue(nxt, 1 - slot, r)

    @pl.when(i == nb - 1)
    def _():
        drain(1 - slot)


def moe_combine_ln(yb, dest, wts_t, h, ln_g, ln_b, layer, which):
    n, d = h.shape
    t = COMBINE_T
    row = pl.BlockSpec((t, d), lambda i, ds: (i, 0))
    par = pl.BlockSpec((None, None, 1, d), lambda i, ds: (layer, which, 0, 0))
    kernel = functools.partial(_combine_kernel, n_tokens=n)
    return pl.pallas_call(
        kernel,
        out_shape=(jax.ShapeDtypeStruct((n, d), F32), jax.ShapeDtypeStruct((n, d), BF16)),
        grid_spec=pltpu.PrefetchScalarGridSpec(
            num_scalar_prefetch=1,
            grid=(n // t,),
            in_specs=[
                pl.BlockSpec(memory_space=pl.ANY),
                pl.BlockSpec((t, TOP_K), lambda i, ds: (i, 0)),
                row, par, par,
            ],
            out_specs=(row, row),
            scratch_shapes=[pltpu.VMEM((2, TOP_K, t, d), F32), pltpu.SemaphoreType.DMA((2,))],
        ),
        compiler_params=_params(1, 48 << 20),
        name="moe_combine_ln",
    )(dest, yb, wts_t, h, ln_g, ln_b)


def moe_layer(h, hb, w_router, w_gate, w_up, w_down, ln_g, ln_b, layer, moe_idx):
    n = h.shape[0]
    m = n * TOP_K
    n_blocks = (m + N_EXPERTS * (EXPERT_BLOCK - 1) + EXPERT_BLOCK - 1) // EXPERT_BLOCK
    e, wts, rank, cnt = moe_router(h, w_router, moe_idx)
    counts = cnt[:, 0]
    padded = (counts + EXPERT_BLOCK - 1) // EXPERT_BLOCK * EXPERT_BLOCK
    pad_ends = jnp.cumsum(padded)
    pad_starts = pad_ends - padded
    expert_ids = jnp.arange(N_EXPERTS, dtype=jnp.int32)[:, None, None]
    dest = jnp.sum(jnp.where(e[None] == expert_ids, pad_starts[:, None, None], 0), axis=0) + rank
    block_start = jnp.arange(n_blocks, dtype=jnp.int32) * EXPERT_BLOCK
    block_e = jnp.minimum(jnp.sum(block_start[:, None] >= pad_ends[None, :], axis=1), N_EXPERTS - 1).astype(jnp.int32)
    nused = (pad_ends[-1:] // EXPERT_BLOCK).astype(jnp.int32)

    dest = dest.reshape(-1)
    pad_rows = jnp.concatenate([pad_starts + counts, pad_ends]).astype(jnp.int32)
    schedule = _group_schedule(block_e)
    xb = moe_dispatch(h, dest, pad_rows, nused, n_blocks)
    hid = moe_up(xb, w_gate, w_up, moe_idx, block_e, nused, schedule)
    yb = moe_down(hid, w_down, moe_idx, block_e, nused, schedule)
    return moe_combine_ln(yb, dest, wts.T, h, ln_g, ln_b, layer, 2)


def kernel(x, mem, rel_table, mix_w_in, gmlp_ln_g, gmlp_ln_b, gmlp_w_s, gmlp_b_s, pool_w, pool_scale, mix_w_out, diff_w_qkv, diff_lam_q1, diff_lam_k1, diff_lam_q2, diff_lam_k2, diff_subln_g, diff_w_o, xa_w_q, xa_w_kv, xa_w_o, ffn_w_gate, ffn_w_up, ffn_w_down, moe_w_router, moe_w_gate, moe_w_up, moe_w_down, ln_g, ln_b):
    batch, seq_len, d = x.shape
    mem_len = mem.shape[1]
    n = batch * seq_len
    h = x.reshape(n, d)
    hb = h
    memf = mem.reshape(batch * mem_len, d)
    ln_g4 = ln_g.reshape(DEPTH, 3, 1, d)
    ln_b4 = ln_b.reshape(DEPTH, 3, 1, d)
    lam_params = jnp.stack([diff_lam_q1, diff_lam_k1, diff_lam_q2, diff_lam_k2], axis=1)
    bias_tiles = attention_bias_tiles(rel_table)
    kv_all = matmul_all_layers(memf, xa_w_kv, tn=1024, out_dtype=BF16)
    xa_qk, xa_vo = cross_attention_folds(kv_all, xa_w_q, xa_w_o, batch, mem_len)

    for layer in range(DEPTH):
        i = layer // 2
        if layer % 2 == 0:
            hm = matmul(hb, mix_w_in, i, tm=512, tn=MIX_IN // 2, out_dtype=F32)
            mixed = gmlp_pool_mixer(hm, seq_len, gmlp_ln_g, gmlp_ln_b, gmlp_w_s, gmlp_b_s, pool_w,
                                    pool_scale, i)
            h, hb = matmul_residual_ln(mixed, mix_w_out, i, h, ln_g4, ln_b4, layer, 0)
        else:
            lam_init = 0.8 - 0.6 * math.exp(-0.3 * layer)
            qkv = matmul(hb, diff_w_qkv, i, tm=1024, tn=1024, out_dtype=BF16,
                         scaled_cols=D, scale=LOG2E * C_HEAD_DIM ** -0.5)
            att = diff_attention(qkv, bias_tiles, lam_params, diff_subln_g, i, lam_init, batch, seq_len)
            h, hb = matmul_residual_ln(att, diff_w_o, i, h, ln_g4, ln_b4, layer, 0)

        h, hb = cross_attention_residual_ln(hb, h, xa_qk, xa_vo, ln_g4, ln_b4, layer, 1, seq_len, mem_len)

        if layer % 2 == 0:
            hid = swiglu_up(hb, ffn_w_gate, ffn_w_up, i, tm=1024, tn=512)
            h, hb = matmul_kacc_residual_ln(hid, ffn_w_down, i, h, ln_g4, ln_b4, layer, 2,
                                            tm=1024, tk=512, single_buffer_rows=True)
        else:
            h, hb = moe_layer(h, hb, moe_w_router, moe_w_gate, moe_w_up, moe_w_down, ln_g4, ln_b4,
                              layer, i)
    return h.reshape(batch, seq_len, d)
```

```python
import functools
import math

import numpy as np
import jax
import jax.numpy as jnp
from jax import lax
from jax.experimental import pallas as pl
from jax.experimental.pallas import tpu as pltpu

F32 = jnp.float32
BF16 = jnp.bfloat16

D = 2048
DEPTH = 4
CHUNK = 64
GMLP_CHUNK = 128
A_WIDTH = D // 2
A_HEADS = 8
A_HEAD_DIM = A_WIDTH // A_HEADS
B_WIDTH = D // 2
POOL_WINDOWS = (2, 4, 8, 16)
B_GROUP = B_WIDTH // len(POOL_WINDOWS)
MIX_IN = 2 * A_WIDTH + B_WIDTH
C_HEADS = 16
C_HEAD_DIM = D // (2 * C_HEADS)
C_V_DIM = 2 * C_HEAD_DIM
REL_BUCKETS = 32
REL_MAX_DIST = 128
X_HEADS = 4
X_HEAD_DIM = D // X_HEADS
N_EXPERTS = 8
TOP_K = 2
EXPERT_BLOCK = 256
DN_ALPHA = (2 * DEPTH) ** 0.25
LN_EPS = 1e-5
NEG = -1e30

VMEM_CAP_BYTES = 56 * 1024 * 1024
CAST_ROWS = 256
ATT_TQ = 256
POOL_HALO = 16
PREFETCH_PRIORITY = 1


def _params(n_axes, vmem_bytes):
    return pltpu.CompilerParams(
        dimension_semantics=("arbitrary",) * n_axes,
        vmem_limit_bytes=int(min(max(vmem_bytes, 16 * 1024 * 1024), VMEM_CAP_BYTES)),
    )


def _cast_rows(w_ref, wb_ref):
    k = w_ref.shape[0]
    rows = CAST_ROWS if k % CAST_ROWS == 0 else k

    def body(c, carry):
        r = pl.multiple_of(c * rows, rows)
        wb_ref[pl.ds(r, rows), :] = w_ref[pl.ds(r, rows), :].astype(BF16)
        return carry

    lax.fori_loop(0, k // rows, body, 0)


def _layer_norm(x, g, b):
    mu = jnp.mean(x, axis=-1, keepdims=True)
    xc = x - mu
    var = jnp.mean(xc * xc, axis=-1, keepdims=True)
    return xc * lax.rsqrt(var + LN_EPS) * g + b


def _column_tile_fetch(w_hbm, layer, tn, st_ref, sem):
    def fetch(jj):
        cols = pl.ds(pl.multiple_of(jj * tn, 128), tn)
        return pltpu.make_async_copy(w_hbm.at[layer, :, cols], st_ref, sem)
    return fetch


def _next_weight_tile(fetches, casts):
    j = pl.program_id(0)

    @pl.when(pl.program_id(1) == 0)
    def _():
        @pl.when(j == 0)
        def _():
            for fetch in fetches:
                fetch(0).start()

        for fetch, (st_ref, wb_ref) in zip(fetches, casts):
            fetch(j).wait()
            _cast_rows(st_ref, wb_ref)

        @pl.when(j + 1 < pl.num_programs(0))
        def _():
            for fetch in fetches:
                fetch(j + 1).start(priority=PREFETCH_PRIORITY)


def _mm_kernel(a_ref, w_hbm, o_ref, wb_ref, st_ref, sem, *, layer, tn, scaled_tiles, scale):
    j = pl.program_id(0)
    _next_weight_tile([_column_tile_fetch(w_hbm, layer, tn, st_ref, sem)], [(st_ref, wb_ref)])
    a = a_ref[...].astype(BF16)
    y = jnp.dot(a, wb_ref[...], preferred_element_type=F32)
    if scaled_tiles:
        y = y * jnp.where(j < scaled_tiles, scale, 1.0)
    o_ref[...] = y.astype(o_ref.dtype)


def matmul(a, w, layer, *, tm, tn, out_dtype, scaled_cols=0, scale=1.0):
    m, k = a.shape
    n = w.shape[-1]
    assert m % tm == 0 and n % tn == 0 and tn % 128 == 0 and w.shape[-2] == k and scaled_cols % tn == 0
    vmem = 2 * tm * k * a.dtype.itemsize + k * tn * 4 + k * tn * 2 + 2 * tm * tn * 4 + tm * tn * 4
    kernel = functools.partial(_mm_kernel, layer=layer, tn=tn, scaled_tiles=scaled_cols // tn, scale=scale)
    return pl.pallas_call(
        kernel,
        out_shape=jax.ShapeDtypeStruct((m, n), out_dtype),
        grid=(n // tn, m // tm),
        in_specs=[
            pl.BlockSpec((tm, k), lambda j, i: (i, 0)),
            pl.BlockSpec(memory_space=pl.ANY),
        ],
        out_specs=pl.BlockSpec((tm, tn), lambda j, i: (i, j)),
        scratch_shapes=[pltpu.VMEM((k, tn), BF16), pltpu.VMEM((k, tn), F32), pltpu.SemaphoreType.DMA(())],
        compiler_params=_params(2, vmem + (4 << 20)),
        name="matmul",
    )(a, w)


MMLN_GROUPS = 2


def _mm_ln_kernel(a_ref, w_ref, h_ref, g_ref, b_ref, of_ref, ob_ref, wb_ref):
    @pl.when(pl.program_id(0) == 0)
    def _():
        _cast_rows(w_ref, wb_ref)

    rows = a_ref.shape[0] // MMLN_GROUPS
    for r in range(MMLN_GROUPS):
        sl = slice(r * rows, (r + 1) * rows)
        y = jnp.dot(a_ref[sl, :], wb_ref[...], preferred_element_type=F32)
        out = _layer_norm(DN_ALPHA * h_ref[sl, :] + y, g_ref[...], b_ref[...])
        of_ref[sl, :] = out
        ob_ref[sl, :] = out.astype(BF16)


def matmul_residual_ln(a, w, w_layer, h, ln_g, ln_b, layer, which, *, tm=512):
    m, k = a.shape
    d = w.shape[-1]
    assert m % tm == 0 and h.shape == (m, d)
    row = pl.BlockSpec((tm, d), lambda i: (i, 0))
    par = pl.BlockSpec((None, None, 1, d), lambda i: (layer, which, 0, 0))
    vmem = k * d * 4 + k * d * 2 + 2 * tm * k * 2 + 4 * tm * d * 4 + 2 * tm * d * 2 + 3 * tm * d * 4
    return pl.pallas_call(
        _mm_ln_kernel,
        out_shape=(jax.ShapeDtypeStruct((m, d), F32), jax.ShapeDtypeStruct((m, d), BF16)),
        grid=(m // tm,),
        in_specs=[
            pl.BlockSpec((tm, k), lambda i: (i, 0)),
            pl.BlockSpec((None, k, d), lambda i: (w_layer, 0, 0), pipeline_mode=pl.Buffered(1)),
            row, par, par,
        ],
        out_specs=(row, row),
        scratch_shapes=[pltpu.VMEM((k, d), BF16)],
        compiler_params=_params(1, vmem + (4 << 20)),
        name="matmul_residual_ln",
    )(a, w, h, ln_g, ln_b)


KACC_GROUP_ROWS = 256


def _mm_kacc_ln_kernel(a_ref, w_ref, h_ref, g_ref, b_ref, of_ref, ob_ref, acc_ref):
    k = pl.program_id(1)

    @pl.when(jnp.logical_and(pl.program_id(0) == 0, k == 0))
    def _():
        acc_ref[...] = jnp.zeros(acc_ref.shape, F32)

    part = jnp.dot(a_ref[...], w_ref[...].astype(BF16), preferred_element_type=F32)
    acc_ref[...] = jnp.where(k > 0, acc_ref[...], 0.0) + part

    @pl.when(k == pl.num_programs(1) - 1)
    def _():
        for r in range(0, acc_ref.shape[0], KACC_GROUP_ROWS):
            sl = slice(r, r + KACC_GROUP_ROWS)
            out = _layer_norm(DN_ALPHA * h_ref[sl, :] + acc_ref[sl, :], g_ref[...], b_ref[...])
            of_ref[sl, :] = out
            ob_ref[sl, :] = out.astype(BF16)


def matmul_kacc_residual_ln(a, w, w_layer, h, ln_g, ln_b, layer, which, *, tm, tk, single_buffer_rows):
    m, kk = a.shape
    d = w.shape[-1]
    assert m % tm == 0 and kk % tk == 0 and tm % KACC_GROUP_ROWS == 0
    mode = dict(pipeline_mode=pl.Buffered(1)) if single_buffer_rows else {}
    row = pl.BlockSpec((tm, d), lambda i, k: (i, 0))
    out_row = pl.BlockSpec((tm, d), lambda i, k: (i, 0), **mode)
    par = pl.BlockSpec((None, None, 1, d), lambda i, k: (layer, which, 0, 0))
    return pl.pallas_call(
        _mm_kacc_ln_kernel,
        out_shape=(jax.ShapeDtypeStruct((m, d), F32), jax.ShapeDtypeStruct((m, d), BF16)),
        grid=(m // tm, kk // tk),
        in_specs=[
            pl.BlockSpec((tm, tk), lambda i, k: (i, k)),
            pl.BlockSpec((None, tk, d), lambda i, k: (w_layer, k, 0)),
            row, par, par,
        ],
        out_specs=(out_row, out_row),
        scratch_shapes=[pltpu.VMEM((tm, d), F32)],
        compiler_params=_params(2, VMEM_CAP_BYTES),
        name="matmul_kacc_residual_ln",
    )(a, w, h, ln_g, ln_b)


def _mm_layers_kernel(a_ref, w_ref, o_ref):
    o_ref[...] = jnp.dot(a_ref[...].astype(BF16), w_ref[...].astype(BF16),
                         preferred_element_type=F32).astype(o_ref.dtype)


def matmul_all_layers(a, w, *, tn, out_dtype):
    m, k = a.shape
    layers, _, n = w.shape
    assert n % tn == 0
    return pl.pallas_call(
        _mm_layers_kernel,
        out_shape=jax.ShapeDtypeStruct((layers, m, n), out_dtype),
        grid=(layers, n // tn),
        in_specs=[
            pl.BlockSpec((m, k), lambda l, j: (0, 0)),
            pl.BlockSpec((None, k, tn), lambda l, j: (l, 0, j)),
        ],
        out_specs=pl.BlockSpec((None, m, tn), lambda l, j: (l, 0, j)),
        compiler_params=_params(2, 2 * m * k * 4 + 3 * k * tn * 4 + 3 * m * tn * 4 + (4 << 20)),
        name="matmul_all_layers",
    )(a, w)


def _swiglu_kernel(a_ref, wg_hbm, wu_hbm, o_ref, wgb_ref, wub_ref, sg_ref, su_ref, sems, *, layer, tn):
    _next_weight_tile([_column_tile_fetch(wg_hbm, layer, tn, sg_ref, sems.at[0]),
                       _column_tile_fetch(wu_hbm, layer, tn, su_ref, sems.at[1])],
                      [(sg_ref, wgb_ref), (su_ref, wub_ref)])
    a = a_ref[...]
    g = jnp.dot(a, wgb_ref[...], preferred_element_type=F32)
    u = jnp.dot(a, wub_ref[...], preferred_element_type=F32)
    o_ref[...] = (jax.nn.silu(g) * u).astype(o_ref.dtype)


def swiglu_up(a, wg, wu, layer, *, tm, tn):
    m, k = a.shape
    n = wg.shape[-1]
    assert m % tm == 0 and n % tn == 0 and tn % 128 == 0
    vmem = 2 * tm * k * 2 + 2 * k * tn * 4 + 2 * k * tn * 2 + 2 * tm * tn * 2 + 3 * tm * tn * 4
    hbm = pl.BlockSpec(memory_space=pl.ANY)
    kernel = functools.partial(_swiglu_kernel, layer=layer, tn=tn)
    return pl.pallas_call(
        kernel,
        out_shape=jax.ShapeDtypeStruct((m, n), BF16),
        grid=(n // tn, m // tm),
        in_specs=[pl.BlockSpec((tm, k), lambda j, i: (i, 0)), hbm, hbm],
        out_specs=pl.BlockSpec((tm, tn), lambda j, i: (i, j)),
        scratch_shapes=[pltpu.VMEM((k, tn), BF16), pltpu.VMEM((k, tn), BF16),
                        pltpu.VMEM((k, tn), F32), pltpu.VMEM((k, tn), F32), pltpu.SemaphoreType.DMA((2,))],
        compiler_params=_params(2, vmem + (4 << 20)),
        name="swiglu_up",
    )(a, wg, wu)


MIX_ROWS = 2 * GMLP_CHUNK


def _mixer_kernel(hm_ref, prev_ref, lng_ref, lnb_ref, ws_ref, bst_ref, pw_ref, ps_ref, o_ref,
                  wbd_ref, ext_ref, dbl_ref, *, blocks_per_seq):
    i = pl.program_id(0)

    @pl.when(i == 0)
    def _():
        p = lax.broadcasted_iota(jnp.int32, (GMLP_CHUNK, GMLP_CHUNK), 0)
        q = lax.broadcasted_iota(jnp.int32, (GMLP_CHUNK, GMLP_CHUNK), 1)
        allowed = (q // CHUNK) <= (p // CHUNK)
        wbd_ref[...] = jnp.zeros(wbd_ref.shape, BF16)
        for h in range(A_HEADS):
            w = jnp.where(allowed, ws_ref[h], 0.0).astype(BF16)
            wbd_ref[h, 0:GMLP_CHUNK, 0:GMLP_CHUNK] = w
            wbd_ref[h, GMLP_CHUNK:MIX_ROWS, GMLP_CHUNK:MIX_ROWS] = w

    z = jax.nn.gelu(hm_ref[:, 0:2 * A_WIDTH])
    u = z[:, 0:A_WIDTH]
    v = _layer_norm(z[:, A_WIDTH:2 * A_WIDTH], lng_ref[...], lnb_ref[...]).astype(BF16)
    for h in range(A_HEADS):
        c0 = h * A_HEAD_DIM
        sv = jnp.dot(wbd_ref[h], v[:, c0:c0 + A_HEAD_DIM], preferred_element_type=F32)
        sv = sv + bst_ref[:, h:h + 1]
        o_ref[:, c0:c0 + A_HEAD_DIM] = (u[:, c0:c0 + A_HEAD_DIM] * sv).astype(BF16)

    blk = i % blocks_per_seq
    prev = jnp.where(blk == 0, 0.0, prev_ref[...])
    ext_ref[0:POOL_HALO, :] = prev
    ext_ref[POOL_HALO:POOL_HALO + MIX_ROWS, :] = hm_ref[:, 2 * A_WIDTH:MIX_IN]
    t1 = blk * MIX_ROWS + lax.broadcasted_iota(jnp.int32, (MIX_ROWS, 1), 0) + 1
    for g, w in enumerate(POOL_WINDOWS):
        c0 = g * B_GROUP
        cols = slice(c0, c0 + B_GROUP)
        pin = ext_ref[POOL_HALO:POOL_HALO + MIX_ROWS, cols]
        src_ref, src_cols = ext_ref, cols
        k = 1
        while k < w:
            lo = 2 * k - 1
            n = POOL_HALO + MIX_ROWS - lo
            dbl_ref[lo:lo + n, :] = src_ref[lo:lo + n, src_cols] + src_ref[lo - k:lo - k + n, src_cols]
            src_ref, src_cols = dbl_ref, slice(None)
            k *= 2
        s = dbl_ref[POOL_HALO:POOL_HALO + MIX_ROWS, :]
        cnt = jnp.minimum(t1, w).astype(F32)
        pooled = s / cnt - pin
        b = jnp.dot(pooled.astype(BF16), pw_ref[g].astype(BF16), preferred_element_type=F32)
        b = b * ps_ref[:, c0:c0 + B_GROUP]
        o_ref[:, A_WIDTH + c0:A_WIDTH + c0 + B_GROUP] = b.astype(BF16)


def gmlp_pool_mixer(hm, seq_len, ln_g, ln_b, w_s, b_s, pool_w, pool_scale, layer):
    n = hm.shape[0]
    assert seq_len % MIX_ROWS == 0 and max(POOL_WINDOWS) <= POOL_HALO
    blocks_per_seq = seq_len // MIX_ROWS
    halo_per_block = MIX_ROWS // POOL_HALO
    bst = jnp.tile(b_s[layer].T, (MIX_ROWS // GMLP_CHUNK, 1))
    kernel = functools.partial(_mixer_kernel, blocks_per_seq=blocks_per_seq)
    return pl.pallas_call(
        kernel,
        out_shape=jax.ShapeDtypeStruct((n, D), BF16),
        grid=(n // MIX_ROWS,),
        in_specs=[
            pl.BlockSpec((MIX_ROWS, MIX_IN), lambda i: (i, 0)),
            pl.BlockSpec((POOL_HALO, B_WIDTH),
                         lambda i: (jnp.maximum(i * halo_per_block - 1, 0), 2 * A_WIDTH // B_WIDTH)),
            pl.BlockSpec((None, 1, A_WIDTH), lambda i: (layer, 0, 0)),
            pl.BlockSpec((None, 1, A_WIDTH), lambda i: (layer, 0, 0)),
            pl.BlockSpec((None, A_HEADS, GMLP_CHUNK, GMLP_CHUNK), lambda i: (layer, 0, 0, 0)),
            pl.BlockSpec((MIX_ROWS, A_HEADS), lambda i: (0, 0)),
            pl.BlockSpec((None, len(POOL_WINDOWS), B_GROUP, B_GROUP), lambda i: (layer, 0, 0, 0)),
            pl.BlockSpec((None, 1, B_WIDTH), lambda i: (layer, 0, 0)),
        ],
        out_specs=pl.BlockSpec((MIX_ROWS, D), lambda i: (i, 0)),
        scratch_shapes=[
            pltpu.VMEM((A_HEADS, MIX_ROWS, MIX_ROWS), BF16),
            pltpu.VMEM((POOL_HALO + MIX_ROWS, B_WIDTH), F32),
            pltpu.VMEM((POOL_HALO + MIX_ROWS, B_GROUP), F32),
        ],
        compiler_params=_params(1, 40 << 20),
        name="gmlp_pool_mixer",
    )(hm, hm, ln_g.reshape(-1, 1, A_WIDTH), ln_b.reshape(-1, 1, A_WIDTH), w_s, bst, pool_w,
      pool_scale.reshape(-1, 1, B_WIDTH))


def _t5_bucket_np(rel):
    half = REL_BUCKETS // 2
    max_exact = half // 2
    rel = np.asarray(rel, dtype=np.int64)
    n = np.abs(rel)
    steps = half - max_exact
    ratio = REL_MAX_DIST // max_exact
    large = np.zeros_like(n)
    nn = n.astype(object) ** steps
    for j in range(1, steps + 1):
        large = large + (nn >= (max_exact ** steps) * (ratio ** j)).astype(np.int64)
    large = np.minimum(max_exact + large, half - 1)
    return np.where(rel > 0, half, 0) + np.where(n < max_exact, n, large)


def _bias_bucket_tiles():
    qi = np.arange(ATT_TQ)[:, None]
    kj = np.arange(ATT_TQ)[None, :]
    diag = _t5_bucket_np(kj - qi)
    diag = np.where((kj // CHUNK) <= (qi // CHUNK), diag, -1)
    prev = _t5_bucket_np(kj - ATT_TQ - qi)
    return np.stack([diag, prev], 0).astype(np.int32)


FAR_BUCKET = REL_BUCKETS // 2 - 1
LOG2E = math.log2(math.e)


def _bias_kernel(rel_ref, bk_ref, o_ref):
    h = pl.program_id(0)
    bk = bk_ref[...]
    acc = jnp.zeros(bk.shape, F32)
    for b in range(REL_BUCKETS):
        acc = jnp.where(bk == b, rel_ref[b, h], acc)
    o_ref[...] = jnp.where(bk < 0, NEG, (acc - rel_ref[FAR_BUCKET, h]) * LOG2E)


def attention_bias_tiles(rel_table):
    assert ATT_TQ >= REL_MAX_DIST and ATT_TQ % CHUNK == 0
    bk = jnp.asarray(_bias_bucket_tiles())
    return pl.pallas_call(
        _bias_kernel,
        out_shape=jax.ShapeDtypeStruct((C_HEADS, 2, ATT_TQ, ATT_TQ), F32),
        grid=(C_HEADS,),
        in_specs=[
            pl.BlockSpec(memory_space=pltpu.SMEM),
            pl.BlockSpec((2, ATT_TQ, ATT_TQ), lambda h: (0, 0, 0)),
        ],
        out_specs=pl.BlockSpec((None, 2, ATT_TQ, ATT_TQ), lambda h: (h, 0, 0, 0)),
        compiler_params=_params(1, 16 << 20),
        name="attention_bias_tiles",
    )(rel_table, bk)


ATT_HEADS_PER_STEP = 2


def _diff_attn_kernel(q_ref, k_ref, v_ref, bias_ref, lam_ref, g_ref, o_ref, vx_ref, *, lam_init, seq_len):
    tq = ATT_TQ
    nt = (((1,), (1,)), ((), ()))
    lp = lam_ref[...]
    lam = (jnp.exp(jnp.sum(lp[0:1] * lp[1:2], axis=1, keepdims=True))
           - jnp.exp(jnp.sum(lp[2:3] * lp[3:4], axis=1, keepdims=True)) + lam_init)
    lane = lax.broadcasted_iota(jnp.int32, (tq, C_V_DIM), 1)
    for hh in range(ATT_HEADS_PER_STEP):
        hc = slice(hh * C_V_DIM, (hh + 1) * C_V_DIM)
        vx_ref[hh, :, 0:C_V_DIM] = v_ref[:, hc]
        vx_ref[hh, :, C_V_DIM:2 * C_V_DIM] = jnp.ones((seq_len, C_V_DIM), BF16)
        for qi in range(seq_len // tq):
            q = q_ref[qi * tq:(qi + 1) * tq, hc]
            zero = jnp.zeros_like(q)
            parts = []
            if qi >= 2:
                parts.append((0, (qi - 1) * tq, None))
            if qi >= 1:
                parts.append(((qi - 1) * tq, qi * tq, 1))
            parts.append((qi * tq, (qi + 1) * tq, 0))
            outs = []
            for half in range(2):
                keep = (lane < C_HEAD_DIM) if half == 0 else (lane >= C_HEAD_DIM)
                qh = jnp.where(keep, q, zero)
                scores = []
                m = None
                for a, b, bi in parts:
                    s = lax.dot_general(qh, k_ref[a:b, hc], nt, preferred_element_type=F32)
                    if bi is not None:
                        s = s + bias_ref[hh, bi]
                    scores.append(s)
                    sm = jnp.max(s, axis=1, keepdims=True)
                    m = sm if m is None else jnp.maximum(m, sm)
                acc = None
                for s, (a, b, _) in zip(scores, parts):
                    p = jnp.exp2(s - m).astype(BF16)
                    pv = jnp.dot(p, vx_ref[hh, a:b, :], preferred_element_type=F32)
                    acc = pv if acc is None else acc + pv
                outs.append(acc[:, 0:C_V_DIM] / acc[:, C_V_DIM:C_V_DIM + 1])
            o = outs[0] - lam * outs[1]
            y = o * lax.rsqrt(jnp.mean(o * o, axis=-1, keepdims=True) + LN_EPS) * g_ref[...]
            o_ref[qi * tq:(qi + 1) * tq, hc] = (y * (1.0 - lam_init)).astype(BF16)


def diff_attention(qkv, bias_tiles, lam_params, subln_g, layer, lam_init, batch, seq_len):
    n = qkv.shape[0]
    tq = ATT_TQ
    hps = ATT_HEADS_PER_STEP
    assert seq_len % tq == 0 and C_HEADS % hps == 0
    groups = C_HEADS // hps
    kernel = functools.partial(_diff_attn_kernel, lam_init=lam_init, seq_len=seq_len)
    seq = lambda off: pl.BlockSpec((seq_len, hps * C_V_DIM), lambda b, h: (b, off + h))
    return pl.pallas_call(
        kernel,
        out_shape=jax.ShapeDtypeStruct((n, D), BF16),
        grid=(batch, groups),
        in_specs=[
            seq(0), seq(groups), seq(2 * groups),
            pl.BlockSpec((hps, 2, tq, tq), lambda b, h: (h, 0, 0, 0)),
            pl.BlockSpec((None, 4, C_HEAD_DIM), lambda b, h: (layer, 0, 0)),
            pl.BlockSpec((None, 1, C_V_DIM), lambda b, h: (layer, 0, 0)),
        ],
        out_specs=seq(0),
        scratch_shapes=[pltpu.VMEM((hps, seq_len, 2 * C_V_DIM), BF16)],
        compiler_params=_params(2, 48 << 20),
        name="diff_attention",
    )(qkv, qkv, qkv, bias_tiles, lam_params, subln_g.reshape(-1, 1, C_V_DIM))


def _qk_fold_kernel(wq_ref, k_ref, o_ref, *, batch, mem_len):
    wb = wq_ref[...].astype(BF16)
    for b in range(batch):
        qk = lax.dot_general(wb, k_ref[b * mem_len:(b + 1) * mem_len, :], (((1,), (1,)), ((), ())),
                             preferred_element_type=F32)
        o_ref[b] = (qk * (LOG2E * X_HEAD_DIM ** -0.5)).astype(o_ref.dtype)


def _vo_fold_kernel(v_ref, wo_ref, o_ref, *, batch, mem_len):
    wb = wo_ref[...].astype(BF16)
    for b in range(batch):
        o_ref[b] = jnp.dot(v_ref[b * mem_len:(b + 1) * mem_len, :], wb,
                           preferred_element_type=F32).astype(o_ref.dtype)


def cross_attention_folds(kv_all, w_q, w_o, batch, mem_len):
    layers = kv_all.shape[0]
    grid = (layers, X_HEADS)
    qk = pl.pallas_call(
        functools.partial(_qk_fold_kernel, batch=batch, mem_len=mem_len),
        out_shape=jax.ShapeDtypeStruct((layers, batch, D, X_HEADS * mem_len), BF16),
        grid=grid,
        in_specs=[
            pl.BlockSpec((None, D, X_HEAD_DIM), lambda l, h: (l, 0, h)),
            pl.BlockSpec((None, batch * mem_len, X_HEAD_DIM), lambda l, h: (l, 0, h)),
        ],
        out_specs=pl.BlockSpec((None, batch, D, mem_len), lambda l, h: (l, 0, 0, h)),
        compiler_params=_params(2, 32 << 20),
        name="cross_attention_qk_fold",
    )(w_q, kv_all)
    vo = pl.pallas_call(
        functools.partial(_vo_fold_kernel, batch=batch, mem_len=mem_len),
        out_shape=jax.ShapeDtypeStruct((layers, batch, X_HEADS * mem_len, D), BF16),
        grid=grid,
        in_specs=[
            pl.BlockSpec((None, batch * mem_len, X_HEAD_DIM), lambda l, h: (l, 0, X_HEADS + h)),
            pl.BlockSpec((None, X_HEAD_DIM, D), lambda l, h: (l, h, 0)),
        ],
        out_specs=pl.BlockSpec((None, batch, mem_len, D), lambda l, h: (l, 0, h, 0)),
        compiler_params=_params(2, 32 << 20),
        name="cross_attention_vo_fold",
    )(kv_all, w_o)
    return qk, vo


XATTN_GROUPS = 2


def _xattn_ln_kernel(hb_ref, qk_ref, vo_ref, h_ref, g_ref, b_ref, of_ref, ob_ref, *, mem_len):
    rows = hb_ref.shape[0] // XATTN_GROUPS
    for r in range(XATTN_GROUPS):
        sl = slice(r * rows, (r + 1) * rows)
        s = jnp.dot(hb_ref[sl, :], qk_ref[...], preferred_element_type=F32)
        probs = []
        for hd in range(X_HEADS):
            sh = s[:, hd * mem_len:(hd + 1) * mem_len]
            e = jnp.exp2(sh - jnp.max(sh, axis=1, keepdims=True))
            inv = 1.0 / jnp.sum(e, axis=1, keepdims=True)
            probs.append((e * inv).astype(BF16))
        p = jnp.concatenate(probs, axis=1)
        y = jnp.dot(p, vo_ref[...], preferred_element_type=F32)
        out = _layer_norm(DN_ALPHA * h_ref[sl, :] + y, g_ref[...], b_ref[...])
        of_ref[sl, :] = out
        ob_ref[sl, :] = out.astype(BF16)


def cross_attention_residual_ln(hb, h, qk, vo, ln_g, ln_b, layer, which, seq_len, mem_len, *, tm=512):
    n, d = h.shape
    assert seq_len % tm == 0 and tm % XATTN_GROUPS == 0
    blocks_per_seq = seq_len // tm
    row = pl.BlockSpec((tm, d), lambda i: (i, 0))
    par = pl.BlockSpec((None, None, 1, d), lambda i: (layer, which, 0, 0))
    kw = X_HEADS * mem_len
    kernel = functools.partial(_xattn_ln_kernel, mem_len=mem_len)
    return pl.pallas_call(
        kernel,
        out_shape=(jax.ShapeDtypeStruct((n, d), F32), jax.ShapeDtypeStruct((n, d), BF16)),
        grid=(n // tm,),
        in_specs=[
            row,
            pl.BlockSpec((None, None, d, kw), lambda i: (layer, i // blocks_per_seq, 0, 0)),
            pl.BlockSpec((None, None, kw, d), lambda i: (layer, i // blocks_per_seq, 0, 0)),
            row, par, par,
        ],
        out_specs=(row, row),
        compiler_params=_params(1, 52 << 20),
        name="cross_attention_residual_ln",
    )(hb, qk, vo, h, ln_g, ln_b)


ROUTE_T = 256


def _router_kernel(h_ref, wrt_ref, e_ref, w_ref, rank_ref, cnt_ref, base_ref):
    i = pl.program_id(0)

    @pl.when(i == 0)
    def _():
        base_ref[...] = jnp.zeros(base_ref.shape, F32)

    t = ROUTE_T
    nt = (((1,), (1,)), ((), ()))
    hf = h_ref[...]
    wf = wrt_ref[...]
    hh = hf.astype(BF16)
    wh = wf.astype(BF16)
    hl = (hf - hh.astype(F32)).astype(BF16)
    wl = (wf - wh.astype(F32)).astype(BF16)
    logits = (lax.dot_general(wh, hh, nt, preferred_element_type=F32)
              + lax.dot_general(wh, hl, nt, preferred_element_type=F32)
              + lax.dot_general(wl, hh, nt, preferred_element_type=F32))
    eidx = lax.broadcasted_iota(jnp.int32, (N_EXPERTS, t), 0)
    m1 = jnp.max(logits, axis=0, keepdims=True)
    e1 = jnp.min(jnp.where(logits == m1, eidx, N_EXPERTS), axis=0, keepdims=True)
    oh1 = eidx == e1
    rest = jnp.where(oh1, -jnp.inf, logits)
    m2 = jnp.max(rest, axis=0, keepdims=True)
    e2 = jnp.min(jnp.where(rest == m2, eidx, N_EXPERTS), axis=0, keepdims=True)
    oh2 = eidx == e2
    ex = jnp.exp(m2 - m1)
    den = 1.0 + ex
    e_ref[0:1, :] = e1
    e_ref[1:2, :] = e2
    w_ref[0:1, :] = 1.0 / den
    w_ref[1:2, :] = ex / den

    oh = (oh1.astype(F32) + oh2.astype(F32)).astype(BF16)
    r = lax.broadcasted_iota(jnp.int32, (t, t), 0)
    c = lax.broadcasted_iota(jnp.int32, (t, t), 1)
    before = (r < c).astype(BF16)
    ones = jnp.ones((t, t), BF16)
    pos = base_ref[...] + jnp.dot(oh, before, preferred_element_type=F32)
    rank_ref[0:1, :] = jnp.sum(jnp.where(oh1, pos, 0.0), axis=0, keepdims=True).astype(jnp.int32)
    rank_ref[1:2, :] = jnp.sum(jnp.where(oh2, pos, 0.0), axis=0, keepdims=True).astype(jnp.int32)
    total = base_ref[...] + jnp.dot(oh, ones, preferred_element_type=F32)
    base_ref[...] = total
    cnt_ref[...] = total[:, 0:128].astype(jnp.int32)


def moe_router(h, w_router, layer):
    n = h.shape[0]
    t = ROUTE_T
    wrt = jnp.swapaxes(w_router, 1, 2)
    pair = pl.BlockSpec((TOP_K, t), lambda i: (0, i))
    return pl.pallas_call(
        _router_kernel,
        out_shape=(
            jax.ShapeDtypeStruct((TOP_K, n), jnp.int32),
            jax.ShapeDtypeStruct((TOP_K, n), F32),
            jax.ShapeDtypeStruct((TOP_K, n), jnp.int32),
            jax.ShapeDtypeStruct((N_EXPERTS, 128), jnp.int32),
        ),
        grid=(n // t,),
        in_specs=[
            pl.BlockSpec((t, D), lambda i: (i, 0)),
            pl.BlockSpec((None, N_EXPERTS, D), lambda i: (layer, 0, 0)),
        ],
        out_specs=(pair, pair, pair, pl.BlockSpec((N_EXPERTS, 128), lambda i: (0, 0))),
        scratch_shapes=[pltpu.VMEM((N_EXPERTS, t), F32)],
        compiler_params=_params(1, 16 << 20),
        name="moe_router",
    )(h, wrt)


def _row_copy(src_ref, src_row, dst_ref, dst_row, sem):
    return pltpu.make_async_copy(src_ref.at[pl.ds(src_row, 1)], dst_ref.at[pl.ds(dst_row, 1)], sem)


DISPATCH_T = 512
ISSUE_UNROLL = 8


def _dispatch_kernel(dest_ref, pad_ref, nused_ref, h_ref, xb_hbm, zero_ref, sem, zsem, *, n_tokens, n_blocks):
    i = pl.program_id(0)
    t = DISPATCH_T
    rows = EXPERT_BLOCK

    @pl.when(i == 0)
    def _():
        zero_ref[...] = jnp.zeros(zero_ref.shape, F32)

        def zero_row(r, carry):
            _row_copy(zero_ref, 0, xb_hbm, r, zsem).start()
            return carry

        def wait_row(r, carry):
            _row_copy(zero_ref, 0, xb_hbm, 0, zsem).wait()
            return carry

        def block_copy(blk):
            return pltpu.make_async_copy(zero_ref, xb_hbm.at[pl.ds(pl.multiple_of(blk * rows, rows), rows)], zsem)

        def zero_block(blk, carry):
            block_copy(blk).start()
            return carry

        def wait_block(blk, carry):
            block_copy(0).wait()
            return carry

        for e in range(N_EXPERTS):
            lax.fori_loop(pad_ref[e], pad_ref[N_EXPERTS + e], zero_row, 0)
        lax.fori_loop(nused_ref[0], n_blocks, zero_block, 0)
        for e in range(N_EXPERTS):
            lax.fori_loop(pad_ref[e], pad_ref[N_EXPERTS + e], wait_row, 0)
        lax.fori_loop(nused_ref[0], n_blocks, wait_block, 0)

    for r in range(t):
        tok = i * t + r
        _row_copy(h_ref, r, xb_hbm, dest_ref[tok], sem).start(priority=0)
        _row_copy(h_ref, r, xb_hbm, dest_ref[n_tokens + tok], sem).start(priority=1)
    for _ in range(TOP_K):
        pltpu.make_async_copy(h_ref, xb_hbm.at[pl.ds(0, t)], sem).wait()


def moe_dispatch(h, dest, pad_rows, nused, n_blocks):
    n, d = h.shape
    t = DISPATCH_T
    rows = n_blocks * EXPERT_BLOCK
    kernel = functools.partial(_dispatch_kernel, n_tokens=n, n_blocks=n_blocks)
    return pl.pallas_call(
        kernel,
        out_shape=jax.ShapeDtypeStruct((rows, d), F32),
        grid_spec=pltpu.PrefetchScalarGridSpec(
            num_scalar_prefetch=3,
            grid=(n // t,),
            in_specs=[pl.BlockSpec((t, d), lambda i, *_: (i, 0))],
            out_specs=pl.BlockSpec(memory_space=pl.ANY),
            scratch_shapes=[pltpu.VMEM((EXPERT_BLOCK, d), F32), pltpu.SemaphoreType.DMA(()),
                            pltpu.SemaphoreType.DMA(())],
        ),
        compiler_params=_params(1, 24 << 20),
        name="moe_dispatch",
    )(dest, pad_rows, nused, h)


def _moe_up_kernel(be_ref, nused_ref, first_ref, nxte_ref, last_ref, x_ref, wg_hbm, wu_hbm, o_ref,
                   wgb_ref, wub_ref, sg_ref, su_ref, sems, *, layer, tn, n_col_tiles):
    j = pl.program_id(0)
    i = pl.program_id(1)

    def fetch(e, jj):
        cols = pl.ds(pl.multiple_of(jj * tn, 128), tn)
        return (pltpu.make_async_copy(wg_hbm.at[layer, e, :, cols], sg_ref, sems.at[0]),
                pltpu.make_async_copy(wu_hbm.at[layer, e, :, cols], su_ref, sems.at[1]))

    @pl.when(first_ref[i] == 1)
    def _():
        @pl.when(jnp.logical_and(j == 0, i == 0))
        def _():
            for c in fetch(be_ref[0], 0):
                c.start()

        for c in fetch(be_ref[i], j):
            c.wait()
        _cast_rows(sg_ref, wgb_ref)
        _cast_rows(su_ref, wub_ref)
        nj = j + last_ref[i]

        @pl.when(nj < n_col_tiles)
        def _():
            for c in fetch(nxte_ref[i], nj):
                c.start(priority=PREFETCH_PRIORITY)

    @pl.when(i < nused_ref[0])
    def _():
        x = x_ref[...].astype(BF16)
        g = jnp.dot(x, wgb_ref[...], preferred_element_type=F32)
        u = jnp.dot(x, wub_ref[...], preferred_element_type=F32)
        o_ref[...] = (jax.nn.silu(g) * u).astype(o_ref.dtype)

    @pl.when(i >= nused_ref[0])
    def _():
        o_ref[...] = jnp.zeros(o_ref.shape, o_ref.dtype)


def _group_schedule(block_e):
    nb = block_e.shape[0]
    idx = jnp.arange(nb, dtype=jnp.int32)
    first = jnp.concatenate([jnp.ones((1,), bool), block_e[1:] != block_e[:-1]])
    later_first = jnp.logical_and(first[None, :], idx[None, :] > idx[:, None])
    nxt_idx = jnp.min(jnp.where(later_first, idx[None, :], nb), axis=1)
    last = nxt_idx >= nb
    nxt_idx = jnp.where(last, 0, nxt_idx)
    nxt_e = jnp.sum(jnp.where(idx[None, :] == nxt_idx[:, None], block_e[None, :], 0), axis=1)
    return first.astype(jnp.int32), nxt_e.astype(jnp.int32), last.astype(jnp.int32)


def moe_up(xb, w_gate, w_up, layer, block_e, nused, schedule, *, tn=1408):
    p, k = xb.shape
    rows = EXPERT_BLOCK
    f = w_gate.shape[-1]
    assert f % tn == 0 and tn % 128 == 0
    first, nxt_e, last = schedule
    kernel = functools.partial(_moe_up_kernel, layer=layer, tn=tn, n_col_tiles=f // tn)
    hbm = pl.BlockSpec(memory_space=pl.ANY)
    return pl.pallas_call(
        kernel,
        out_shape=jax.ShapeDtypeStruct((p, f), BF16),
        grid_spec=pltpu.PrefetchScalarGridSpec(
            num_scalar_prefetch=5,
            grid=(f // tn, p // rows),
            in_specs=[pl.BlockSpec((rows, k), lambda j, i, *_: (i, 0)), hbm, hbm],
            out_specs=pl.BlockSpec((rows, tn), lambda j, i, *_: (i, j)),
            scratch_shapes=[pltpu.VMEM((k, tn), BF16), pltpu.VMEM((k, tn), BF16),
                            pltpu.VMEM((k, tn), F32), pltpu.VMEM((k, tn), F32),
                            pltpu.SemaphoreType.DMA((2,))],
        ),
        compiler_params=_params(2, 52 << 20),
        name="moe_up",
    )(block_e, nused, first, nxt_e, last, xb, w_gate, w_up)


def _moe_down_kernel(be_ref, nused_ref, first_ref, nxte_ref, last_ref, x_ref, w_hbm, o_ref,
                     wb_ref, st_ref, sem, *, layer):
    i = pl.program_id(0)

    def fetch(e):
        return pltpu.make_async_copy(w_hbm.at[layer, e], st_ref, sem)

    @pl.when(first_ref[i] == 1)
    def _():
        @pl.when(i == 0)
        def _():
            fetch(be_ref[0]).start()

        fetch(be_ref[i]).wait()
        _cast_rows(st_ref, wb_ref)

        @pl.when(last_ref[i] == 0)
        def _():
            fetch(nxte_ref[i]).start(priority=PREFETCH_PRIORITY)

    @pl.when(i < nused_ref[0])
    def _():
        o_ref[...] = jnp.dot(x_ref[...], wb_ref[...], preferred_element_type=F32)

    @pl.when(i >= nused_ref[0])
    def _():
        o_ref[...] = jnp.zeros(o_ref.shape, o_ref.dtype)


def moe_down(hb, w_down, layer, block_e, nused, schedule):
    p, k = hb.shape
    rows = EXPERT_BLOCK
    n = w_down.shape[-1]
    first, nxt_e, last = schedule
    kernel = functools.partial(_moe_down_kernel, layer=layer)
    return pl.pallas_call(
        kernel,
        out_shape=jax.ShapeDtypeStruct((p, n), F32),
        grid_spec=pltpu.PrefetchScalarGridSpec(
            num_scalar_prefetch=5,
            grid=(p // rows,),
            in_specs=[pl.BlockSpec((rows, k), lambda i, *_: (i, 0)), pl.BlockSpec(memory_space=pl.ANY)],
            out_specs=pl.BlockSpec((rows, n), lambda i, *_: (i, 0)),
            scratch_shapes=[pltpu.VMEM((k, n), BF16), pltpu.VMEM((k, n), F32), pltpu.SemaphoreType.DMA(())],
        ),
        compiler_params=_params(1, 50 << 20),
        name="moe_down",
    )(block_e, nused, first, nxt_e, last, hb, w_down)


COMBINE_T = 256


COMBINE_GROUP = 32


def _combine_kernel(dest_ref, y_hbm, wt_ref, h_ref, g_ref, b_ref, of_ref, ob_ref,
                    buf_ref, sems, *, n_tokens):
    i = pl.program_id(0)
    nb = pl.num_programs(0)
    t = COMBINE_T
    slot = i % 2

    def issue(block, to_slot, r):
        tok = block * t + r
        _row_copy(y_hbm, dest_ref[tok], buf_ref.at[to_slot, 0], r, sems.at[to_slot]).start(priority=0)
        _row_copy(y_hbm, dest_ref[n_tokens + tok], buf_ref.at[to_slot, 1], r, sems.at[to_slot]).start(priority=1)

    def drain(from_slot):
        for k in range(TOP_K):
            pltpu.make_async_copy(y_hbm.at[pl.ds(0, t)], buf_ref.at[from_slot, k], sems.at[from_slot]).wait()

    @pl.when(i == 0)
    def _():
        def body(r, carry):
            issue(0, 0, r)
            return carry

        lax.fori_loop(0, t, body, 0, unroll=ISSUE_UNROLL)

    drain(slot)
    nxt = jnp.minimum(i + 1, nb - 1)
    for r0 in range(0, t, COMBINE_GROUP):
        sl = slice(r0, r0 + COMBINE_GROUP)
        ff = buf_ref[slot, 0, sl, :] * wt_ref[sl, 0:1] + buf_ref[slot, 1, sl, :] * wt_ref[sl, 1:2]
        y = _layer_norm(DN_ALPHA * h_ref[sl, :] + ff, g_ref[...], b_ref[...])
        of_ref[sl, :] = y
        ob_ref[sl, :] = y.astype(BF16)
        for r in range(r0, r0 + COMBINE_GROUP):
            issue(nxt, 1 - slot, r)

    @pl.when(i == nb - 1)
    def _():
        drain(1 - slot)


def moe_combine_ln(yb, dest, wts_t, h, ln_g, ln_b, layer, which):
    n, d = h.shape
    t = COMBINE_T
    row = pl.BlockSpec((t, d), lambda i, ds: (i, 0))
    par = pl.BlockSpec((None, None, 1, d), lambda i, ds: (layer, which, 0, 0))
    kernel = functools.partial(_combine_kernel, n_tokens=n)
    return pl.pallas_call(
        kernel,
        out_shape=(jax.ShapeDtypeStruct((n, d), F32), jax.ShapeDtypeStruct((n, d), BF16)),
        grid_spec=pltpu.PrefetchScalarGridSpec(
            num_scalar_prefetch=1,
            grid=(n // t,),
            in_specs=[
                pl.BlockSpec(memory_space=pl.ANY),
                pl.BlockSpec((t, TOP_K), lambda i, ds: (i, 0)),
                row, par, par,
            ],
            out_specs=(row, row),
            scratch_shapes=[pltpu.VMEM((2, TOP_K, t, d), F32), pltpu.SemaphoreType.DMA((2,))],
        ),
        compiler_params=_params(1, 48 << 20),
        name="moe_combine_ln",
    )(dest, yb, wts_t, h, ln_g, ln_b)


def moe_layer(h, hb, w_router, w_gate, w_up, w_down, ln_g, ln_b, layer, moe_idx):
    n = h.shape[0]
    m = n * TOP_K
    n_blocks = (m + N_EXPERTS * (EXPERT_BLOCK - 1) + EXPERT_BLOCK - 1) // EXPERT_BLOCK
    e, wts, rank, cnt = moe_router(h, w_router, moe_idx)
    counts = cnt[:, 0]
    padded = (counts + EXPERT_BLOCK - 1) // EXPERT_BLOCK * EXPERT_BLOCK
    pad_ends = jnp.cumsum(padded)
    pad_starts = pad_ends - padded
    expert_ids = jnp.arange(N_EXPERTS, dtype=jnp.int32)[:, None, None]
    dest = jnp.sum(jnp.where(e[None] == expert_ids, pad_starts[:, None, None], 0), axis=0) + rank
    block_start = jnp.arange(n_blocks, dtype=jnp.int32) * EXPERT_BLOCK
    block_e = jnp.minimum(jnp.sum(block_start[:, None] >= pad_ends[None, :], axis=1), N_EXPERTS - 1).astype(jnp.int32)
    nused = (pad_ends[-1:] // EXPERT_BLOCK).astype(jnp.int32)

    dest = dest.reshape(-1)
    pad_rows = jnp.concatenate([pad_starts + counts, pad_ends]).astype(jnp.int32)
    schedule = _group_schedule(block_e)
    xb = moe_dispatch(h, dest, pad_rows, nused, n_blocks)
    hid = moe_up(xb, w_gate, w_up, moe_idx, block_e, nused, schedule)
    yb = moe_down(hid, w_down, moe_idx, block_e, nused, schedule)
    return moe_combine_ln(yb, dest, wts.T, h, ln_g, ln_b, layer, 2)


def kernel(x, mem, rel_table, mix_w_in, gmlp_ln_g, gmlp_ln_b, gmlp_w_s, gmlp_b_s, pool_w, pool_scale, mix_w_out, diff_w_qkv, diff_lam_q1, diff_lam_k1, diff_lam_q2, diff_lam_k2, diff_subln_g, diff_w_o, xa_w_q, xa_w_kv, xa_w_o, ffn_w_gate, ffn_w_up, ffn_w_down, moe_w_router, moe_w_gate, moe_w_up, moe_w_down, ln_g, ln_b):
    batch, seq_len, d = x.shape
    mem_len = mem.shape[1]
    n = batch * seq_len
    h = x.reshape(n, d)
    hb = h
    memf = mem.reshape(batch * mem_len, d)
    ln_g4 = ln_g.reshape(DEPTH, 3, 1, d)
    ln_b4 = ln_b.reshape(DEPTH, 3, 1, d)
    lam_params = jnp.stack([diff_lam_q1, diff_lam_k1, diff_lam_q2, diff_lam_k2], axis=1)
    bias_tiles = attention_bias_tiles(rel_table)
    kv_all = matmul_all_layers(memf, xa_w_kv, tn=1024, out_dtype=BF16)
    xa_qk, xa_vo = cross_attention_folds(kv_all, xa_w_q, xa_w_o, batch, mem_len)

    for layer in range(DEPTH):
        i = layer // 2
        if layer % 2 == 0:
            hm = matmul(hb, mix_w_in, i, tm=512, tn=MIX_IN // 2, out_dtype=F32)
            mixed = gmlp_pool_mixer(hm, seq_len, gmlp_ln_g, gmlp_ln_b, gmlp_w_s, gmlp_b_s, pool_w,
                                    pool_scale, i)
            h, hb = matmul_residual_ln(mixed, mix_w_out, i, h, ln_g4, ln_b4, layer, 0)
        else:
            lam_init = 0.8 - 0.6 * math.exp(-0.3 * layer)
            qkv = matmul(hb, diff_w_qkv, i, tm=1024, tn=1024, out_dtype=BF16,
                         scaled_cols=D, scale=LOG2E * C_HEAD_DIM ** -0.5)
            att = diff_attention(qkv, bias_tiles, lam_params, diff_subln_g, i, lam_init, batch, seq_len)
            h, hb = matmul_residual_ln(att, diff_w_o, i, h, ln_g4, ln_b4, layer, 0)

        h, hb = cross_attention_residual_ln(hb, h, xa_qk, xa_vo, ln_g4, ln_b4, layer, 1, seq_len, mem_len)

        if layer % 2 == 0:
            hid = swiglu_up(hb, ffn_w_gate, ffn_w_up, i, tm=1024, tn=512)
            h, hb = matmul_kacc_residual_ln(hid, ffn_w_down, i, h, ln_g4, ln_b4, layer, 2,
                                            tm=1024, tk=512, single_buffer_rows=True)
        else:
            h, hb = moe_layer(h, hb, moe_w_router, moe_w_gate, moe_w_up, moe_w_down, ln_g4, ln_b4,
                              layer, i)
    return h.reshape(batch, seq_len, d)
```
